```python
import math
import jax
import jax.numpy as jnp
from jax import lax
import numpy as np

D_MODEL = 2048
BATCH = 1
SEQ = 16384
DEPTH = 4

MEM_LEN = 256
EPS = 1e-6
DN_HEADS = 8
DN_DK = 128
DN_DV = 128
DN_CONV = 4
DN_CHUNK = 64
SWA_HEADS = 8
SWA_KV_HEADS = 2
SWA_DH = 64
WINDOW = 128
SB_HEADS = 4
SB_DH = 128
SB_BLOCK = 128
X_HEADS = 4
X_DH = 128
D_FF = 4096
FFN_CONV = 3
N_BRANCH = 3

DN_QK_WIDTH = DN_HEADS * DN_DK
DN_WIDTH = DN_HEADS * DN_DV
SWA_WIDTH = SWA_HEADS * SWA_DH
SWA_KV_WIDTH = SWA_KV_HEADS * SWA_DH
SB_WIDTH = SB_HEADS * SB_DH
X_WIDTH = X_HEADS * X_DH
IN_SIZES = (2 * DN_QK_WIDTH + DN_WIDTH, DN_WIDTH, DN_HEADS, DN_HEADS, SWA_WIDTH, 2 * SWA_KV_WIDTH, 3 * SB_WIDTH, N_BRANCH * D_MODEL)
IN_COLS = sum(IN_SIZES)

kernel_name = 'hybrid_gdn_swa_stickbreak_decoder'


def rmsnorm(x, g):
    xf = x.astype(jnp.float32)
    y = xf * lax.rsqrt(jnp.mean(xf * xf, axis=-1, keepdims=True) + EPS)
    return (y * g.astype(jnp.float32)).astype(x.dtype)


def l2norm(x):
    xf = x.astype(jnp.float32)
    return xf * lax.rsqrt(jnp.sum(xf * xf, axis=-1, keepdims=True) + EPS)


def causal_dwconv(x, w):
    width = w.shape[0]
    seq = x.shape[1]
    xp = jnp.pad(x, ((0, 0), (width - 1, 0), (0, 0)))
    out = xp[:, 0:seq] * w[0]
    for i in range(1, width):
        out = out + xp[:, i:i + seq] * w[i]
    return out


def split_cols(t, sizes):
    offsets = [int(o) for o in np.cumsum(sizes)[:-1]]
    return jnp.split(t, offsets, axis=-1)


def alibi_slopes(n):
    return jnp.exp2(-8.0 * (jnp.arange(n, dtype=jnp.float32) + 1.0) / n)


def gated_delta_rule(q, k, v, g, beta):
    bsz, seq, heads, dk = q.shape
    dv = v.shape[-1]
    c = DN_CHUNK
    n = seq // c
    f32 = jnp.float32

    def chunks(t):
        return t.astype(f32).reshape(bsz, n, c, heads, -1).transpose(0, 1, 3, 2, 4)

    qc = chunks(q) * (dk ** -0.5)
    kc = chunks(k)
    vc = chunks(v)
    gc = jnp.cumsum(g.astype(f32).reshape(bsz, n, c, heads).transpose(0, 1, 3, 2), axis=-1)
    bc = beta.astype(f32).reshape(bsz, n, c, heads).transpose(0, 1, 3, 2)[..., None]
    causal = jnp.tril(jnp.ones((c, c), dtype=bool))
    strict = jnp.tril(jnp.ones((c, c), dtype=bool), -1)
    diff = gc[..., :, None] - gc[..., None, :]
    decay = jnp.where(causal, jnp.exp(jnp.where(causal, diff, 0.0)), 0.0)
    kb = kc * bc
    kk = jnp.einsum('bnhid,bnhjd->bnhij', kb, kc) * decay
    t_mat = jnp.eye(c, dtype=f32) + jnp.where(strict, kk, 0.0)
    u = lax.linalg.triangular_solve(t_mat, vc * bc, left_side=True, lower=True, unit_diagonal=True)
    w = lax.linalg.triangular_solve(t_mat, kb * jnp.exp(gc)[..., None], left_side=True, lower=True, unit_diagonal=True)
    qk = jnp.einsum('bnhid,bnhjd->bnhij', qc, kc) * decay
    q_dec = qc * jnp.exp(gc)[..., None]
    k_dec = kc * jnp.exp(gc[..., -1:] - gc)[..., None]
    g_last = jnp.exp(gc[..., -1])

    def step(state, xs):
        qk_n, qd_n, kd_n, u_n, w_n, gl_n = xs
        v_new = u_n - jnp.einsum('bhcd,bhde->bhce', w_n, state)
        o = jnp.einsum('bhcd,bhde->bhce', qd_n, state) + jnp.einsum('bhij,bhje->bhie', qk_n, v_new)
        state = state * gl_n[..., None, None] + jnp.einsum('bhcd,bhce->bhde', kd_n, v_new)
        return state, o

    xs = tuple(jnp.moveaxis(t, 1, 0) for t in (qk, q_dec, k_dec, u, w, g_last))
    state0 = jnp.zeros((bsz, heads, dk, dv), f32)
    _, o = lax.scan(step, state0, xs)
    return o.transpose(1, 0, 3, 2, 4).reshape(bsz, seq, heads, dv)


def sliding_window_gqa(q, k, v, sinks):
    bsz, seq = q.shape[:2]
    n = seq // WINDOW
    grp = SWA_HEADS // SWA_KV_HEADS
    qb = q.reshape(bsz, n, WINDOW, SWA_KV_HEADS, grp, SWA_DH)

    def band(t):
        tb = t.reshape(bsz, n, WINDOW, SWA_KV_HEADS, SWA_DH)
        prev = jnp.pad(tb, ((0, 0), (1, 0), (0, 0), (0, 0), (0, 0)))[:, :-1]
        return jnp.concatenate([prev, tb], axis=2)

    kb = band(k)
    vb = band(v)
    s = jnp.einsum('bnqhgd,bnkhd->bnhgqk', qb, kb).astype(jnp.float32) * (SWA_DH ** -0.5)
    qi = jnp.arange(WINDOW)[:, None]
    kj = jnp.arange(2 * WINDOW)[None, :]
    dist = qi + WINDOW - kj
    blk = jnp.arange(n)[:, None, None]
    valid = (dist >= 0) & (dist < WINDOW) & (blk * WINDOW + kj - WINDOW >= 0)
    slopes = alibi_slopes(SWA_HEADS).reshape(SWA_KV_HEADS, grp)[:, :, None, None]
    s = s - slopes * dist.astype(jnp.float32)
    s = jnp.where(valid[None, :, None, None], s, -jnp.inf)
    sink = sinks.astype(jnp.float32).reshape(SWA_KV_HEADS, grp)[:, :, None, None]
    m = jnp.maximum(jnp.max(s, axis=-1, keepdims=True), sink)
    p = jnp.exp(s - m)
    p = p / (jnp.sum(p, axis=-1, keepdims=True) + jnp.exp(sink - m))
    o = jnp.einsum('bnhgqk,bnkhd->bnqhgd', p.astype(vb.dtype), vb)
    return o.reshape(bsz, seq, SWA_WIDTH)


def stick_breaking_attention(q, k, v):
    bsz, seq, heads, dh = q.shape
    n = seq // SB_BLOCK
    f32 = jnp.float32
    tri_incl = jnp.tril(jnp.ones((SB_BLOCK, SB_BLOCK), f32))
    tri_blocks = jnp.tril(jnp.ones((n, n), f32), -1)
    scale = dh ** -0.5
    outs = []
    for i in range(n):
        nk = i + 1
        length = nk * SB_BLOCK
        q_blk = q[:, i * SB_BLOCK:(i + 1) * SB_BLOCK]
        k_blk = k[:, :length]
        v_blk = v[:, :length]
        z = jnp.einsum('bqhd,bshd->bhqs', q_blk, k_blk).astype(f32) * scale
        qpos = i * SB_BLOCK + jnp.arange(SB_BLOCK)
        valid = jnp.arange(length)[None, :] < qpos[:, None]
        zm = jnp.where(valid, z, -jnp.inf)
        l = jax.nn.log_sigmoid(-zm).reshape(bsz, heads, SB_BLOCK, nk, SB_BLOCK)
        rev_in = jnp.einsum('bhqnj,js->bhqns', l, tri_incl)
        after = jnp.einsum('bhqm,mn->bhqn', jnp.sum(l, axis=-1), tri_blocks[:nk, :nk])
        rev = (rev_in + after[..., None]).reshape(bsz, heads, SB_BLOCK, length)
        a = jnp.exp(zm + rev)
        outs.append(jnp.einsum('bhqs,bshd->bqhd', a.astype(v.dtype), v_blk))
    o = jnp.concatenate(outs, axis=1)
    return o.reshape(bsz, seq, heads * dh)


def hybrid_mixer(xn, w_in, dn_conv, dn_a_log, dn_dt_bias, dn_norm, swa_sinks, w_br_dn, w_br_swa, w_br_sb, w_o):
    bsz, seq, _ = xn.shape
    proj = xn @ w_in
    dn_qkv, dn_z, dn_a, dn_b, swa_q, swa_kv, sb_qkv, gates = split_cols(proj, IN_SIZES)
    dn_qkv = jax.nn.silu(causal_dwconv(dn_qkv, dn_conv))
    q, k, v = split_cols(dn_qkv, (DN_QK_WIDTH, DN_QK_WIDTH, DN_WIDTH))
    q = l2norm(q.reshape(bsz, seq, DN_HEADS, DN_DK))
    k = l2norm(k.reshape(bsz, seq, DN_HEADS, DN_DK))
    v = v.reshape(bsz, seq, DN_HEADS, DN_DV)
    g = -jnp.exp(dn_a_log.astype(jnp.float32)) * jax.nn.softplus(dn_a.astype(jnp.float32) + dn_dt_bias.astype(jnp.float32))
    beta = jax.nn.sigmoid(dn_b.astype(jnp.float32))
    o_dn = gated_delta_rule(q, k, v, g, beta)
    o_dn = rmsnorm(o_dn, dn_norm) * jax.nn.silu(dn_z.reshape(bsz, seq, DN_HEADS, DN_DV).astype(jnp.float32))
    y_dn = o_dn.reshape(bsz, seq, DN_WIDTH).astype(xn.dtype)
    swa_k, swa_v = split_cols(swa_kv, (SWA_KV_WIDTH, SWA_KV_WIDTH))
    y_swa = sliding_window_gqa(swa_q.reshape(bsz, seq, SWA_HEADS, SWA_DH), swa_k.reshape(bsz, seq, SWA_KV_HEADS, SWA_DH), swa_v.reshape(bsz, seq, SWA_KV_HEADS, SWA_DH), swa_sinks)
    sb_q, sb_k, sb_v = split_cols(sb_qkv, (SB_WIDTH, SB_WIDTH, SB_WIDTH))
    y_sb = stick_breaking_attention(sb_q.reshape(bsz, seq, SB_HEADS, SB_DH), sb_k.reshape(bsz, seq, SB_HEADS, SB_DH), sb_v.reshape(bsz, seq, SB_HEADS, SB_DH))
    gate = jax.nn.sigmoid(gates).reshape(bsz, seq, N_BRANCH, D_MODEL)
    merged = gate[:, :, 0] * (y_dn @ w_br_dn) + gate[:, :, 1] * (y_swa @ w_br_swa) + gate[:, :, 2] * (y_sb @ w_br_sb)
    return merged @ w_o


def memory_cross_attention(hn, mem_n, w_xq, w_xkv, w_xo):
    bsz, seq, _ = hn.shape
    q = (hn @ w_xq).reshape(bsz, seq, X_HEADS, X_DH)
    kv = (mem_n @ w_xkv).reshape(bsz, mem_n.shape[1], 2, X_HEADS, X_DH)
    s = jnp.einsum('bqhd,bmhd->bhqm', q, kv[:, :, 0]).astype(jnp.float32) * (X_DH ** -0.5)
    p = jax.nn.softmax(s, axis=-1)
    o = jnp.einsum('bhqm,bmhd->bqhd', p.astype(kv.dtype), kv[:, :, 1])
    return o.reshape(bsz, seq, X_WIDTH) @ w_xo


def conv_ffn(hn, w_up, ffn_conv, w_down):
    up = causal_dwconv(hn @ w_up, ffn_conv)
    gate, val = jnp.split(up, 2, axis=-1)
    return (jax.nn.silu(gate) * val) @ w_down


def setup_inputs(seed: int = 0) -> dict:
    key = jax.random.key(seed)
    ks = jax.random.split(key, 24)
    f32 = jnp.float32

    def dense(k, shape, fan_in):
        return jax.random.normal(k, shape, f32) * (fan_in ** -0.5)

    def gain(k, shape):
        return 1.0 + 0.02 * jax.random.normal(k, shape, f32)

    dt = jnp.exp(jax.random.uniform(ks[6], (DEPTH, DN_HEADS), f32, minval=math.log(1e-3), maxval=math.log(0.1)))
    return {
        'x': jax.random.normal(ks[0], (BATCH, SEQ, D_MODEL), f32),
        'mem': jax.random.normal(ks[1], (BATCH, MEM_LEN, D_MODEL), f32),
        'norm_mix': gain(ks[2], (DEPTH, D_MODEL)),
        'w_in': dense(ks[3], (DEPTH, D_MODEL, IN_COLS), D_MODEL),
        'dn_conv': dense(ks[4], (DEPTH, DN_CONV, 2 * DN_QK_WIDTH + DN_WIDTH), DN_CONV),
        'dn_a_log': jnp.log(jax.random.uniform(ks[5], (DEPTH, DN_HEADS), f32, minval=1.0, maxval=16.0)),
        'dn_dt_bias': dt + jnp.log(-jnp.expm1(-dt)),
        'dn_norm': gain(ks[7], (DEPTH, DN_DV)),
        'swa_sinks': 0.5 * jax.random.normal(ks[8], (DEPTH, SWA_HEADS), f32),
        'w_br_dn': dense(ks[9], (DEPTH, DN_WIDTH, D_MODEL), DN_WIDTH),
        'w_br_swa': dense(ks[10], (DEPTH, SWA_WIDTH, D_MODEL), SWA_WIDTH),
        'w_br_sb': dense(ks[11], (DEPTH, SB_WIDTH, D_MODEL), SB_WIDTH),
        'w_o': dense(ks[12], (DEPTH, D_MODEL, D_MODEL), D_MODEL),
        'norm_xattn': gain(ks[13], (DEPTH, D_MODEL)),
        'norm_mem': gain(ks[14], (DEPTH, D_MODEL)),
        'w_xq': dense(ks[15], (DEPTH, D_MODEL, X_WIDTH), D_MODEL),
        'w_xkv': dense(ks[16], (DEPTH, D_MODEL, 2 * X_WIDTH), D_MODEL),
        'w_xo': dense(ks[17], (DEPTH, X_WIDTH, D_MODEL), X_WIDTH),
        'norm_ffn': gain(ks[18], (DEPTH, D_MODEL)),
        'w_up': dense(ks[19], (DEPTH, D_MODEL, 2 * D_FF), D_MODEL),
        'ffn_conv': dense(ks[20], (DEPTH, FFN_CONV, 2 * D_FF), FFN_CONV),
        'w_down': dense(ks[21], (DEPTH, D_FF, D_MODEL), D_FF),
        'norm_final': gain(ks[22], (D_MODEL,)),
    }


def reference(x, mem, norm_mix, w_in, dn_conv, dn_a_log, dn_dt_bias, dn_norm, swa_sinks, w_br_dn, w_br_swa, w_br_sb, w_o, norm_xattn, norm_mem, w_xq, w_xkv, w_xo, norm_ffn, w_up, ffn_conv, w_down, norm_final):
    h = x
    for l in range(DEPTH):
        h = h + hybrid_mixer(rmsnorm(h, norm_mix[l]), w_in[l], dn_conv[l], dn_a_log[l], dn_dt_bias[l], dn_norm[l], swa_sinks[l], w_br_dn[l], w_br_swa[l], w_br_sb[l], w_o[l])
        h = h + memory_cross_attention(rmsnorm(h, norm_xattn[l]), rmsnorm(mem, norm_mem[l]), w_xq[l], w_xkv[l], w_xo[l])
        h = h + conv_ffn(rmsnorm(h, norm_ffn[l]), w_up[l], ffn_conv[l], w_down[l])
    return rmsnorm(h, norm_final)
```

```python
import functools

import jax
import jax.numpy as jnp
from jax import lax
from jax.experimental import pallas as pl
from jax.experimental.pallas import tpu as pltpu

F32 = jnp.float32
BF16 = jnp.bfloat16
HIGHEST = lax.Precision.HIGHEST

D_MODEL = 2048
EPS = 1e-6
DN_HEADS = 8
DN_DK = 128
DN_DV = 128
DN_CONV = 4
DN_CHUNK = 64
SWA_HEADS = 8
SWA_KV_HEADS = 2
SWA_DH = 64
WINDOW = 128
SB_HEADS = 4
SB_DH = 128
X_HEADS = 4
X_DH = 128
D_FF = 4096
FFN_CONV = 3
N_BRANCH = 3

DN_QK_WIDTH = DN_HEADS * DN_DK
DN_WIDTH = DN_HEADS * DN_DV
SWA_WIDTH = SWA_HEADS * SWA_DH
SWA_KV_WIDTH = SWA_KV_HEADS * SWA_DH
SB_WIDTH = SB_HEADS * SB_DH
X_WIDTH = X_HEADS * X_DH

LANES = 128
SUBLANES = 8
VMEM_LIMIT = 56 * 1024 * 1024

PROJ_COLS = N_BRANCH * D_MODEL + 3 * DN_QK_WIDTH + DN_WIDTH + SWA_WIDTH + 3 * SB_WIDTH + 2 * SWA_KV_WIDTH
COL_DN_Q = N_BRANCH * D_MODEL
COL_DN_Z = COL_DN_Q + 3 * DN_QK_WIDTH
COL_SWA_Q = COL_DN_Z + DN_WIDTH
COL_SB_Q = COL_SWA_Q + SWA_WIDTH
COL_SWA_KV = COL_SB_Q + 3 * SB_WIDTH
AB_COLS = LANES

NEG_BIG = -1e30


def _cparams(semantics):
    return pltpu.CompilerParams(dimension_semantics=semantics, vmem_limit_bytes=VMEM_LIMIT)


def _dot(a, b):
    return jnp.dot(a, b, preferred_element_type=F32)


def _dot_nt(a, b, precision=None):
    return lax.dot_general(a, b, (((1,), (1,)), ((), ())), preferred_element_type=F32, precision=precision)


def _dot_hi(a, b):
    return jnp.dot(a, b, preferred_element_type=F32, precision=HIGHEST)


def _sigmoid(x):
    return 1.0 / (1.0 + jnp.exp(-x))


def _softplus(x):
    return jnp.maximum(x, 0.0) + jnp.log1p(jnp.exp(-jnp.abs(x)))


def _rms_rows(x, g):
    ms = jnp.mean(x * x, axis=-1, keepdims=True)
    return x * lax.rsqrt(ms + EPS) * g


def _in_proj_kernel(h_ref, g_ref, w_ref, wab_ref, o_ref, ab_ref, xn_ref):
    @pl.when(pl.program_id(1) == 0)
    def _():
        xn = _rms_rows(h_ref[...], g_ref[...]).astype(BF16)
        xn_ref[...] = xn
        ab_ref[...] = _dot(xn, wab_ref[...])

    o_ref[...] = _dot(xn_ref[...], w_ref[...]).astype(o_ref.dtype)


def _in_proj(h, g, w, wab, tm, tn):
    s = h.shape[0]
    n = w.shape[1]
    return pl.pallas_call(
        _in_proj_kernel,
        grid=(s // tm, n // tn),
        in_specs=[
            pl.BlockSpec((tm, D_MODEL), lambda i, j: (i, 0)),
            pl.BlockSpec((1, D_MODEL), lambda i, j: (0, 0)),
            pl.BlockSpec((D_MODEL, tn), lambda i, j: (0, j)),
            pl.BlockSpec((D_MODEL, AB_COLS), lambda i, j: (0, 0)),
        ],
        out_specs=[
            pl.BlockSpec((tm, tn), lambda i, j: (i, j)),
            pl.BlockSpec((tm, AB_COLS), lambda i, j: (i, 0)),
        ],
        out_shape=[jax.ShapeDtypeStruct((s, n), BF16), jax.ShapeDtypeStruct((s, AB_COLS), F32)],
        scratch_shapes=[pltpu.VMEM((tm, D_MODEL), BF16)],
        compiler_params=_cparams(("parallel", "arbitrary")),
        name="in_proj",
    )(h, g, w, wab)


def _dn_kernel(q_ref, k_ref, v_ref, z_ref, ab_ref, cw_ref, alog_ref, dtb_ref, nrm_ref, cum_ref, eg_ref, eb_ref,
               o_ref, xbuf, qs, ks, vs, gce, gle, ble, state):
    i = pl.program_id(0)
    t = q_ref.shape[0]
    c = DN_CHUNK
    halo = SUBLANES

    @pl.when(i == 0)
    def _():
        xbuf[0:halo, :] = jnp.zeros((halo, xbuf.shape[1]), F32)
        state[...] = jnp.zeros(state.shape, F32)

    @pl.when(i > 0)
    def _():
        xbuf[0:halo, :] = xbuf[t:t + halo, :]

    xbuf[halo:halo + t, 0:DN_QK_WIDTH] = q_ref[...].astype(F32)
    xbuf[halo:halo + t, DN_QK_WIDTH:2 * DN_QK_WIDTH] = k_ref[...].astype(F32)
    xbuf[halo:halo + t, 2 * DN_QK_WIDTH:] = v_ref[...].astype(F32)

    n_slabs = (2 * DN_QK_WIDTH + DN_WIDTH) // LANES
    for s in range(n_slabs):
        cs = slice(s * LANES, (s + 1) * LANES)
        acc = cw_ref[0:1, cs] * xbuf[halo - 3:halo - 3 + t, cs]
        for tap in range(1, DN_CONV):
            acc = acc + cw_ref[tap:tap + 1, cs] * xbuf[halo - 3 + tap:halo - 3 + tap + t, cs]
        y = acc * _sigmoid(acc)
        if s < 2 * DN_HEADS:
            y = y * lax.rsqrt(jnp.sum(y * y, axis=-1, keepdims=True) + EPS)
        if s < DN_HEADS:
            qs[:, cs] = y * (DN_DK ** -0.5)
        elif s < 2 * DN_HEADS:
            ks[:, (s - DN_HEADS) * LANES:(s - DN_HEADS + 1) * LANES] = y
        else:
            vs[:, (s - 2 * DN_HEADS) * LANES:(s - 2 * DN_HEADS + 1) * LANES] = y

    ab = ab_ref[...]
    g = -jnp.exp(alog_ref[...]) * _softplus(ab + dtb_ref[...])
    beta = _sigmoid(ab)
    cums = _dot_hi(cum_ref[...], g)
    gce[...] = _dot_hi(cums[0:t], eg_ref[...])
    gle[...] = _dot_hi(cums[t:2 * t], eg_ref[...])
    ble[...] = _dot_hi(beta, eb_ref[...])

    row = lax.broadcasted_iota(jnp.int32, (c, c), 0)
    col = lax.broadcasted_iota(jnp.int32, (c, c), 1)
    causal = row >= col
    strict = row > col
    eye = (row == col).astype(F32)
    pick0 = (lax.broadcasted_iota(jnp.int32, (c, LANES), 1) == 0).astype(F32)
    nrm = nrm_ref[...]

    def chunk_body(ci, carry):
        r0 = pl.multiple_of(ci * c, c)
        rows = pl.ds(r0, c)
        for hd in range(DN_HEADS):
            cs = slice(hd * LANES, (hd + 1) * LANES)
            q_ = qs[rows, cs]
            k_ = ks[rows, cs]
            v_ = vs[rows, cs]
            gcc = gce[rows, cs]
            glc = gle[rows, cs]
            bet = ble[rows, cs]
            egc = jnp.exp(gcc)
            kb = k_ * bet
            kf = k_.astype(BF16)
            g_row = _dot_nt(pick0, gcc, precision=HIGHEST)
            dm = jnp.where(causal, gcc[:, 0:c] - g_row, 0.0)
            decay = jnp.where(causal, jnp.exp(dm), 0.0)
            kk = _dot_nt(kb.astype(BF16), kf) * decay
            nmat = jnp.where(strict, kk, 0.0)
            p = eye - nmat
            m = _dot_hi(nmat, nmat)
            levels = c.bit_length() - 2
            for lvl in range(levels):
                p = p + _dot_hi(p, m)
                if lvl < levels - 1:
                    m = _dot_hi(m, m)
            rhs = jnp.concatenate([v_ * bet, kb * egc], axis=1)
            uw = _dot_hi(p, rhs)
            u = uw[:, 0:DN_DV]
            w = uw[:, DN_DV:]
            qk = _dot_nt(q_.astype(BF16), kf) * decay
            q_dec = q_ * egc
            k_dec = k_ * jnp.exp(glc - gcc)
            st = state[hd]
            stb = st.astype(BF16)
            v_new = u - _dot(w.astype(BF16), stb)
            vnb = v_new.astype(BF16)
            o = _dot(q_dec.astype(BF16), stb) + _dot(qk.astype(BF16), vnb)
            egl = jnp.exp(glc)
            state[hd] = st * jnp.concatenate([egl, egl], axis=0) + _dot(k_dec.T.astype(BF16), vnb)
            zz = z_ref[rows, cs].astype(F32)
            y = _rms_rows(o, nrm) * (zz * _sigmoid(zz))
            o_ref[rows, cs] = y.astype(o_ref.dtype)
        return carry

    lax.fori_loop(0, t // c, chunk_body, 0)


def _dn_constants(t):
    r = jnp.arange(t)
    same = (r[:, None] // DN_CHUNK) == (r[None, :] // DN_CHUNK)
    lower = r[:, None] >= r[None, :]
    cum = jnp.concatenate([(same & lower).astype(F32), same.astype(F32)], axis=0)
    lane = jnp.arange(LANES)[:, None]
    head = jnp.arange(DN_WIDTH)[None, :] // DN_DV
    eg = (lane == head).astype(F32)
    eb = (lane == head + DN_HEADS).astype(F32)
    return cum, eg, eb


def _delta_net(proj, ab, conv_w, a_log_pad, dt_bias_pad, dn_norm, t):
    s = proj.shape[0]
    cum, eg, eb = _dn_constants(t)
    cb = COL_DN_Q // DN_QK_WIDTH
    const = lambda shape: pl.BlockSpec(shape, lambda i: (0, 0))
    return pl.pallas_call(
        _dn_kernel,
        grid=(s // t,),
        in_specs=[
            pl.BlockSpec((t, DN_QK_WIDTH), lambda i: (i, cb)),
            pl.BlockSpec((t, DN_QK_WIDTH), lambda i: (i, cb + 1)),
            pl.BlockSpec((t, DN_WIDTH), lambda i: (i, cb + 2)),
            pl.BlockSpec((t, DN_WIDTH), lambda i: (i, cb + 3)),
            pl.BlockSpec((t, AB_COLS), lambda i: (i, 0)),
            const((DN_CONV, 2 * DN_QK_WIDTH + DN_WIDTH)),
            const((1, AB_COLS)),
            const((1, AB_COLS)),
            const((1, DN_DV)),
            const((2 * t, t)),
            const((LANES, DN_WIDTH)),
            const((LANES, DN_WIDTH)),
        ],
        out_specs=pl.BlockSpec((t, DN_WIDTH), lambda i: (i, 0)),
        out_shape=jax.ShapeDtypeStruct((s, DN_WIDTH), BF16),
        scratch_shapes=[
            pltpu.VMEM((t + SUBLANES, 2 * DN_QK_WIDTH + DN_WIDTH), F32),
            pltpu.VMEM((t, DN_QK_WIDTH), F32),
            pltpu.VMEM((t, DN_QK_WIDTH), F32),
            pltpu.VMEM((t, DN_WIDTH), F32),
            pltpu.VMEM((t, DN_WIDTH), F32),
            pltpu.VMEM((t, DN_WIDTH), F32),
            pltpu.VMEM((t, DN_WIDTH), F32),
            pltpu.VMEM((DN_HEADS, DN_DK, DN_DV), F32),
        ],
        compiler_params=_cparams(("arbitrary",)),
        name="delta_net",
    )(proj, proj, proj, proj, ab, conv_w, a_log_pad, dt_bias_pad, dn_norm, cum, eg, eb)


def _swa_kernel(sinks_ref, q_ref, kv_ref, kvp_ref, o_ref):
    i = pl.program_id(0)
    tq = q_ref.shape[0]
    w = WINDOW
    nwb = tq // w
    grp = SWA_HEADS // SWA_KV_HEADS
    ext = jnp.concatenate([kvp_ref[...], kv_ref[...]], axis=0)
    qi = lax.broadcasted_iota(jnp.int32, (w, 2 * w), 0)
    kj = lax.broadcasted_iota(jnp.int32, (w, 2 * w), 1)
    dist = qi + w - kj
    in_band = (dist >= 0) & (dist < w)
    distf = dist.astype(F32)
    for wb in range(nwb):
        blk = i * nwb + wb
        valid = in_band & (blk * w + kj - w >= 0)
        win = ext[wb * w:wb * w + 2 * w]
        qwb = q_ref[wb * w:(wb + 1) * w, :]
        outs = []
        for hk in range(SWA_KV_HEADS):
            kh = win[:, hk * SWA_DH:(hk + 1) * SWA_DH]
            vh = win[:, SWA_KV_WIDTH + hk * SWA_DH:SWA_KV_WIDTH + (hk + 1) * SWA_DH]
            for gi in range(grp):
                hd = hk * grp + gi
                slope = 2.0 ** (-8.0 * (hd + 1) / SWA_HEADS)
                qh = qwb[:, hd * SWA_DH:(hd + 1) * SWA_DH]
                sc = _dot_nt(qh, kh) * (SWA_DH ** -0.5) - slope * distf
                sc = jnp.where(valid, sc, NEG_BIG)
                sink = sinks_ref[hd]
                mx = jnp.maximum(jnp.max(sc, axis=-1, keepdims=True), sink)
                p = jnp.exp(sc - mx)
                den = jnp.sum(p, axis=-1, keepdims=True) + jnp.exp(sink - mx)
                outs.append(_dot(p.astype(BF16), vh) / den)
        o_ref[wb * w:(wb + 1) * w, :] = jnp.concatenate(outs, axis=1).astype(o_ref.dtype)


def _swa(proj, sinks, tq):
    s = proj.shape[0]
    nwb = tq // WINDOW
    qb = COL_SWA_Q // SWA_WIDTH
    kvb = COL_SWA_KV // (2 * SWA_KV_WIDTH)
    return pl.pallas_call(
        _swa_kernel,
        grid=(s // tq,),
        in_specs=[
            pl.BlockSpec(memory_space=pltpu.SMEM),
            pl.BlockSpec((tq, SWA_WIDTH), lambda i: (i, qb)),
            pl.BlockSpec((tq, 2 * SWA_KV_WIDTH), lambda i: (i, kvb)),
            pl.BlockSpec((WINDOW, 2 * SWA_KV_WIDTH), lambda i: (jnp.maximum(i * nwb - 1, 0), kvb)),
        ],
        out_specs=pl.BlockSpec((tq, SWA_WIDTH), lambda i: (i, 0)),
        out_shape=jax.ShapeDtypeStruct((s, SWA_WIDTH), BF16),
        compiler_params=_cparams(("parallel",)),
        name="swa",
    )(sinks, proj, proj, proj)


def _sb_kernel(q_ref, k_ref, v_ref, tri_ref, o_ref, acc_ref, aft_ref):
    i = pl.program_id(1)
    tq = q_ref.shape[0]
    tk = tq
    blk = LANES
    nsub = tk // blk
    q = q_ref[...]
    tri = tri_ref[...]
    acc_ref[...] = jnp.zeros(acc_ref.shape, F32)
    aft_ref[...] = jnp.zeros(aft_ref.shape, F32)

    def tile(k0, mask):
        kblk = k_ref[pl.ds(k0, tk), :]
        vblk = v_ref[pl.ds(k0, tk), :]
        z = _dot_nt(q, kblk) * (SB_DH ** -0.5)
        lneg = -_softplus(z)
        if mask is not None:
            lneg = jnp.where(mask, lneg, 0.0)
        aft = aft_ref[...]
        revs = [None] * nsub
        for sb in reversed(range(nsub)):
            cum = _dot(lneg[:, sb * blk:(sb + 1) * blk].astype(BF16), tri)
            revs[sb] = cum[:, 0:blk] + aft
            aft = aft + cum[:, blk:]
        a = jnp.exp(z + jnp.concatenate(revs, axis=1))
        if mask is not None:
            a = jnp.where(mask, a, 0.0)
        acc_ref[...] += _dot(a.astype(BF16), vblk)
        aft_ref[...] = aft

    row = lax.broadcasted_iota(jnp.int32, (tq, tk), 0)
    col = lax.broadcasted_iota(jnp.int32, (tq, tk), 1)
    tile(pl.multiple_of(i * tq, tq), row > col)

    def body(n, carry):
        tile(pl.multiple_of((i - 1 - n) * tk, tk), None)
        return carry

    lax.fori_loop(0, i, body, 0)
    o_ref[...] = acc_ref[...].astype(o_ref.dtype)


def _stick_breaking(proj, tq):
    s = proj.shape[0]
    qb = COL_SB_Q // SB_DH
    r = jnp.arange(LANES)
    tri = jnp.concatenate([(r[:, None] >= r[None, :]).astype(BF16), jnp.ones((LANES, LANES), BF16)], axis=1)
    return pl.pallas_call(
        _sb_kernel,
        grid=(SB_HEADS, s // tq),
        in_specs=[
            pl.BlockSpec((tq, SB_DH), lambda h, i: (i, qb + h)),
            pl.BlockSpec((s, SB_DH), lambda h, i: (0, qb + SB_HEADS + h)),
            pl.BlockSpec((s, SB_DH), lambda h, i: (0, qb + 2 * SB_HEADS + h)),
            pl.BlockSpec((LANES, 2 * LANES), lambda h, i: (0, 0)),
        ],
        out_specs=pl.BlockSpec((tq, SB_DH), lambda h, i: (i, h)),
        out_shape=jax.ShapeDtypeStruct((s, SB_WIDTH), BF16),
        scratch_shapes=[pltpu.VMEM((tq, SB_DH), F32), pltpu.VMEM((tq, LANES), F32)],
        compiler_params=_cparams(("parallel", "parallel")),
        name="stick_breaking",
    )(proj, proj, proj, tri)


def _merge_kernel(h_ref, g0_ref, g1_ref, g2_ref, ydn_ref, yswa_ref, ysb_ref, wdn_ref, wswa_ref, wsb_ref, wo_ref,
                  o_ref):
    merged = _sigmoid(g0_ref[...].astype(F32)) * _dot(ydn_ref[...], wdn_ref[...])
    merged = merged + _sigmoid(g1_ref[...].astype(F32)) * _dot(yswa_ref[...], wswa_ref[...])
    merged = merged + _sigmoid(g2_ref[...].astype(F32)) * _dot(ysb_ref[...], wsb_ref[...])
    o_ref[...] = h_ref[...] + _dot(merged.astype(BF16), wo_ref[...])


def _merge(h, proj, y_dn, y_swa, y_sb, w_dn, w_swa, w_sb, w_o, tm):
    s = h.shape[0]
    rows = lambda width, col=0: pl.BlockSpec((tm, width), lambda i: (i, col))
    const = lambda a: pl.BlockSpec(a.shape, lambda i: (0, 0), pipeline_mode=pl.Buffered(1))
    return pl.pallas_call(
        _merge_kernel,
        grid=(s // tm,),
        in_specs=[
            rows(D_MODEL), rows(D_MODEL, 0), rows(D_MODEL, 1), rows(D_MODEL, 2),
            rows(DN_WIDTH), rows(SWA_WIDTH), rows(SB_WIDTH),
            const(w_dn), const(w_swa), const(w_sb), const(w_o),
        ],
        out_specs=rows(D_MODEL),
        out_shape=jax.ShapeDtypeStruct((s, D_MODEL), F32),
        compiler_params=_cparams(("parallel",)),
        name="merge",
    )(h, proj, proj, proj, y_dn, y_swa, y_sb, w_dn, w_swa, w_sb, w_o)


def _norm_matmul_kernel(x_ref, g_ref, w_ref, o_ref):
    xn = _rms_rows(x_ref[...], g_ref[...]).astype(BF16)
    o_ref[...] = _dot(xn, w_ref[...]).astype(o_ref.dtype)


def _mem_kv(mem, g, w):
    m = mem.shape[0]
    n = w.shape[1]
    return pl.pallas_call(
        _norm_matmul_kernel,
        grid=(1,),
        in_specs=[
            pl.BlockSpec((m, D_MODEL), lambda i: (0, 0)),
            pl.BlockSpec((1, D_MODEL), lambda i: (0, 0)),
            pl.BlockSpec((D_MODEL, n), lambda i: (0, 0)),
        ],
        out_specs=pl.BlockSpec((m, n), lambda i: (0, 0)),
        out_shape=jax.ShapeDtypeStruct((m, n), BF16),
        compiler_params=_cparams(("arbitrary",)),
        name="mem_kv",
    )(mem, g, w)


def _xattn_kernel(h_ref, g_ref, kv_ref, wq_ref, wo_ref, o_ref):
    h = h_ref[...]
    hn = _rms_rows(h, g_ref[...]).astype(BF16)
    q = _dot(hn, wq_ref[...]).astype(BF16)
    outs = []
    for hd in range(X_HEADS):
        cs = slice(hd * X_DH, (hd + 1) * X_DH)
        kh = kv_ref[:, cs]
        vh = kv_ref[:, X_WIDTH + hd * X_DH:X_WIDTH + (hd + 1) * X_DH]
        sc = _dot_nt(q[:, cs], kh) * (X_DH ** -0.5)
        mx = jnp.max(sc, axis=-1, keepdims=True)
        p = jnp.exp(sc - mx)
        den = jnp.sum(p, axis=-1, keepdims=True)
        outs.append((_dot(p.astype(BF16), vh) / den).astype(BF16))
    o = jnp.concatenate(outs, axis=1)
    o_ref[...] = h + _dot(o, wo_ref[...])


def _xattn(h, g, kv, w_q, w_o, tm):
    s = h.shape[0]
    const = lambda a: pl.BlockSpec(a.shape, lambda i: (0, 0))
    return pl.pallas_call(
        _xattn_kernel,
        grid=(s // tm,),
        in_specs=[pl.BlockSpec((tm, D_MODEL), lambda i: (i, 0)), const(g), const(kv), const(w_q), const(w_o)],
        out_specs=pl.BlockSpec((tm, D_MODEL), lambda i: (i, 0)),
        out_shape=jax.ShapeDtypeStruct((s, D_MODEL), F32),
        compiler_params=_cparams(("parallel",)),
        name="xattn",
    )(h, g, kv, w_q, w_o)


def _ffn_kernel(h_ref, g_ref, wg_ref, wv_ref, cg_ref, cv_ref, wd_ref, o_ref, hn_ref, acc_ref, gbuf, vbuf, carry):
    i = pl.program_id(0)
    j = pl.program_id(1)
    tm = h_ref.shape[0]
    halo = SUBLANES

    @pl.when(j == 0)
    def _():
        hn_ref[...] = _rms_rows(h_ref[...], g_ref[...]).astype(BF16)
        acc_ref[...] = jnp.zeros(acc_ref.shape, F32)

    @pl.when(i == 0)
    def _():
        carry[j] = jnp.zeros(carry.shape[1:], F32)

    hn = hn_ref[...]
    gbuf[0:halo, :] = carry[j, 0]
    vbuf[0:halo, :] = carry[j, 1]
    gbuf[halo:, :] = _dot(hn, wg_ref[...])
    vbuf[halo:, :] = _dot(hn, wv_ref[...])
    carry[j, 0] = gbuf[tm:tm + halo, :]
    carry[j, 1] = vbuf[tm:tm + halo, :]

    def conv(buf, cw_ref):
        base = halo - (FFN_CONV - 1)
        out = cw_ref[0:1, :] * buf[base:base + tm, :]
        for tap in range(1, FFN_CONV):
            out = out + cw_ref[tap:tap + 1, :] * buf[base + tap:base + tap + tm, :]
        return out

    gate = conv(gbuf, cg_ref)
    val = conv(vbuf, cv_ref)
    act = (gate * _sigmoid(gate) * val).astype(BF16)
    acc_ref[...] += _dot(act, wd_ref[...])

    @pl.when(j == pl.num_programs(1) - 1)
    def _():
        o_ref[...] = h_ref[...] + acc_ref[...]


def _ffn(h, g, w_up, conv_w, w_down, tm, tn):
    s = h.shape[0]
    nj = D_FF // tn
    return pl.pallas_call(
        _ffn_kernel,
        grid=(s // tm, nj),
        in_specs=[
            pl.BlockSpec((tm, D_MODEL), lambda i, j: (i, 0)),
            pl.BlockSpec((1, D_MODEL), lambda i, j: (0, 0)),
            pl.BlockSpec((D_MODEL, tn), lambda i, j: (0, j)),
            pl.BlockSpec((D_MODEL, tn), lambda i, j: (0, j + nj)),
            pl.BlockSpec((FFN_CONV, tn), lambda i, j: (0, j)),
            pl.BlockSpec((FFN_CONV, tn), lambda i, j: (0, j + nj)),
            pl.BlockSpec((tn, D_MODEL), lambda i, j: (j, 0)),
        ],
        out_specs=pl.BlockSpec((tm, D_MODEL), lambda i, j: (i, 0)),
        out_shape=jax.ShapeDtypeStruct((s, D_MODEL), F32),
        scratch_shapes=[
            pltpu.VMEM((tm, D_MODEL), BF16),
            pltpu.VMEM((tm, D_MODEL), F32),
            pltpu.VMEM((tm + SUBLANES, tn), F32),
            pltpu.VMEM((tm + SUBLANES, tn), F32),
            pltpu.VMEM((nj, 2, SUBLANES, tn), F32),
        ],
        compiler_params=_cparams(("arbitrary", "arbitrary")),
        name="conv_ffn",
    )(h, g, w_up, w_up, conv_w, conv_w, w_down)


def _final_norm_kernel(h_ref, g_ref, o_ref):
    o_ref[...] = _rms_rows(h_ref[...], g_ref[...])


def _final_norm(h, g, tm):
    s = h.shape[0]
    return pl.pallas_call(
        _final_norm_kernel,
        grid=(s // tm,),
        in_specs=[pl.BlockSpec((tm, D_MODEL), lambda i: (i, 0)), pl.BlockSpec((1, D_MODEL), lambda i: (0, 0))],
        out_specs=pl.BlockSpec((tm, D_MODEL), lambda i: (i, 0)),
        out_shape=jax.ShapeDtypeStruct((s, D_MODEL), F32),
        compiler_params=_cparams(("parallel",)),
        name="final_norm",
    )(h, g)


def _tile(s, want):
    return min(s, want)


def _reorder_w_in(w_in):
    o_z = 2 * DN_QK_WIDTH + DN_WIDTH
    o_a = o_z + DN_WIDTH
    o_swq = o_a + 2 * DN_HEADS
    o_swkv = o_swq + SWA_WIDTH
    o_sb = o_swkv + 2 * SWA_KV_WIDTH
    o_gate = o_sb + 3 * SB_WIDTH
    big = jnp.concatenate(
        [w_in[..., o_gate:], w_in[..., 0:o_a], w_in[..., o_swq:o_swkv], w_in[..., o_sb:o_gate],
         w_in[..., o_swkv:o_sb]], axis=-1).astype(BF16)
    ab = w_in[..., o_a:o_swq]
    ab = jnp.pad(ab, ((0, 0), (0, 0), (0, AB_COLS - ab.shape[-1]))).astype(BF16)
    return big, ab


def kernel(x, mem, norm_mix, w_in, dn_conv, dn_a_log, dn_dt_bias, dn_norm, swa_sinks, w_br_dn, w_br_swa, w_br_sb, w_o, norm_xattn, norm_mem, w_xq, w_xkv, w_xo, norm_ffn, w_up, ffn_conv, w_down, norm_final):
    depth = w_in.shape[0]
    s = x.shape[1]
    assert x.shape[0] == 1 and s % 256 == 0
    h = x[0]
    mem2 = mem[0]

    w_big, w_ab = _reorder_w_in(w_in)
    w_br_dn_b = w_br_dn.astype(BF16)
    w_br_swa_b = w_br_swa.astype(BF16)
    w_br_sb_b = w_br_sb.astype(BF16)
    w_o_b = w_o.astype(BF16)
    w_xq_b = w_xq.astype(BF16)
    w_xkv_b = w_xkv.astype(BF16)
    w_xo_b = w_xo.astype(BF16)
    w_up_b = w_up.astype(BF16)
    w_down_b = w_down.astype(BF16)
    pad8 = lambda a: jnp.pad(a, ((0, 0), (0, AB_COLS - a.shape[-1])))
    a_log_pad = pad8(dn_a_log)
    dt_bias_pad = pad8(dn_dt_bias)

    for l in range(depth):
        proj, ab = _in_proj(h, norm_mix[l][None], w_big[l], w_ab[l], tm=_tile(s, 512), tn=1792)
        y_dn = _delta_net(proj, ab, dn_conv[l], a_log_pad[l][None], dt_bias_pad[l][None], dn_norm[l][None],
                          t=_tile(s, 256))
        y_swa = _swa(proj, swa_sinks[l], tq=_tile(s, 512))
        y_sb = _stick_breaking(proj, tq=_tile(s, 256))
        h = _merge(h, proj, y_dn, y_swa, y_sb, w_br_dn_b[l], w_br_swa_b[l], w_br_sb_b[l], w_o_b[l],
                   tm=_tile(s, 256))
        kv = _mem_kv(mem2, norm_mem[l][None], w_xkv_b[l])
        h = _xattn(h, norm_xattn[l][None], kv, w_xq_b[l], w_xo_b[l], tm=_tile(s, 512))
        h = _ffn(h, norm_ffn[l][None], w_up_b[l], ffn_conv[l], w_down_b[l], tm=_tile(s, 512), tn=512)
    return _final_norm(h, norm_final[None], tm=_tile(s, 1024))[None]
```

```python
import functools

import jax
import jax.numpy as jnp
from jax import lax
from jax.experimental import pallas as pl
from jax.experimental.pallas import tpu as pltpu

F32 = jnp.float32
BF16 = jnp.bfloat16
HIGHEST = lax.Precision.HIGHEST

D_MODEL = 2048
EPS = 1e-6
DN_HEADS = 8
DN_DK = 128
DN_DV = 128
DN_CONV = 4
DN_CHUNK = 64
SWA_HEADS = 8
SWA_KV_HEADS = 2
SWA_DH = 64
WINDOW = 128
SB_HEADS = 4
SB_DH = 128
X_HEADS = 4
X_DH = 128
D_FF = 4096
FFN_CONV = 3
N_BRANCH = 3

DN_QK_WIDTH = DN_HEADS * DN_DK
DN_WIDTH = DN_HEADS * DN_DV
SWA_WIDTH = SWA_HEADS * SWA_DH
SWA_KV_WIDTH = SWA_KV_HEADS * SWA_DH
SB_WIDTH = SB_HEADS * SB_DH
X_WIDTH = X_HEADS * X_DH

LANES = 128
SUBLANES = 8
VMEM_LIMIT = 56 * 1024 * 1024

PROJ_COLS = N_BRANCH * D_MODEL + 3 * DN_QK_WIDTH + DN_WIDTH + SWA_WIDTH + 3 * SB_WIDTH + 2 * SWA_KV_WIDTH
COL_DN_Q = N_BRANCH * D_MODEL
COL_DN_Z = COL_DN_Q + 3 * DN_QK_WIDTH
COL_SWA_Q = COL_DN_Z + DN_WIDTH
COL_SB_Q = COL_SWA_Q + SWA_WIDTH
COL_SWA_KV = COL_SB_Q + 3 * SB_WIDTH
AB_COLS = LANES

NEG_BIG = -1e30
LOG2E = 1.4426950408889634
SB_SKIP_LOG2 = 170.0


def _cparams(semantics):
    return pltpu.CompilerParams(dimension_semantics=semantics, vmem_limit_bytes=VMEM_LIMIT)


def _dot(a, b):
    return jnp.dot(a, b, preferred_element_type=F32)


def _dot_nt(a, b, precision=None):
    return lax.dot_general(a, b, (((1,), (1,)), ((), ())), preferred_element_type=F32, precision=precision)


def _dot_hi(a, b):
    return jnp.dot(a, b, preferred_element_type=F32, precision=HIGHEST)


def _sigmoid(x):
    return 1.0 / (1.0 + jnp.exp(-x))


def _softplus(x):
    return jnp.maximum(x, 0.0) + jnp.log1p(jnp.exp(-jnp.abs(x)))


def _rms_rows(x, g):
    ms = jnp.mean(x * x, axis=-1, keepdims=True)
    return x * lax.rsqrt(ms + EPS) * g


def _in_proj_kernel(h_ref, g_ref, w_ref, wab_ref, o_ref, ab_ref, xn_ref):
    @pl.when(pl.program_id(1) == 0)
    def _():
        xn = _rms_rows(h_ref[...], g_ref[...]).astype(BF16)
        xn_ref[...] = xn
        ab_ref[...] = _dot(xn, wab_ref[...])

    o_ref[...] = _dot(xn_ref[...], w_ref[...]).astype(o_ref.dtype)


def _in_proj(h, g, w, wab, tm, tn):
    s = h.shape[0]
    n = w.shape[1]
    return pl.pallas_call(
        _in_proj_kernel,
        grid=(s // tm, n // tn),
        in_specs=[
            pl.BlockSpec((tm, D_MODEL), lambda i, j: (i, 0)),
            pl.BlockSpec((1, D_MODEL), lambda i, j: (0, 0)),
            pl.BlockSpec((D_MODEL, tn), lambda i, j: (0, j)),
            pl.BlockSpec((D_MODEL, AB_COLS), lambda i, j: (0, 0)),
        ],
        out_specs=[
            pl.BlockSpec((tm, tn), lambda i, j: (i, j)),
            pl.BlockSpec((tm, AB_COLS), lambda i, j: (i, 0)),
        ],
        out_shape=[jax.ShapeDtypeStruct((s, n), BF16), jax.ShapeDtypeStruct((s, AB_COLS), F32)],
        scratch_shapes=[pltpu.VMEM((tm, D_MODEL), BF16)],
        compiler_params=_cparams(("parallel", "arbitrary")),
        name="in_proj",
    )(h, g, w, wab)


def _dn_kernel(q_ref, k_ref, v_ref, z_ref, ab_ref, cw_ref, alog_ref, dtb_ref, nrm_ref, cum_ref, eg_ref, eb_ref,
               o_ref, xbuf, qs, ks, vs, gce, gle, ble, state):
    i = pl.program_id(0)
    t = q_ref.shape[0]
    c = DN_CHUNK
    halo = SUBLANES

    @pl.when(i == 0)
    def _():
        xbuf[0:halo, :] = jnp.zeros((halo, xbuf.shape[1]), F32)
        state[...] = jnp.zeros(state.shape, F32)

    @pl.when(i > 0)
    def _():
        xbuf[0:halo, :] = xbuf[t:t + halo, :]

    xbuf[halo:halo + t, 0:DN_QK_WIDTH] = q_ref[...].astype(F32)
    xbuf[halo:halo + t, DN_QK_WIDTH:2 * DN_QK_WIDTH] = k_ref[...].astype(F32)
    xbuf[halo:halo + t, 2 * DN_QK_WIDTH:] = v_ref[...].astype(F32)

    n_slabs = (2 * DN_QK_WIDTH + DN_WIDTH) // LANES
    for s in range(n_slabs):
        cs = slice(s * LANES, (s + 1) * LANES)
        acc = cw_ref[0:1, cs] * xbuf[halo - 3:halo - 3 + t, cs]
        for tap in range(1, DN_CONV):
            acc = acc + cw_ref[tap:tap + 1, cs] * xbuf[halo - 3 + tap:halo - 3 + tap + t, cs]
        y = acc * _sigmoid(acc)
        if s < 2 * DN_HEADS:
            y = y * lax.rsqrt(jnp.sum(y * y, axis=-1, keepdims=True) + EPS)
        if s < DN_HEADS:
            qs[:, cs] = y * (DN_DK ** -0.5)
        elif s < 2 * DN_HEADS:
            ks[:, (s - DN_HEADS) * LANES:(s - DN_HEADS + 1) * LANES] = y
        else:
            vs[:, (s - 2 * DN_HEADS) * LANES:(s - 2 * DN_HEADS + 1) * LANES] = y

    ab = ab_ref[...]
    g = -jnp.exp(alog_ref[...]) * _softplus(ab + dtb_ref[...])
    beta = _sigmoid(ab)
    cums = _dot_hi(cum_ref[...], g)
    gce[...] = _dot_hi(cums[0:t], eg_ref[...])
    gle[...] = _dot_hi(cums[t:2 * t], eg_ref[...])
    ble[...] = _dot_hi(beta, eb_ref[...])

    row = lax.broadcasted_iota(jnp.int32, (c, c), 0)
    col = lax.broadcasted_iota(jnp.int32, (c, c), 1)
    causal = row >= col
    strict = row > col
    eye = (row == col).astype(F32)
    pick0 = (lax.broadcasted_iota(jnp.int32, (c, LANES), 1) == 0).astype(F32)
    nrm = nrm_ref[...]

    def chunk_body(ci, carry):
        r0 = pl.multiple_of(ci * c, c)
        rows = pl.ds(r0, c)
        for hd in range(DN_HEADS):
            cs = slice(hd * LANES, (hd + 1) * LANES)
            q_ = qs[rows, cs]
            k_ = ks[rows, cs]
            v_ = vs[rows, cs]
            gcc = gce[rows, cs]
            glc = gle[rows, cs]
            bet = ble[rows, cs]
            egc = jnp.exp(gcc)
            kb = k_ * bet
            kf = k_.astype(BF16)
            g_row = _dot_nt(pick0, gcc, precision=HIGHEST)
            dm = jnp.where(causal, gcc[:, 0:c] - g_row, 0.0)
            decay = jnp.where(causal, jnp.exp(dm), 0.0)
            kk = _dot_nt(kb.astype(BF16), kf) * decay
            nmat = jnp.where(strict, kk, 0.0)
            p = eye - nmat
            m = _dot_hi(nmat, nmat)
            levels = c.bit_length() - 2
            for lvl in range(levels):
                p = p + _dot_hi(p, m)
                if lvl < levels - 1:
                    m = _dot_hi(m, m)
            rhs = jnp.concatenate([v_ * bet, kb * egc], axis=1)
            uw = _dot_hi(p, rhs)
            u = uw[:, 0:DN_DV]
            w = uw[:, DN_DV:]
            qk = _dot_nt(q_.astype(BF16), kf) * decay
            q_dec = q_ * egc
            k_dec = k_ * jnp.exp(glc - gcc)
            st = state[hd]
            stb = st.astype(BF16)
            v_new = u - _dot(w.astype(BF16), stb)
            vnb = v_new.astype(BF16)
            o = _dot(q_dec.astype(BF16), stb) + _dot(qk.astype(BF16), vnb)
            egl = jnp.exp(glc)
            state[hd] = st * jnp.concatenate([egl, egl], axis=0) + _dot(k_dec.T.astype(BF16), vnb)
            zz = z_ref[rows, cs].astype(F32)
            y = _rms_rows(o, nrm) * (zz * _sigmoid(zz))
            o_ref[rows, cs] = y.astype(o_ref.dtype)
        return carry

    lax.fori_loop(0, t // c, chunk_body, 0)


def _dn_constants(t):
    r = jnp.arange(t)
    same = (r[:, None] // DN_CHUNK) == (r[None, :] // DN_CHUNK)
    lower = r[:, None] >= r[None, :]
    cum = jnp.concatenate([(same & lower).astype(F32), same.astype(F32)], axis=0)
    lane = jnp.arange(LANES)[:, None]
    head = jnp.arange(DN_WIDTH)[None, :] // DN_DV
    eg = (lane == head).astype(F32)
    eb = (lane == head + DN_HEADS).astype(F32)
    return cum, eg, eb


def _delta_net(proj, ab, conv_w, a_log_pad, dt_bias_pad, dn_norm, t):
    s = proj.shape[0]
    cum, eg, eb = _dn_constants(t)
    cb = COL_DN_Q // DN_QK_WIDTH
    const = lambda shape: pl.BlockSpec(shape, lambda i: (0, 0))
    return pl.pallas_call(
        _dn_kernel,
        grid=(s // t,),
        in_specs=[
            pl.BlockSpec((t, DN_QK_WIDTH), lambda i: (i, cb)),
            pl.BlockSpec((t, DN_QK_WIDTH), lambda i: (i, cb + 1)),
            pl.BlockSpec((t, DN_WIDTH), lambda i: (i, cb + 2)),
            pl.BlockSpec((t, DN_WIDTH), lambda i: (i, cb + 3)),
            pl.BlockSpec((t, AB_COLS), lambda i: (i, 0)),
            const((DN_CONV, 2 * DN_QK_WIDTH + DN_WIDTH)),
            const((1, AB_COLS)),
            const((1, AB_COLS)),
            const((1, DN_DV)),
            const((2 * t, t)),
            const((LANES, DN_WIDTH)),
            const((LANES, DN_WIDTH)),
        ],
        out_specs=pl.BlockSpec((t, DN_WIDTH), lambda i: (i, 0)),
        out_shape=jax.ShapeDtypeStruct((s, DN_WIDTH), BF16),
        scratch_shapes=[
            pltpu.VMEM((t + SUBLANES, 2 * DN_QK_WIDTH + DN_WIDTH), F32),
            pltpu.VMEM((t, DN_QK_WIDTH), F32),
            pltpu.VMEM((t, DN_QK_WIDTH), F32),
            pltpu.VMEM((t, DN_WIDTH), F32),
            pltpu.VMEM((t, DN_WIDTH), F32),
            pltpu.VMEM((t, DN_WIDTH), F32),
            pltpu.VMEM((t, DN_WIDTH), F32),
            pltpu.VMEM((DN_HEADS, DN_DK, DN_DV), F32),
        ],
        compiler_params=_cparams(("arbitrary",)),
        name="delta_net",
    )(proj, proj, proj, proj, ab, conv_w, a_log_pad, dt_bias_pad, dn_norm, cum, eg, eb)


def _swa_kernel(sinks_ref, q_ref, kv_ref, kvp_ref, o_ref):
    i = pl.program_id(0)
    tq = q_ref.shape[0]
    w = WINDOW
    nwb = tq // w
    grp = SWA_HEADS // SWA_KV_HEADS
    ext = jnp.concatenate([kvp_ref[...], kv_ref[...]], axis=0)
    qi = lax.broadcasted_iota(jnp.int32, (w, 2 * w), 0)
    kj = lax.broadcasted_iota(jnp.int32, (w, 2 * w), 1)
    dist = qi + w - kj
    in_band = (dist >= 0) & (dist < w)
    distf = dist.astype(F32)
    for wb in range(nwb):
        blk = i * nwb + wb
        valid = in_band & (blk * w + kj - w >= 0)
        win = ext[wb * w:wb * w + 2 * w]
        qwb = q_ref[wb * w:(wb + 1) * w, :]
        outs = []
        for hk in range(SWA_KV_HEADS):
            kh = win[:, hk * SWA_DH:(hk + 1) * SWA_DH]
            vh = win[:, SWA_KV_WIDTH + hk * SWA_DH:SWA_KV_WIDTH + (hk + 1) * SWA_DH]
            for gi in range(grp):
                hd = hk * grp + gi
                slope = 2.0 ** (-8.0 * (hd + 1) / SWA_HEADS)
                qh = qwb[:, hd * SWA_DH:(hd + 1) * SWA_DH]
                sc = _dot_nt(qh, kh) * (SWA_DH ** -0.5) - slope * distf
                sc = jnp.where(valid, sc, NEG_BIG)
                sink = sinks_ref[hd]
                mx = jnp.maximum(jnp.max(sc, axis=-1, keepdims=True), sink)
                p = jnp.exp(sc - mx)
                den = jnp.sum(p, axis=-1, keepdims=True) + jnp.exp(sink - mx)
                outs.append(_dot(p.astype(BF16), vh) / den)
        o_ref[wb * w:(wb + 1) * w, :] = jnp.concatenate(outs, axis=1).astype(o_ref.dtype)


def _swa(proj, sinks, tq):
    s = proj.shape[0]
    nwb = tq // WINDOW
    qb = COL_SWA_Q // SWA_WIDTH
    kvb = COL_SWA_KV // (2 * SWA_KV_WIDTH)
    return pl.pallas_call(
        _swa_kernel,
        grid=(s // tq,),
        in_specs=[
            pl.BlockSpec(memory_space=pltpu.SMEM),
            pl.BlockSpec((tq, SWA_WIDTH), lambda i: (i, qb)),
            pl.BlockSpec((tq, 2 * SWA_KV_WIDTH), lambda i: (i, kvb)),
            pl.BlockSpec((WINDOW, 2 * SWA_KV_WIDTH), lambda i: (jnp.maximum(i * nwb - 1, 0), kvb)),
        ],
        out_specs=pl.BlockSpec((tq, SWA_WIDTH), lambda i: (i, 0)),
        out_shape=jax.ShapeDtypeStruct((s, SWA_WIDTH), BF16),
        compiler_params=_cparams(("parallel",)),
        name="swa",
    )(sinks, proj, proj, proj)


def _sb_knorm_kernel(k_ref, o_ref, run_ref):
    t = pl.program_id(0)

    @pl.when(t == 0)
    def _():
        for hd in range(SB_HEADS):
            run_ref[hd] = 0.0

    k = k_ref[...].astype(F32)
    for hd in range(SB_HEADS):
        kh = k[:, hd * SB_DH:(hd + 1) * SB_DH]
        cur = jnp.maximum(run_ref[hd], jnp.max(jnp.sum(kh * kh, axis=-1, keepdims=True)))
        run_ref[hd] = cur
        o_ref[t, hd] = cur


def _sb_key_norms(proj, tk):
    s = proj.shape[0]
    kb = COL_SB_Q // SB_WIDTH + 1
    return pl.pallas_call(
        _sb_knorm_kernel,
        grid=(s // tk,),
        in_specs=[pl.BlockSpec((tk, SB_WIDTH), lambda t: (t, kb))],
        out_specs=pl.BlockSpec(memory_space=pltpu.SMEM),
        out_shape=jax.ShapeDtypeStruct((s // tk, SB_HEADS), F32),
        scratch_shapes=[pltpu.SMEM((SB_HEADS,), F32)],
        compiler_params=_cparams(("arbitrary",)),
        name="sb_key_norms",
    )(proj)


def _sb_kernel(kmax_ref, q_ref, k_ref, v_ref, ntri_ref, o_ref, acc_ref, aft_ref):
    hd = pl.program_id(0)
    i = pl.program_id(1)
    tq = q_ref.shape[0]
    tk = tq
    blk = LANES
    nsub = tk // blk
    c2 = (SB_DH ** -0.5) * LOG2E
    q = q_ref[...]
    ntri = ntri_ref[...]
    acc_ref[...] = jnp.zeros(acc_ref.shape, F32)
    aft_ref[...] = jnp.zeros(aft_ref.shape, F32)
    qf = q.astype(F32)
    qn2 = jnp.sum(qf * qf, axis=-1, keepdims=True)

    def tile(k0, mask):
        kblk = k_ref[pl.ds(k0, tk), :]
        vblk = v_ref[pl.ds(k0, tk), :]
        z2 = _dot_nt(q, kblk) * c2
        sp2 = jnp.maximum(z2, 0.0) + jnp.log(1.0 + jnp.exp2(-jnp.abs(z2))) * LOG2E
        if mask is not None:
            sp2 = jnp.where(mask, sp2, 0.0)
        aft = aft_ref[...]
        revs = [None] * nsub
        for sb in reversed(range(nsub)):
            cum = _dot(sp2[:, sb * blk:(sb + 1) * blk].astype(BF16), ntri)
            revs[sb] = cum[:, 0:blk] + aft
            aft = aft + cum[:, blk:]
        a = jnp.exp2(z2 + jnp.concatenate(revs, axis=1))
        if mask is not None:
            a = jnp.where(mask, a, 0.0)
        acc_ref[...] += _dot(a.astype(BF16), vblk)
        aft_ref[...] = aft

    def live(kt):
        kt = jnp.maximum(kt, 0)
        bound = jnp.sqrt(qn2 * kmax_ref[kt, hd]) * c2 + aft_ref[:, 0:1]
        return jnp.max(bound) > -SB_SKIP_LOG2

    row = lax.broadcasted_iota(jnp.int32, (tq, tk), 0)
    col = lax.broadcasted_iota(jnp.int32, (tq, tk), 1)
    tile(pl.multiple_of(i * tq, tq), row > col)

    def cond(carry):
        n, go = carry
        return jnp.logical_and(n < i, go)

    def body(carry):
        n, _ = carry
        kt = i - 1 - n
        tile(pl.multiple_of(kt * tk, tk), None)
        return n + 1, live(kt - 1)

    lax.while_loop(cond, body, (jnp.int32(0), live(i - 1)))
    o_ref[...] = acc_ref[...].astype(o_ref.dtype)


def _stick_breaking(proj, tq):
    s = proj.shape[0]
    qb = COL_SB_Q // SB_DH
    r = jnp.arange(LANES)
    ntri = -jnp.concatenate([(r[:, None] >= r[None, :]).astype(BF16), jnp.ones((LANES, LANES), BF16)], axis=1)
    kmax = _sb_key_norms(proj, tq)
    return pl.pallas_call(
        _sb_kernel,
        grid=(SB_HEADS, s // tq),
        in_specs=[
            pl.BlockSpec(memory_space=pltpu.SMEM),
            pl.BlockSpec((tq, SB_DH), lambda h, i: (i, qb + h)),
            pl.BlockSpec((s, SB_DH), lambda h, i: (0, qb + SB_HEADS + h)),
            pl.BlockSpec((s, SB_DH), lambda h, i: (0, qb + 2 * SB_HEADS + h)),
            pl.BlockSpec((LANES, 2 * LANES), lambda h, i: (0, 0)),
        ],
        out_specs=pl.BlockSpec((tq, SB_DH), lambda h, i: (i, h)),
        out_shape=jax.ShapeDtypeStruct((s, SB_WIDTH), BF16),
        scratch_shapes=[pltpu.VMEM((tq, SB_DH), F32), pltpu.VMEM((tq, LANES), F32)],
        compiler_params=_cparams(("parallel", "parallel")),
        name="stick_breaking",
    )(kmax, proj, proj, proj, ntri)


def _merge_kernel(h_ref, g0_ref, g1_ref, g2_ref, ydn_ref, yswa_ref, ysb_ref, wdn_ref, wswa_ref, wsb_ref, wo_ref,
                  o_ref):
    merged = _sigmoid(g0_ref[...].astype(F32)) * _dot(ydn_ref[...], wdn_ref[...])
    merged = merged + _sigmoid(g1_ref[...].astype(F32)) * _dot(yswa_ref[...], wswa_ref[...])
    merged = merged + _sigmoid(g2_ref[...].astype(F32)) * _dot(ysb_ref[...], wsb_ref[...])
    o_ref[...] = h_ref[...] + _dot(merged.astype(BF16), wo_ref[...])


def _merge(h, proj, y_dn, y_swa, y_sb, w_dn, w_swa, w_sb, w_o, tm):
    s = h.shape[0]
    rows = lambda width, col=0: pl.BlockSpec((tm, width), lambda i: (i, col))
    const = lambda a: pl.BlockSpec(a.shape, lambda i: (0, 0), pipeline_mode=pl.Buffered(1))
    return pl.pallas_call(
        _merge_kernel,
        grid=(s // tm,),
        in_specs=[
            rows(D_MODEL), rows(D_MODEL, 0), rows(D_MODEL, 1), rows(D_MODEL, 2),
            rows(DN_WIDTH), rows(SWA_WIDTH), rows(SB_WIDTH),
            const(w_dn), const(w_swa), const(w_sb), const(w_o),
        ],
        out_specs=rows(D_MODEL),
        out_shape=jax.ShapeDtypeStruct((s, D_MODEL), F32),
        compiler_params=_cparams(("parallel",)),
        name="merge",
    )(h, proj, proj, proj, y_dn, y_swa, y_sb, w_dn, w_swa, w_sb, w_o)


def _norm_matmul_kernel(x_ref, g_ref, w_ref, o_ref):
    xn = _rms_rows(x_ref[...], g_ref[...]).astype(BF16)
    o_ref[...] = _dot(xn, w_ref[...]).astype(o_ref.dtype)


def _mem_kv(mem, g, w):
    m = mem.shape[0]
    n = w.shape[1]
    return pl.pallas_call(
        _norm_matmul_kernel,
        grid=(1,),
        in_specs=[
            pl.BlockSpec((m, D_MODEL), lambda i: (0, 0)),
            pl.BlockSpec((1, D_MODEL), lambda i: (0, 0)),
            pl.BlockSpec((D_MODEL, n), lambda i: (0, 0)),
        ],
        out_specs=pl.BlockSpec((m, n), lambda i: (0, 0)),
        out_shape=jax.ShapeDtypeStruct((m, n), BF16),
        compiler_params=_cparams(("arbitrary",)),
        name="mem_kv",
    )(mem, g, w)


def _xattn_kernel(h_ref, g_ref, kv_ref, wq_ref, wo_ref, o_ref):
    h = h_ref[...]
    hn = _rms_rows(h, g_ref[...]).astype(BF16)
    q = _dot(hn, wq_ref[...]).astype(BF16)
    outs = []
    for hd in range(X_HEADS):
        cs = slice(hd * X_DH, (hd + 1) * X_DH)
        kh = kv_ref[:, cs]
        vh = kv_ref[:, X_WIDTH + hd * X_DH:X_WIDTH + (hd + 1) * X_DH]
        sc = _dot_nt(q[:, cs], kh) * (X_DH ** -0.5)
        mx = jnp.max(sc, axis=-1, keepdims=True)
        p = jnp.exp(sc - mx)
        den = jnp.sum(p, axis=-1, keepdims=True)
        outs.append((_dot(p.astype(BF16), vh) / den).astype(BF16))
    o = jnp.concatenate(outs, axis=1)
    o_ref[...] = h + _dot(o, wo_ref[...])


def _xattn(h, g, kv, w_q, w_o, tm):
    s = h.shape[0]
    const = lambda a: pl.BlockSpec(a.shape, lambda i: (0, 0))
    return pl.pallas_call(
        _xattn_kernel,
        grid=(s // tm,),
        in_specs=[pl.BlockSpec((tm, D_MODEL), lambda i: (i, 0)), const(g), const(kv), const(w_q), const(w_o)],
        out_specs=pl.BlockSpec((tm, D_MODEL), lambda i: (i, 0)),
        out_shape=jax.ShapeDtypeStruct((s, D_MODEL), F32),
        compiler_params=_cparams(("parallel",)),
        name="xattn",
    )(h, g, kv, w_q, w_o)


def _ffn_kernel(h_ref, g_ref, wg_ref, wv_ref, cg_ref, cv_ref, wd_ref, o_ref, hn_ref, acc_ref, gbuf, vbuf, carry):
    i = pl.program_id(0)
    j = pl.program_id(1)
    tm = h_ref.shape[0]
    halo = SUBLANES

    @pl.when(j == 0)
    def _():
        hn_ref[...] = _rms_rows(h_ref[...], g_ref[...]).astype(BF16)
        acc_ref[...] = jnp.zeros(acc_ref.shape, F32)

    @pl.when(i == 0)
    def _():
        carry[j] = jnp.zeros(carry.shape[1:], F32)

    hn = hn_ref[...]
    gbuf[0:halo, :] = carry[j, 0]
    vbuf[0:halo, :] = carry[j, 1]
    gbuf[halo:, :] = _dot(hn, wg_ref[...])
    vbuf[halo:, :] = _dot(hn, wv_ref[...])
    carry[j, 0] = gbuf[tm:tm + halo, :]
    carry[j, 1] = vbuf[tm:tm + halo, :]

    def conv(buf, cw_ref):
        base = halo - (FFN_CONV - 1)
        out = cw_ref[0:1, :] * buf[base:base + tm, :]
        for tap in range(1, FFN_CONV):
            out = out + cw_ref[tap:tap + 1, :] * buf[base + tap:base + tap + tm, :]
        return out

    gate = conv(gbuf, cg_ref)
    val = conv(vbuf, cv_ref)
    act = (gate * _sigmoid(gate) * val).astype(BF16)
    acc_ref[...] += _dot(act, wd_ref[...])

    @pl.when(j == pl.num_programs(1) - 1)
    def _():
        o_ref[...] = h_ref[...] + acc_ref[...]


def _ffn(h, g, w_up, conv_w, w_down, tm, tn):
    s = h.shape[0]
    nj = D_FF // tn
    return pl.pallas_call(
        _ffn_kernel,
        grid=(s // tm, nj),
        in_specs=[
            pl.BlockSpec((tm, D_MODEL), lambda i, j: (i, 0)),
            pl.BlockSpec((1, D_MODEL), lambda i, j: (0, 0)),
            pl.BlockSpec((D_MODEL, tn), lambda i, j: (0, j)),
            pl.BlockSpec((D_MODEL, tn), lambda i, j: (0, j + nj)),
            pl.BlockSpec((FFN_CONV, tn), lambda i, j: (0, j)),
            pl.BlockSpec((FFN_CONV, tn), lambda i, j: (0, j + nj)),
            pl.BlockSpec((tn, D_MODEL), lambda i, j: (j, 0)),
        ],
        out_specs=pl.BlockSpec((tm, D_MODEL), lambda i, j: (i, 0)),
        out_shape=jax.ShapeDtypeStruct((s, D_MODEL), F32),
        scratch_shapes=[
            pltpu.VMEM((tm, D_MODEL), BF16),
            pltpu.VMEM((tm, D_MODEL), F32),
            pltpu.VMEM((tm + SUBLANES, tn), F32),
            pltpu.VMEM((tm + SUBLANES, tn), F32),
            pltpu.VMEM((nj, 2, SUBLANES, tn), F32),
        ],
        compiler_params=_cparams(("arbitrary", "arbitrary")),
        name="conv_ffn",
    )(h, g, w_up, w_up, conv_w, conv_w, w_down)


def _final_norm_kernel(h_ref, g_ref, o_ref):
    o_ref[...] = _rms_rows(h_ref[...], g_ref[...])


def _final_norm(h, g, tm):
    s = h.shape[0]
    return pl.pallas_call(
        _final_norm_kernel,
        grid=(s // tm,),
        in_specs=[pl.BlockSpec((tm, D_MODEL), lambda i: (i, 0)), pl.BlockSpec((1, D_MODEL), lambda i: (0, 0))],
        out_specs=pl.BlockSpec((tm, D_MODEL), lambda i: (i, 0)),
        out_shape=jax.ShapeDtypeStruct((s, D_MODEL), F32),
        compiler_params=_cparams(("parallel",)),
        name="final_norm",
    )(h, g)


def _tile(s, want):
    return min(s, want)


def _reorder_w_in(w_in):
    o_z = 2 * DN_QK_WIDTH + DN_WIDTH
    o_a = o_z + DN_WIDTH
    o_swq = o_a + 2 * DN_HEADS
    o_swkv = o_swq + SWA_WIDTH
    o_sb = o_swkv + 2 * SWA_KV_WIDTH
    o_gate = o_sb + 3 * SB_WIDTH
    big = jnp.concatenate(
        [w_in[..., o_gate:], w_in[..., 0:o_a], w_in[..., o_swq:o_swkv], w_in[..., o_sb:o_gate],
         w_in[..., o_swkv:o_sb]], axis=-1).astype(BF16)
    ab = w_in[..., o_a:o_swq]
    ab = jnp.pad(ab, ((0, 0), (0, 0), (0, AB_COLS - ab.shape[-1]))).astype(BF16)
    return big, ab


def kernel(x, mem, norm_mix, w_in, dn_conv, dn_a_log, dn_dt_bias, dn_norm, swa_sinks, w_br_dn, w_br_swa, w_br_sb, w_o, norm_xattn, norm_mem, w_xq, w_xkv, w_xo, norm_ffn, w_up, ffn_conv, w_down, norm_final):
    depth = w_in.shape[0]
    s = x.shape[1]
    assert x.shape[0] == 1 and s % 256 == 0
    h = x[0]
    mem2 = mem[0]

    w_big, w_ab = _reorder_w_in(w_in)
    w_br_dn_b = w_br_dn.astype(BF16)
    w_br_swa_b = w_br_swa.astype(BF16)
    w_br_sb_b = w_br_sb.astype(BF16)
    w_o_b = w_o.astype(BF16)
    w_xq_b = w_xq.astype(BF16)
    w_xkv_b = w_xkv.astype(BF16)
    w_xo_b = w_xo.astype(BF16)
    w_up_b = w_up.astype(BF16)
    w_down_b = w_down.astype(BF16)
    pad8 = lambda a: jnp.pad(a, ((0, 0), (0, AB_COLS - a.shape[-1])))
    a_log_pad = pad8(dn_a_log)
    dt_bias_pad = pad8(dn_dt_bias)

    for l in range(depth):
        proj, ab = _in_proj(h, norm_mix[l][None], w_big[l], w_ab[l], tm=_tile(s, 512), tn=1792)
        y_dn = _delta_net(proj, ab, dn_conv[l], a_log_pad[l][None], dt_bias_pad[l][None], dn_norm[l][None],
                          t=_tile(s, 256))
        y_swa = _swa(proj, swa_sinks[l], tq=_tile(s, 512))
        y_sb = _stick_breaking(proj, tq=_tile(s, 256))
        h = _merge(h, proj, y_dn, y_swa, y_sb, w_br_dn_b[l], w_br_swa_b[l], w_br_sb_b[l], w_o_b[l],
                   tm=_tile(s, 256))
        kv = _mem_kv(mem2, norm_mem[l][None], w_xkv_b[l])
        h = _xattn(h, norm_xattn[l][None], kv, w_xq_b[l], w_xo_b[l], tm=_tile(s, 512))
        h = _ffn(h, norm_ffn[l][None], w_up_b[l], ffn_conv[l], w_down_b[l], tm=_tile(s, 512), tn=512)
    return _final_norm(h, norm_final[None], tm=_tile(s, 1024))[None]
```

```python
import functools

import jax
import jax.numpy as jnp
from jax import lax
from jax.experimental import pallas as pl
from jax.experimental.pallas import tpu as pltpu

F32 = jnp.float32
BF16 = jnp.bfloat16
HIGHEST = lax.Precision.HIGHEST

D_MODEL = 2048
EPS = 1e-6
DN_HEADS = 8
DN_DK = 128
DN_DV = 128
DN_CONV = 4
DN_CHUNK = 64
SWA_HEADS = 8
SWA_KV_HEADS = 2
SWA_DH = 64
WINDOW = 128
SB_HEADS = 4
SB_DH = 128
X_HEADS = 4
X_DH = 128
D_FF = 4096
FFN_CONV = 3
N_BRANCH = 3

DN_QK_WIDTH = DN_HEADS * DN_DK
DN_WIDTH = DN_HEADS * DN_DV
SWA_WIDTH = SWA_HEADS * SWA_DH
SWA_KV_WIDTH = SWA_KV_HEADS * SWA_DH
SB_WIDTH = SB_HEADS * SB_DH
X_WIDTH = X_HEADS * X_DH

LANES = 128
SUBLANES = 8
VMEM_LIMIT = 56 * 1024 * 1024

PROJ_COLS = N_BRANCH * D_MODEL + 3 * DN_QK_WIDTH + DN_WIDTH + SWA_WIDTH + 3 * SB_WIDTH + 2 * SWA_KV_WIDTH
COL_DN_Q = N_BRANCH * D_MODEL
COL_DN_Z = COL_DN_Q + 3 * DN_QK_WIDTH
COL_SWA_Q = COL_DN_Z + DN_WIDTH
COL_SB_Q = COL_SWA_Q + SWA_WIDTH
COL_SWA_KV = COL_SB_Q + 3 * SB_WIDTH
AB_COLS = LANES

NEG_BIG = -1e30
LOG2E = 1.4426950408889634
SB_SKIP_LOG2 = 170.0


def _cparams(semantics):
    return pltpu.CompilerParams(dimension_semantics=semantics, vmem_limit_bytes=VMEM_LIMIT)


def _dot(a, b):
    return jnp.dot(a, b, preferred_element_type=F32)


def _dot_nt(a, b, precision=None):
    return lax.dot_general(a, b, (((1,), (1,)), ((), ())), preferred_element_type=F32, precision=precision)


def _dot_hi(a, b):
    return jnp.dot(a, b, preferred_element_type=F32, precision=HIGHEST)


def _dot_solve(a, b):
    return jnp.dot(a.astype(BF16), b.astype(BF16), preferred_element_type=F32)


def _sigmoid(x):
    return 1.0 / (1.0 + jnp.exp(-x))


def _softplus(x):
    return jnp.maximum(x, 0.0) + jnp.log1p(jnp.exp(-jnp.abs(x)))


def _rms_rows(x, g):
    ms = jnp.mean(x * x, axis=-1, keepdims=True)
    return x * lax.rsqrt(ms + EPS) * g


def _in_proj_kernel(h_ref, g_ref, w_ref, wab_ref, o_ref, ab_ref, xn_ref):
    @pl.when(pl.program_id(1) == 0)
    def _():
        xn = _rms_rows(h_ref[...], g_ref[...]).astype(BF16)
        xn_ref[...] = xn
        ab_ref[...] = _dot(xn, wab_ref[...])

    o_ref[...] = _dot(xn_ref[...], w_ref[...]).astype(o_ref.dtype)


def _in_proj(h, g, w, wab, tm, tn):
    s = h.shape[0]
    n = w.shape[1]
    return pl.pallas_call(
        _in_proj_kernel,
        grid=(s // tm, n // tn),
        in_specs=[
            pl.BlockSpec((tm, D_MODEL), lambda i, j: (i, 0)),
            pl.BlockSpec((1, D_MODEL), lambda i, j: (0, 0)),
            pl.BlockSpec((D_MODEL, tn), lambda i, j: (0, j)),
            pl.BlockSpec((D_MODEL, AB_COLS), lambda i, j: (0, 0)),
        ],
        out_specs=[
            pl.BlockSpec((tm, tn), lambda i, j: (i, j)),
            pl.BlockSpec((tm, AB_COLS), lambda i, j: (i, 0)),
        ],
        out_shape=[jax.ShapeDtypeStruct((s, n), BF16), jax.ShapeDtypeStruct((s, AB_COLS), F32)],
        scratch_shapes=[pltpu.VMEM((tm, D_MODEL), BF16)],
        compiler_params=_cparams(("parallel", "arbitrary")),
        name="in_proj",
    )(h, g, w, wab)


def _dn_kernel(q_ref, k_ref, v_ref, z_ref, ab_ref, cw_ref, alog_ref, dtb_ref, nrm_ref, cum_ref, eg_ref, eb_ref,
               o_ref, xbuf, qs, ks, vs, gce, gle, ble, gct, qb, mb, qkb, rbv, rbk, ub, wb, qdb, kdt, state):
    i = pl.program_id(0)
    t = q_ref.shape[0]
    c = DN_CHUNK
    halo = SUBLANES

    @pl.when(i == 0)
    def _():
        xbuf[0:halo, :] = jnp.zeros((halo, xbuf.shape[1]), F32)
        state[...] = jnp.zeros(state.shape, F32)

    @pl.when(i > 0)
    def _():
        xbuf[0:halo, :] = xbuf[t:t + halo, :]

    xbuf[halo:halo + t, 0:DN_QK_WIDTH] = q_ref[...].astype(F32)
    xbuf[halo:halo + t, DN_QK_WIDTH:2 * DN_QK_WIDTH] = k_ref[...].astype(F32)
    xbuf[halo:halo + t, 2 * DN_QK_WIDTH:] = v_ref[...].astype(F32)

    n_slabs = (2 * DN_QK_WIDTH + DN_WIDTH) // LANES
    for s in range(n_slabs):
        cs = slice(s * LANES, (s + 1) * LANES)
        acc = cw_ref[0:1, cs] * xbuf[halo - 3:halo - 3 + t, cs]
        for tap in range(1, DN_CONV):
            acc = acc + cw_ref[tap:tap + 1, cs] * xbuf[halo - 3 + tap:halo - 3 + tap + t, cs]
        y = acc * _sigmoid(acc)
        if s < 2 * DN_HEADS:
            y = y * lax.rsqrt(jnp.sum(y * y, axis=-1, keepdims=True) + EPS)
        if s < DN_HEADS:
            qs[:, cs] = y * (DN_DK ** -0.5)
        elif s < 2 * DN_HEADS:
            ks[:, (s - DN_HEADS) * LANES:(s - DN_HEADS + 1) * LANES] = y
        else:
            vs[:, (s - 2 * DN_HEADS) * LANES:(s - 2 * DN_HEADS + 1) * LANES] = y

    ab = ab_ref[...]
    g = -jnp.exp(alog_ref[...]) * _softplus(ab + dtb_ref[...])
    beta = _sigmoid(ab)
    cums = _dot_hi(cum_ref[...], g)
    gce[...] = _dot_hi(cums[0:t], eg_ref[...])
    gle[...] = _dot_hi(cums[t:2 * t], eg_ref[...])
    ble[...] = _dot_hi(beta, eb_ref[...])
    gct[...] = cums[0:t].T

    row = lax.broadcasted_iota(jnp.int32, (c, c), 0)
    col = lax.broadcasted_iota(jnp.int32, (c, c), 1)
    causal = row >= col
    strict = row > col
    nrm = nrm_ref[...]
    items = [(ci, hd) for ci in range(t // c) for hd in range(DN_HEADS)]

    for it, (ci, hd) in enumerate(items):
        rows = slice(ci * c, (ci + 1) * c)
        cs = slice(hd * LANES, (hd + 1) * LANES)
        q_ = qs[rows, cs]
        k_ = ks[rows, cs]
        gcc = gce[rows, cs]
        kf = k_.astype(BF16)
        g_row = gct[hd:hd + 1, rows]
        dm = jnp.where(causal, gcc[:, 0:c] - g_row, 0.0)
        decay = jnp.where(causal, jnp.exp(dm), 0.0)
        kk = _dot_nt((k_ * ble[rows, cs]).astype(BF16), kf) * decay
        nmat = jnp.where(strict, kk, 0.0)
        qb[it] = -nmat
        mb[it] = nmat
        qkb[it] = (_dot_nt(q_.astype(BF16), kf) * decay).astype(BF16)

    for it, (ci, hd) in enumerate(items):
        rows = slice(ci * c, (ci + 1) * c)
        cs = slice(hd * LANES, (hd + 1) * LANES)
        q_ = qs[rows, cs]
        k_ = ks[rows, cs]
        gcc = gce[rows, cs]
        bet = ble[rows, cs]
        egc = jnp.exp(gcc)
        rbv[rows, cs] = vs[rows, cs] * bet
        rbk[rows, cs] = k_ * bet * egc
        qdb[rows, cs] = (q_ * egc).astype(BF16)
        kdt[it] = (k_ * jnp.exp(gle[rows, cs] - gcc)).T.astype(BF16)

    for it in range(len(items)):
        nmat = mb[it]
        mb[it] = _dot_solve(nmat, nmat)
    levels = c.bit_length() - 2
    for lvl in range(levels):
        for it in range(len(items)):
            qm = qb[it]
            m = mb[it]
            qb[it] = qm + m + _dot_solve(qm, m)
            if lvl < levels - 1:
                mb[it] = _dot_solve(m, m)

    for it, (ci, hd) in enumerate(items):
        rows = slice(ci * c, (ci + 1) * c)
        cs = slice(hd * LANES, (hd + 1) * LANES)
        rhs = jnp.concatenate([rbv[rows, cs], rbk[rows, cs]], axis=1)
        uw = rhs + _dot_solve(qb[it], rhs)
        ub[rows, cs] = uw[:, 0:DN_DV]
        wb[rows, cs] = uw[:, DN_DV:].astype(BF16)

    for it, (ci, hd) in enumerate(items):
        rows = slice(ci * c, (ci + 1) * c)
        cs = slice(hd * LANES, (hd + 1) * LANES)
        st = state[hd]
        stb = st.astype(BF16)
        v_new = ub[rows, cs] - _dot(wb[rows, cs], stb)
        vnb = v_new.astype(BF16)
        o = _dot(qdb[rows, cs], stb) + _dot(qkb[it], vnb)
        egl = jnp.exp(gle[rows, cs])
        state[hd] = st * jnp.concatenate([egl, egl], axis=0) + _dot(kdt[it], vnb)
        zz = z_ref[rows, cs].astype(F32)
        y = _rms_rows(o, nrm) * (zz * _sigmoid(zz))
        o_ref[rows, cs] = y.astype(o_ref.dtype)


def _dn_constants(t):
    r = jnp.arange(t)
    same = (r[:, None] // DN_CHUNK) == (r[None, :] // DN_CHUNK)
    lower = r[:, None] >= r[None, :]
    cum = jnp.concatenate([(same & lower).astype(F32), same.astype(F32)], axis=0)
    lane = jnp.arange(LANES)[:, None]
    head = jnp.arange(DN_WIDTH)[None, :] // DN_DV
    eg = (lane == head).astype(F32)
    eb = (lane == head + DN_HEADS).astype(F32)
    return cum, eg, eb


def _delta_net(proj, ab, conv_w, a_log_pad, dt_bias_pad, dn_norm, t):
    s = proj.shape[0]
    cum, eg, eb = _dn_constants(t)
    cb = COL_DN_Q // DN_QK_WIDTH
    n_items = (t // DN_CHUNK) * DN_HEADS
    const = lambda shape: pl.BlockSpec(shape, lambda i: (0, 0))
    return pl.pallas_call(
        _dn_kernel,
        grid=(s // t,),
        in_specs=[
            pl.BlockSpec((t, DN_QK_WIDTH), lambda i: (i, cb)),
            pl.BlockSpec((t, DN_QK_WIDTH), lambda i: (i, cb + 1)),
            pl.BlockSpec((t, DN_WIDTH), lambda i: (i, cb + 2)),
            pl.BlockSpec((t, DN_WIDTH), lambda i: (i, cb + 3)),
            pl.BlockSpec((t, AB_COLS), lambda i: (i, 0)),
            const((DN_CONV, 2 * DN_QK_WIDTH + DN_WIDTH)),
            const((1, AB_COLS)),
            const((1, AB_COLS)),
            const((1, DN_DV)),
            const((2 * t, t)),
            const((LANES, DN_WIDTH)),
            const((LANES, DN_WIDTH)),
        ],
        out_specs=pl.BlockSpec((t, DN_WIDTH), lambda i: (i, 0)),
        out_shape=jax.ShapeDtypeStruct((s, DN_WIDTH), BF16),
        scratch_shapes=[
            pltpu.VMEM((t + SUBLANES, 2 * DN_QK_WIDTH + DN_WIDTH), F32),
            pltpu.VMEM((t, DN_QK_WIDTH), F32),
            pltpu.VMEM((t, DN_QK_WIDTH), F32),
            pltpu.VMEM((t, DN_WIDTH), F32),
            pltpu.VMEM((t, DN_WIDTH), F32),
            pltpu.VMEM((t, DN_WIDTH), F32),
            pltpu.VMEM((t, DN_WIDTH), F32),
            pltpu.VMEM((LANES, t), F32),
            pltpu.VMEM((n_items, DN_CHUNK, DN_CHUNK), F32),
            pltpu.VMEM((n_items, DN_CHUNK, DN_CHUNK), F32),
            pltpu.VMEM((n_items, DN_CHUNK, DN_CHUNK), BF16),
            pltpu.VMEM((t, DN_WIDTH), F32),
            pltpu.VMEM((t, DN_QK_WIDTH), F32),
            pltpu.VMEM((t, DN_WIDTH), F32),
            pltpu.VMEM((t, DN_QK_WIDTH), BF16),
            pltpu.VMEM((t, DN_QK_WIDTH), BF16),
            pltpu.VMEM((n_items, DN_DK, DN_CHUNK), BF16),
            pltpu.VMEM((DN_HEADS, DN_DK, DN_DV), F32),
        ],
        compiler_params=_cparams(("arbitrary",)),
        name="delta_net",
    )(proj, proj, proj, proj, ab, conv_w, a_log_pad, dt_bias_pad, dn_norm, cum, eg, eb)


def _swa_kernel(sinks_ref, q_ref, kv_ref, kvp_ref, o_ref):
    i = pl.program_id(0)
    tq = q_ref.shape[0]
    w = WINDOW
    nwb = tq // w
    grp = SWA_HEADS // SWA_KV_HEADS
    ext = jnp.concatenate([kvp_ref[...], kv_ref[...]], axis=0)
    qi = lax.broadcasted_iota(jnp.int32, (w, 2 * w), 0)
    kj = lax.broadcasted_iota(jnp.int32, (w, 2 * w), 1)
    dist = qi + w - kj
    in_band = (dist >= 0) & (dist < w)
    distf = dist.astype(F32)
    for wb in range(nwb):
        blk = i * nwb + wb
        valid = in_band & (blk * w + kj - w >= 0)
        win = ext[wb * w:wb * w + 2 * w]
        qwb = q_ref[wb * w:(wb + 1) * w, :]
        outs = []
        for hk in range(SWA_KV_HEADS):
            kh = win[:, hk * SWA_DH:(hk + 1) * SWA_DH]
            vh = win[:, SWA_KV_WIDTH + hk * SWA_DH:SWA_KV_WIDTH + (hk + 1) * SWA_DH]
            for gi in range(grp):
                hd = hk * grp + gi
                slope = 2.0 ** (-8.0 * (hd + 1) / SWA_HEADS)
                qh = qwb[:, hd * SWA_DH:(hd + 1) * SWA_DH]
                sc = _dot_nt(qh, kh) * (SWA_DH ** -0.5) - slope * distf
                sc = jnp.where(valid, sc, NEG_BIG)
                sink = sinks_ref[hd]
                mx = jnp.maximum(jnp.max(sc, axis=-1, keepdims=True), sink)
                p = jnp.exp(sc - mx)
                den = jnp.sum(p, axis=-1, keepdims=True) + jnp.exp(sink - mx)
                outs.append(_dot(p.astype(BF16), vh) / den)
        o_ref[wb * w:(wb + 1) * w, :] = jnp.concatenate(outs, axis=1).astype(o_ref.dtype)


def _swa(proj, sinks, tq):
    s = proj.shape[0]
    nwb = tq // WINDOW
    qb = COL_SWA_Q // SWA_WIDTH
    kvb = COL_SWA_KV // (2 * SWA_KV_WIDTH)
    return pl.pallas_call(
        _swa_kernel,
        grid=(s // tq,),
        in_specs=[
            pl.BlockSpec(memory_space=pltpu.SMEM),
            pl.BlockSpec((tq, SWA_WIDTH), lambda i: (i, qb)),
            pl.BlockSpec((tq, 2 * SWA_KV_WIDTH), lambda i: (i, kvb)),
            pl.BlockSpec((WINDOW, 2 * SWA_KV_WIDTH), lambda i: (jnp.maximum(i * nwb - 1, 0), kvb)),
        ],
        out_specs=pl.BlockSpec((tq, SWA_WIDTH), lambda i: (i, 0)),
        out_shape=jax.ShapeDtypeStruct((s, SWA_WIDTH), BF16),
        compiler_params=_cparams(("parallel",)),
        name="swa",
    )(sinks, proj, proj, proj)


def _sb_knorm_kernel(k_ref, o_ref, run_ref):
    t = pl.program_id(0)

    @pl.when(t == 0)
    def _():
        for hd in range(SB_HEADS):
            run_ref[hd] = 0.0

    k = k_ref[...].astype(F32)
    for hd in range(SB_HEADS):
        kh = k[:, hd * SB_DH:(hd + 1) * SB_DH]
        cur = jnp.maximum(run_ref[hd], jnp.max(jnp.sum(kh * kh, axis=-1, keepdims=True)))
        run_ref[hd] = cur
        o_ref[t, hd] = cur


def _sb_key_norms(proj, tk):
    s = proj.shape[0]
    kb = COL_SB_Q // SB_WIDTH + 1
    return pl.pallas_call(
        _sb_knorm_kernel,
        grid=(s // tk,),
        in_specs=[pl.BlockSpec((tk, SB_WIDTH), lambda t: (t, kb))],
        out_specs=pl.BlockSpec(memory_space=pltpu.SMEM),
        out_shape=jax.ShapeDtypeStruct((s // tk, SB_HEADS), F32),
        scratch_shapes=[pltpu.SMEM((SB_HEADS,), F32)],
        compiler_params=_cparams(("arbitrary",)),
        name="sb_key_norms",
    )(proj)


def _sb_kernel(kmax_ref, q_ref, k_ref, v_ref, ntri_ref, o_ref, acc_ref, aft_ref):
    hd = pl.program_id(0)
    i = pl.program_id(1)
    tq = q_ref.shape[0]
    tk = tq
    blk = LANES
    nsub = tk // blk
    c2 = (SB_DH ** -0.5) * LOG2E
    q = q_ref[...]
    ntri = ntri_ref[...]
    acc_ref[...] = jnp.zeros(acc_ref.shape, F32)
    aft_ref[...] = jnp.zeros(aft_ref.shape, F32)
    qf = q.astype(F32)
    qn2 = jnp.sum(qf * qf, axis=-1, keepdims=True)

    def tile(k0, mask):
        kblk = k_ref[pl.ds(k0, tk), :]
        vblk = v_ref[pl.ds(k0, tk), :]
        z2 = _dot_nt(q, kblk) * c2
        sp2 = jnp.maximum(z2, 0.0) + jnp.log(1.0 + jnp.exp2(-jnp.abs(z2))) * LOG2E
        if mask is not None:
            sp2 = jnp.where(mask, sp2, 0.0)
        aft = aft_ref[...]
        revs = [None] * nsub
        for sb in reversed(range(nsub)):
            cum = _dot(sp2[:, sb * blk:(sb + 1) * blk].astype(BF16), ntri)
            revs[sb] = cum[:, 0:blk] + aft
            aft = aft + cum[:, blk:]
        a = jnp.exp2(z2 + jnp.concatenate(revs, axis=1))
        if mask is not None:
            a = jnp.where(mask, a, 0.0)
        acc_ref[...] += _dot(a.astype(BF16), vblk)
        aft_ref[...] = aft

    def live(kt):
        kt = jnp.maximum(kt, 0)
        bound = jnp.sqrt(qn2 * kmax_ref[kt, hd]) * c2 + aft_ref[:, 0:1]
        return jnp.max(bound) > -SB_SKIP_LOG2

    row = lax.broadcasted_iota(jnp.int32, (tq, tk), 0)
    col = lax.broadcasted_iota(jnp.int32, (tq, tk), 1)
    tile(pl.multiple_of(i * tq, tq), row > col)

    def cond(carry):
        n, go = carry
        return jnp.logical_and(n < i, go)

    def body(carry):
        n, _ = carry
        kt = i - 1 - n
        tile(pl.multiple_of(kt * tk, tk), None)
        return n + 1, live(kt - 1)

    lax.while_loop(cond, body, (jnp.int32(0), live(i - 1)))
    o_ref[...] = acc_ref[...].astype(o_ref.dtype)


def _stick_breaking(proj, tq):
    s = proj.shape[0]
    qb = COL_SB_Q // SB_DH
    r = jnp.arange(LANES)
    ntri = -jnp.concatenate([(r[:, None] >= r[None, :]).astype(BF16), jnp.ones((LANES, LANES), BF16)], axis=1)
    kmax = _sb_key_norms(proj, tq)
    return pl.pallas_call(
        _sb_kernel,
        grid=(SB_HEADS, s // tq),
        in_specs=[
            pl.BlockSpec(memory_space=pltpu.SMEM),
            pl.BlockSpec((tq, SB_DH), lambda h, i: (i, qb + h)),
            pl.BlockSpec((s, SB_DH), lambda h, i: (0, qb + SB_HEADS + h)),
            pl.BlockSpec((s, SB_DH), lambda h, i: (0, qb + 2 * SB_HEADS + h)),
            pl.BlockSpec((LANES, 2 * LANES), lambda h, i: (0, 0)),
        ],
        out_specs=pl.BlockSpec((tq, SB_DH), lambda h, i: (i, h)),
        out_shape=jax.ShapeDtypeStruct((s, SB_WIDTH), BF16),
        scratch_shapes=[pltpu.VMEM((tq, SB_DH), F32), pltpu.VMEM((tq, LANES), F32)],
        compiler_params=_cparams(("parallel", "parallel")),
        name="stick_breaking",
    )(kmax, proj, proj, proj, ntri)


def _merge_kernel(h_ref, g0_ref, g1_ref, g2_ref, ydn_ref, yswa_ref, ysb_ref, wdn_ref, wswa_ref, wsb_ref, wo_ref,
                  o_ref):
    merged = _sigmoid(g0_ref[...].astype(F32)) * _dot(ydn_ref[...], wdn_ref[...])
    merged = merged + _sigmoid(g1_ref[...].astype(F32)) * _dot(yswa_ref[...], wswa_ref[...])
    merged = merged + _sigmoid(g2_ref[...].astype(F32)) * _dot(ysb_ref[...], wsb_ref[...])
    o_ref[...] = h_ref[...] + _dot(merged.astype(BF16), wo_ref[...])


def _merge(h, proj, y_dn, y_swa, y_sb, w_dn, w_swa, w_sb, w_o, tm):
    s = h.shape[0]
    rows = lambda width, col=0: pl.BlockSpec((tm, width), lambda i: (i, col))
    const = lambda a: pl.BlockSpec(a.shape, lambda i: (0, 0), pipeline_mode=pl.Buffered(1))
    return pl.pallas_call(
        _merge_kernel,
        grid=(s // tm,),
        in_specs=[
            rows(D_MODEL), rows(D_MODEL, 0), rows(D_MODEL, 1), rows(D_MODEL, 2),
            rows(DN_WIDTH), rows(SWA_WIDTH), rows(SB_WIDTH),
            const(w_dn), const(w_swa), const(w_sb), const(w_o),
        ],
        out_specs=rows(D_MODEL),
        out_shape=jax.ShapeDtypeStruct((s, D_MODEL), F32),
        compiler_params=_cparams(("parallel",)),
        name="merge",
    )(h, proj, proj, proj, y_dn, y_swa, y_sb, w_dn, w_swa, w_sb, w_o)


def _norm_matmul_kernel(x_ref, g_ref, w_ref, o_ref):
    xn = _rms_rows(x_ref[...], g_ref[...]).astype(BF16)
    o_ref[...] = _dot(xn, w_ref[...]).astype(o_ref.dtype)


def _mem_kv(mem, g, w):
    m = mem.shape[0]
    n = w.shape[1]
    return pl.pallas_call(
        _norm_matmul_kernel,
        grid=(1,),
        in_specs=[
            pl.BlockSpec((m, D_MODEL), lambda i: (0, 0)),
            pl.BlockSpec((1, D_MODEL), lambda i: (0, 0)),
            pl.BlockSpec((D_MODEL, n), lambda i: (0, 0)),
        ],
        out_specs=pl.BlockSpec((m, n), lambda i: (0, 0)),
        out_shape=jax.ShapeDtypeStruct((m, n), BF16),
        compiler_params=_cparams(("arbitrary",)),
        name="mem_kv",
    )(mem, g, w)


def _xattn_kernel(h_ref, g_ref, kv_ref, wq_ref, wo_ref, o_ref):
    h = h_ref[...]
    hn = _rms_rows(h, g_ref[...]).astype(BF16)
    q = _dot(hn, wq_ref[...]).astype(BF16)
    outs = []
    for hd in range(X_HEADS):
        cs = slice(hd * X_DH, (hd + 1) * X_DH)
        kh = kv_ref[:, cs]
        vh = kv_ref[:, X_WIDTH + hd * X_DH:X_WIDTH + (hd + 1) * X_DH]
        sc = _dot_nt(q[:, cs], kh) * (X_DH ** -0.5)
        mx = jnp.max(sc, axis=-1, keepdims=True)
        p = jnp.exp(sc - mx)
        den = jnp.sum(p, axis=-1, keepdims=True)
        outs.append((_dot(p.astype(BF16), vh) / den).astype(BF16))
    o = jnp.concatenate(outs, axis=1)
    o_ref[...] = h + _dot(o, wo_ref[...])


def _xattn(h, g, kv, w_q, w_o, tm):
    s = h.shape[0]
    const = lambda a: pl.BlockSpec(a.shape, lambda i: (0, 0))
    return pl.pallas_call(
        _xattn_kernel,
        grid=(s // tm,),
        in_specs=[pl.BlockSpec((tm, D_MODEL), lambda i: (i, 0)), const(g), const(kv), const(w_q), const(w_o)],
        out_specs=pl.BlockSpec((tm, D_MODEL), lambda i: (i, 0)),
        out_shape=jax.ShapeDtypeStruct((s, D_MODEL), F32),
        compiler_params=_cparams(("parallel",)),
        name="xattn",
    )(h, g, kv, w_q, w_o)


def _ffn_kernel(h_ref, g_ref, wg_ref, wv_ref, cg_ref, cv_ref, wd_ref, o_ref, hn_ref, acc_ref, gbuf, vbuf, carry):
    i = pl.program_id(0)
    j = pl.program_id(1)
    tm = h_ref.shape[0]
    halo = SUBLANES

    @pl.when(j == 0)
    def _():
        hn_ref[...] = _rms_rows(h_ref[...], g_ref[...]).astype(BF16)
        acc_ref[...] = jnp.zeros(acc_ref.shape, F32)

    @pl.when(i == 0)
    def _():
        carry[j] = jnp.zeros(carry.shape[1:], F32)

    hn = hn_ref[...]
    gbuf[0:halo, :] = carry[j, 0]
    vbuf[0:halo, :] = carry[j, 1]
    gbuf[halo:, :] = _dot(hn, wg_ref[...])
    vbuf[halo:, :] = _dot(hn, wv_ref[...])
    carry[j, 0] = gbuf[tm:tm + halo, :]
    carry[j, 1] = vbuf[tm:tm + halo, :]

    def conv(buf, cw_ref):
        base = halo - (FFN_CONV - 1)
        out = cw_ref[0:1, :] * buf[base:base + tm, :]
        for tap in range(1, FFN_CONV):
            out = out + cw_ref[tap:tap + 1, :] * buf[base + tap:base + tap + tm, :]
        return out

    gate = conv(gbuf, cg_ref)
    val = conv(vbuf, cv_ref)
    act = (gate * _sigmoid(gate) * val).astype(BF16)
    acc_ref[...] += _dot(act, wd_ref[...])

    @pl.when(j == pl.num_programs(1) - 1)
    def _():
        o_ref[...] = h_ref[...] + acc_ref[...]


def _ffn(h, g, w_up, conv_w, w_down, tm, tn):
    s = h.shape[0]
    nj = D_FF // tn
    return pl.pallas_call(
        _ffn_kernel,
        grid=(s // tm, nj),
        in_specs=[
            pl.BlockSpec((tm, D_MODEL), lambda i, j: (i, 0)),
            pl.BlockSpec((1, D_MODEL), lambda i, j: (0, 0)),
            pl.BlockSpec((D_MODEL, tn), lambda i, j: (0, j)),
            pl.BlockSpec((D_MODEL, tn), lambda i, j: (0, j + nj)),
            pl.BlockSpec((FFN_CONV, tn), lambda i, j: (0, j)),
            pl.BlockSpec((FFN_CONV, tn), lambda i, j: (0, j + nj)),
            pl.BlockSpec((tn, D_MODEL), lambda i, j: (j, 0)),
        ],
        out_specs=pl.BlockSpec((tm, D_MODEL), lambda i, j: (i, 0)),
        out_shape=jax.ShapeDtypeStruct((s, D_MODEL), F32),
        scratch_shapes=[
            pltpu.VMEM((tm, D_MODEL), BF16),
            pltpu.VMEM((tm, D_MODEL), F32),
            pltpu.VMEM((tm + SUBLANES, tn), F32),
            pltpu.VMEM((tm + SUBLANES, tn), F32),
            pltpu.VMEM((nj, 2, SUBLANES, tn), F32),
        ],
        compiler_params=_cparams(("arbitrary", "arbitrary")),
        name="conv_ffn",
    )(h, g, w_up, w_up, conv_w, conv_w, w_down)


def _final_norm_kernel(h_ref, g_ref, o_ref):
    o_ref[...] = _rms_rows(h_ref[...], g_ref[...])


def _final_norm(h, g, tm):
    s = h.shape[0]
    return pl.pallas_call(
        _final_norm_kernel,
        grid=(s // tm,),
        in_specs=[pl.BlockSpec((tm, D_MODEL), lambda i: (i, 0)), pl.BlockSpec((1, D_MODEL), lambda i: (0, 0))],
        out_specs=pl.BlockSpec((tm, D_MODEL), lambda i: (i, 0)),
        out_shape=jax.ShapeDtypeStruct((s, D_MODEL), F32),
        compiler_params=_cparams(("parallel",)),
        name="final_norm",
    )(h, g)


def _tile(s, want):
    return min(s, want)


def _reorder_w_in(w_in):
    o_z = 2 * DN_QK_WIDTH + DN_WIDTH
    o_a = o_z + DN_WIDTH
    o_swq = o_a + 2 * DN_HEADS
    o_swkv = o_swq + SWA_WIDTH
    o_sb = o_swkv + 2 * SWA_KV_WIDTH
    o_gate = o_sb + 3 * SB_WIDTH
    big = jnp.concatenate(
        [w_in[..., o_gate:], w_in[..., 0:o_a], w_in[..., o_swq:o_swkv], w_in[..., o_sb:o_gate],
         w_in[..., o_swkv:o_sb]], axis=-1).astype(BF16)
    ab = w_in[..., o_a:o_swq]
    ab = jnp.pad(ab, ((0, 0), (0, 0), (0, AB_COLS - ab.shape[-1]))).astype(BF16)
    return big, ab


def kernel(x, mem, norm_mix, w_in, dn_conv, dn_a_log, dn_dt_bias, dn_norm, swa_sinks, w_br_dn, w_br_swa, w_br_sb, w_o, norm_xattn, norm_mem, w_xq, w_xkv, w_xo, norm_ffn, w_up, ffn_conv, w_down, norm_final):
    depth = w_in.shape[0]
    s = x.shape[1]
    assert x.shape[0] == 1 and s % 256 == 0
    h = x[0]
    mem2 = mem[0]

    w_big, w_ab = _reorder_w_in(w_in)
    w_br_dn_b = w_br_dn.astype(BF16)
    w_br_swa_b = w_br_swa.astype(BF16)
    w_br_sb_b = w_br_sb.astype(BF16)
    w_o_b = w_o.astype(BF16)
    w_xq_b = w_xq.astype(BF16)
    w_xkv_b = w_xkv.astype(BF16)
    w_xo_b = w_xo.astype(BF16)
    w_up_b = w_up.astype(BF16)
    w_down_b = w_down.astype(BF16)
    pad8 = lambda a: jnp.pad(a, ((0, 0), (0, AB_COLS - a.shape[-1])))
    a_log_pad = pad8(dn_a_log)
    dt_bias_pad = pad8(dn_dt_bias)

    for l in range(depth):
        proj, ab = _in_proj(h, norm_mix[l][None], w_big[l], w_ab[l], tm=_tile(s, 512), tn=1792)
        y_dn = _delta_net(proj, ab, dn_conv[l], a_log_pad[l][None], dt_bias_pad[l][None], dn_norm[l][None],
                          t=_tile(s, 256))
        y_swa = _swa(proj, swa_sinks[l], tq=_tile(s, 512))
        y_sb = _stick_breaking(proj, tq=_tile(s, 256))
        h = _merge(h, proj, y_dn, y_swa, y_sb, w_br_dn_b[l], w_br_swa_b[l], w_br_sb_b[l], w_o_b[l],
                   tm=_tile(s, 256))
        kv = _mem_kv(mem2, norm_mem[l][None], w_xkv_b[l])
        h = _xattn(h, norm_xattn[l][None], kv, w_xq_b[l], w_xo_b[l], tm=_tile(s, 512))
        h = _ffn(h, norm_ffn[l][None], w_up_b[l], ffn_conv[l], w_down_b[l], tm=_tile(s, 512), tn=512)
    return _final_norm(h, norm_final[None], tm=_tile(s, 1024))[None]
```

```python
import functools

import jax
import jax.numpy as jnp
from jax import lax
from jax.experimental import pallas as pl
from jax.experimental.pallas import tpu as pltpu

F32 = jnp.float32
BF16 = jnp.bfloat16

D_MODEL = 2048
EPS = 1e-6
DN_HEADS = 8
DN_DK = 128
DN_DV = 128
DN_CONV = 4
DN_BLOCK = 128
DN_SOLVE_BASE = 2
SWA_HEADS = 8
SWA_KV_HEADS = 2
SWA_DH = 64
WINDOW = 128
SB_HEADS = 4
SB_DH = 128
X_HEADS = 4
X_DH = 128
D_FF = 4096
FFN_CONV = 3
N_BRANCH = 3

DN_QK_WIDTH = DN_HEADS * DN_DK
DN_WIDTH = DN_HEADS * DN_DV
SWA_WIDTH = SWA_HEADS * SWA_DH
SWA_KV_WIDTH = SWA_KV_HEADS * SWA_DH
SB_WIDTH = SB_HEADS * SB_DH
X_WIDTH = X_HEADS * X_DH

LANES = 128
SUBLANES = 8
VMEM_LIMIT = 56 * 1024 * 1024

PROJ_COLS = N_BRANCH * D_MODEL + 3 * DN_QK_WIDTH + DN_WIDTH + SWA_WIDTH + 3 * SB_WIDTH + 2 * SWA_KV_WIDTH
COL_DN_Q = N_BRANCH * D_MODEL
COL_DN_Z = COL_DN_Q + 3 * DN_QK_WIDTH
COL_SWA_Q = COL_DN_Z + DN_WIDTH
COL_SB_Q = COL_SWA_Q + SWA_WIDTH
COL_SWA_KV = COL_SB_Q + 3 * SB_WIDTH
AB_COLS = LANES

NEG_BIG = -1e30
LOG2E = 1.4426950408889634
SB_SKIP_LOG2 = 170.0


def _cparams(semantics):
    return pltpu.CompilerParams(dimension_semantics=semantics, vmem_limit_bytes=VMEM_LIMIT)


def _dot(a, b):
    return jnp.dot(a, b, preferred_element_type=F32)


def _dot_nt(a, b, precision=None):
    return lax.dot_general(a, b, (((1,), (1,)), ((), ())), preferred_element_type=F32, precision=precision)


def _split3(x):
    x1 = x.astype(BF16)
    r1 = x - x1.astype(F32)
    x2 = r1.astype(BF16)
    x3 = (r1 - x2.astype(F32)).astype(BF16)
    return x1, x2, x3


def _dot_01_lhs(m01, x):
    x1, x2, x3 = _split3(x)
    return _dot(m01, x1) + _dot(m01, x2) + _dot(m01, x3)


def _dot_01_rhs(x, m01):
    x1, x2, x3 = _split3(x)
    return _dot(x1, m01) + _dot(x2, m01) + _dot(x3, m01)


def _dot_solve(a, b):
    return jnp.dot(a.astype(BF16), b.astype(BF16), preferred_element_type=F32)


def _sigmoid(x):
    return 1.0 / (1.0 + jnp.exp(-x))


def _softplus(x):
    return jnp.maximum(x, 0.0) + jnp.log1p(jnp.exp(-jnp.abs(x)))


def _rms_rows(x, g):
    ms = jnp.mean(x * x, axis=-1, keepdims=True)
    return x * lax.rsqrt(ms + EPS) * g


def _in_proj_kernel(h_ref, g_ref, w_ref, wab_ref, o_ref, ab_ref, xn_ref):
    @pl.when(pl.program_id(1) == 0)
    def _():
        xn = _rms_rows(h_ref[...], g_ref[...]).astype(BF16)
        xn_ref[...] = xn
        ab_ref[...] = _dot(xn, wab_ref[...])

    o_ref[...] = _dot(xn_ref[...], w_ref[...]).astype(o_ref.dtype)


def _in_proj(h, g, w, wab, l, tm, tn):
    s = h.shape[0]
    n = w.shape[2]
    return pl.pallas_call(
        _in_proj_kernel,
        grid=(s // tm, n // tn),
        in_specs=[
            pl.BlockSpec((tm, D_MODEL), lambda i, j: (i, 0)),
            pl.BlockSpec((1, D_MODEL), lambda i, j: (0, 0)),
            pl.BlockSpec((None, D_MODEL, tn), lambda i, j: (l, 0, j)),
            pl.BlockSpec((None, D_MODEL, AB_COLS), lambda i, j: (l, 0, 0)),
        ],
        out_specs=[
            pl.BlockSpec((tm, tn), lambda i, j: (i, j)),
            pl.BlockSpec((tm, AB_COLS), lambda i, j: (i, 0)),
        ],
        out_shape=[jax.ShapeDtypeStruct((s, n), BF16), jax.ShapeDtypeStruct((s, AB_COLS), F32)],
        scratch_shapes=[pltpu.VMEM((tm, D_MODEL), BF16)],
        compiler_params=_cparams(("parallel", "arbitrary")),
        name="in_proj",
    )(h, g, w, wab)


def _dn_kernel(q_ref, k_ref, v_ref, z_ref, ab_ref, cw_ref, alog_ref, dtb_ref, nrm_ref, cum_ref, eg_ref, eb_ref,
               o_ref, xbuf, qs, ks, vs, gce, gle, ble, gct, nb, qb, mb, qkb, rbv, rbk, ub, wb, qdb, kdt, state):
    i = pl.program_id(0)
    t = q_ref.shape[0]
    c = DN_BLOCK
    halo = SUBLANES

    @pl.when(i == 0)
    def _():
        xbuf[0:halo, :] = jnp.zeros((halo, xbuf.shape[1]), F32)
        state[...] = jnp.zeros(state.shape, F32)

    @pl.when(i > 0)
    def _():
        xbuf[0:halo, :] = xbuf[t:t + halo, :]

    xbuf[halo:halo + t, 0:DN_QK_WIDTH] = q_ref[...].astype(F32)
    xbuf[halo:halo + t, DN_QK_WIDTH:2 * DN_QK_WIDTH] = k_ref[...].astype(F32)
    xbuf[halo:halo + t, 2 * DN_QK_WIDTH:] = v_ref[...].astype(F32)

    n_slabs = (2 * DN_QK_WIDTH + DN_WIDTH) // LANES
    for s in range(n_slabs):
        cs = slice(s * LANES, (s + 1) * LANES)
        acc = cw_ref[0:1, cs] * xbuf[halo - 3:halo - 3 + t, cs]
        for tap in range(1, DN_CONV):
            acc = acc + cw_ref[tap:tap + 1, cs] * xbuf[halo - 3 + tap:halo - 3 + tap + t, cs]
        y = acc * _sigmoid(acc)
        if s < 2 * DN_HEADS:
            y = y * lax.rsqrt(jnp.sum(y * y, axis=-1, keepdims=True) + EPS)
        if s < DN_HEADS:
            qs[:, cs] = y * (DN_DK ** -0.5)
        elif s < 2 * DN_HEADS:
            ks[:, (s - DN_HEADS) * LANES:(s - DN_HEADS + 1) * LANES] = y
        else:
            vs[:, (s - 2 * DN_HEADS) * LANES:(s - 2 * DN_HEADS + 1) * LANES] = y

    ab = ab_ref[...]
    g = -jnp.exp(alog_ref[...]) * _softplus(ab + dtb_ref[...])
    beta = _sigmoid(ab)
    cums = _dot_01_lhs(cum_ref[...], g)
    gce[...] = _dot_01_rhs(cums[0:t], eg_ref[...])
    gle[...] = _dot_01_rhs(cums[t:2 * t], eg_ref[...])
    ble[...] = _dot_01_rhs(beta, eb_ref[...])
    gct[...] = cums[0:t].T

    row = lax.broadcasted_iota(jnp.int32, (c, c), 0)
    col = lax.broadcasted_iota(jnp.int32, (c, c), 1)
    causal = row >= col
    strict = row > col
    nrm = nrm_ref[...]
    items = [(ci, hd) for ci in range(t // c) for hd in range(DN_HEADS)]

    for it, (ci, hd) in enumerate(items):
        rows = slice(ci * c, (ci + 1) * c)
        cs = slice(hd * LANES, (hd + 1) * LANES)
        k_ = ks[rows, cs]
        gcc = gce[rows, cs]
        g_row = gct[hd:hd + 1, rows]
        dm = jnp.where(causal, gcc - g_row, 0.0)
        decay = jnp.where(causal, jnp.exp(dm), 0.0)
        lhs = jnp.concatenate([k_ * ble[rows, cs], qs[rows, cs]], axis=0).astype(BF16)
        kq = _dot_nt(lhs, k_.astype(BF16))
        nb[it] = jnp.where(strict, kq[0:c] * decay, 0.0)
        qkb[it] = (kq[c:] * decay).astype(BF16)

    for it, (ci, hd) in enumerate(items):
        rows = slice(ci * c, (ci + 1) * c)
        cs = slice(hd * LANES, (hd + 1) * LANES)
        q_ = qs[rows, cs]
        k_ = ks[rows, cs]
        gcc = gce[rows, cs]
        bet = ble[rows, cs]
        egc = jnp.exp(gcc)
        rbv[rows, cs] = vs[rows, cs] * bet
        rbk[rows, cs] = k_ * bet * egc
        qdb[rows, cs] = (q_ * egc).astype(BF16)
        kdt[it] = (k_ * jnp.exp(gle[rows, cs] - gcc)).T.astype(BF16)

    base = DN_SOLVE_BASE
    levels = base.bit_length() - 2
    for it in range(len(items)):
        nd = jnp.where(row // base == col // base, nb[it], 0.0)
        qb[it] = -nd
        if levels > 0:
            mb[it] = _dot_solve(nd, nd)
    for lvl in range(levels):
        for it in range(len(items)):
            qm = qb[it]
            m = mb[it]
            if lvl < levels - 1:
                prod = _dot_solve(jnp.concatenate([qm, m], axis=0), m)
                qb[it] = qm + m + prod[0:c]
                mb[it] = prod[c:]
            else:
                qb[it] = qm + m + _dot_solve(qm, m)
    size = base
    while size < c:
        join = (row // (2 * size) == col // (2 * size)) & (row // size != col // size)
        for it in range(len(items)):
            qm = qb[it]
            lo = jnp.where(join, nb[it], 0.0)
            a = lo + _dot_solve(qm, lo)
            qb[it] = qm - a - _dot_solve(a, qm)
        size *= 2

    for it, (ci, hd) in enumerate(items):
        rows = slice(ci * c, (ci + 1) * c)
        cs = slice(hd * LANES, (hd + 1) * LANES)
        rhs = jnp.concatenate([rbv[rows, cs], rbk[rows, cs]], axis=1)
        uw = rhs + _dot_solve(qb[it], rhs)
        ub[rows, cs] = uw[:, 0:DN_DV]
        wb[rows, cs] = uw[:, DN_DV:].astype(BF16)

    for it, (ci, hd) in enumerate(items):
        rows = slice(ci * c, (ci + 1) * c)
        cs = slice(hd * LANES, (hd + 1) * LANES)
        st = state[hd]
        ws = _dot(jnp.concatenate([wb[rows, cs], qdb[rows, cs]], axis=0), st.astype(BF16))
        vnb = (ub[rows, cs] - ws[0:c]).astype(BF16)
        kv = _dot(jnp.concatenate([qkb[it], kdt[it]], axis=0), vnb)
        o = ws[c:] + kv[0:c]
        state[hd] = st * jnp.exp(gle[rows, cs]) + kv[c:]
        zz = z_ref[rows, cs].astype(F32)
        y = _rms_rows(o, nrm) * (zz * _sigmoid(zz))
        o_ref[rows, cs] = y.astype(o_ref.dtype)


def _dn_constants(t):
    r = jnp.arange(t)
    same = (r[:, None] // DN_BLOCK) == (r[None, :] // DN_BLOCK)
    lower = r[:, None] >= r[None, :]
    cum = jnp.concatenate([(same & lower).astype(BF16), same.astype(BF16)], axis=0)
    lane = jnp.arange(LANES)[:, None]
    head = jnp.arange(DN_WIDTH)[None, :] // DN_DV
    eg = (lane == head).astype(BF16)
    eb = (lane == head + DN_HEADS).astype(BF16)
    return cum, eg, eb


def _delta_net(proj, ab, conv_w, a_log_pad, dt_bias_pad, dn_norm, t):
    s = proj.shape[0]
    cum, eg, eb = _dn_constants(t)
    cb = COL_DN_Q // DN_QK_WIDTH
    n_items = (t // DN_BLOCK) * DN_HEADS
    const = lambda shape: pl.BlockSpec(shape, lambda i: (0, 0))
    return pl.pallas_call(
        _dn_kernel,
        grid=(s // t,),
        in_specs=[
            pl.BlockSpec((t, DN_QK_WIDTH), lambda i: (i, cb)),
            pl.BlockSpec((t, DN_QK_WIDTH), lambda i: (i, cb + 1)),
            pl.BlockSpec((t, DN_WIDTH), lambda i: (i, cb + 2)),
            pl.BlockSpec((t, DN_WIDTH), lambda i: (i, cb + 3)),
            pl.BlockSpec((t, AB_COLS), lambda i: (i, 0)),
            const((DN_CONV, 2 * DN_QK_WIDTH + DN_WIDTH)),
            const((1, AB_COLS)),
            const((1, AB_COLS)),
            const((1, DN_DV)),
            const((2 * t, t)),
            const((LANES, DN_WIDTH)),
            const((LANES, DN_WIDTH)),
        ],
        out_specs=pl.BlockSpec((t, DN_WIDTH), lambda i: (i, 0)),
        out_shape=jax.ShapeDtypeStruct((s, DN_WIDTH), BF16),
        scratch_shapes=[
            pltpu.VMEM((t + SUBLANES, 2 * DN_QK_WIDTH + DN_WIDTH), F32),
            pltpu.VMEM((t, DN_QK_WIDTH), F32),
            pltpu.VMEM((t, DN_QK_WIDTH), F32),
            pltpu.VMEM((t, DN_WIDTH), F32),
            pltpu.VMEM((t, DN_WIDTH), F32),
            pltpu.VMEM((t, DN_WIDTH), F32),
            pltpu.VMEM((t, DN_WIDTH), F32),
            pltpu.VMEM((LANES, t), F32),
            pltpu.VMEM((n_items, DN_BLOCK, DN_BLOCK), F32),
            pltpu.VMEM((n_items, DN_BLOCK, DN_BLOCK), F32),
            pltpu.VMEM((n_items, DN_BLOCK, DN_BLOCK), F32),
            pltpu.VMEM((n_items, DN_BLOCK, DN_BLOCK), BF16),
            pltpu.VMEM((t, DN_WIDTH), F32),
            pltpu.VMEM((t, DN_QK_WIDTH), F32),
            pltpu.VMEM((t, DN_WIDTH), F32),
            pltpu.VMEM((t, DN_QK_WIDTH), BF16),
            pltpu.VMEM((t, DN_QK_WIDTH), BF16),
            pltpu.VMEM((n_items, DN_DK, DN_BLOCK), BF16),
            pltpu.VMEM((DN_HEADS, DN_DK, DN_DV), F32),
        ],
        compiler_params=_cparams(("arbitrary",)),
        name="delta_net",
    )(proj, proj, proj, proj, ab, conv_w, a_log_pad, dt_bias_pad, dn_norm, cum, eg, eb)


def _swa_kernel(sinks_ref, q_ref, kv_ref, kvp_ref, o_ref):
    i = pl.program_id(0)
    tq = q_ref.shape[0]
    w = WINDOW
    nwb = tq // w
    grp = SWA_HEADS // SWA_KV_HEADS
    ext = jnp.concatenate([kvp_ref[...], kv_ref[...]], axis=0)
    qi = lax.broadcasted_iota(jnp.int32, (w, 2 * w), 0)
    kj = lax.broadcasted_iota(jnp.int32, (w, 2 * w), 1)
    dist = qi + w - kj
    in_band = (dist >= 0) & (dist < w)
    distf = dist.astype(F32)
    for wb in range(nwb):
        blk = i * nwb + wb
        valid = in_band & (blk * w + kj - w >= 0)
        win = ext[wb * w:wb * w + 2 * w]
        qwb = q_ref[wb * w:(wb + 1) * w, :]
        outs = []
        for hk in range(SWA_KV_HEADS):
            kh = win[:, hk * SWA_DH:(hk + 1) * SWA_DH]
            vh = win[:, SWA_KV_WIDTH + hk * SWA_DH:SWA_KV_WIDTH + (hk + 1) * SWA_DH]
            for gi in range(grp):
                hd = hk * grp + gi
                slope = 2.0 ** (-8.0 * (hd + 1) / SWA_HEADS)
                qh = qwb[:, hd * SWA_DH:(hd + 1) * SWA_DH]
                sc = _dot_nt(qh, kh) * (SWA_DH ** -0.5) - slope * distf
                sc = jnp.where(valid, sc, NEG_BIG)
                sink = sinks_ref[hd]
                mx = jnp.maximum(jnp.max(sc, axis=-1, keepdims=True), sink)
                p = jnp.exp(sc - mx)
                den = jnp.sum(p, axis=-1, keepdims=True) + jnp.exp(sink - mx)
                outs.append(_dot(p.astype(BF16), vh) / den)
        o_ref[wb * w:(wb + 1) * w, :] = jnp.concatenate(outs, axis=1).astype(o_ref.dtype)


def _swa(proj, sinks, tq):
    s = proj.shape[0]
    nwb = tq // WINDOW
    qb = COL_SWA_Q // SWA_WIDTH
    kvb = COL_SWA_KV // (2 * SWA_KV_WIDTH)
    return pl.pallas_call(
        _swa_kernel,
        grid=(s // tq,),
        in_specs=[
            pl.BlockSpec(memory_space=pltpu.SMEM),
            pl.BlockSpec((tq, SWA_WIDTH), lambda i: (i, qb)),
            pl.BlockSpec((tq, 2 * SWA_KV_WIDTH), lambda i: (i, kvb)),
            pl.BlockSpec((WINDOW, 2 * SWA_KV_WIDTH), lambda i: (jnp.maximum(i * nwb - 1, 0), kvb)),
        ],
        out_specs=pl.BlockSpec((tq, SWA_WIDTH), lambda i: (i, 0)),
        out_shape=jax.ShapeDtypeStruct((s, SWA_WIDTH), BF16),
        compiler_params=_cparams(("parallel",)),
        name="swa",
    )(sinks, proj, proj, proj)


def _sb_knorm_kernel(k_ref, o_ref, run_ref):
    t = pl.program_id(0)

    @pl.when(t == 0)
    def _():
        for hd in range(SB_HEADS):
            run_ref[hd] = 0.0

    k = k_ref[...].astype(F32)
    for hd in range(SB_HEADS):
        kh = k[:, hd * SB_DH:(hd + 1) * SB_DH]
        cur = jnp.maximum(run_ref[hd], jnp.max(jnp.sum(kh * kh, axis=-1, keepdims=True)))
        run_ref[hd] = cur
        o_ref[t, hd] = cur


def _sb_key_norms(proj, tk):
    s = proj.shape[0]
    kb = COL_SB_Q // SB_WIDTH + 1
    return pl.pallas_call(
        _sb_knorm_kernel,
        grid=(s // tk,),
        in_specs=[pl.BlockSpec((tk, SB_WIDTH), lambda t: (t, kb))],
        out_specs=pl.BlockSpec(memory_space=pltpu.SMEM),
        out_shape=jax.ShapeDtypeStruct((s // tk, SB_HEADS), F32),
        scratch_shapes=[pltpu.SMEM((SB_HEADS,), F32)],
        compiler_params=_cparams(("arbitrary",)),
        name="sb_key_norms",
    )(proj)


def _sb_kernel(kmax_ref, q_ref, k_ref, v_ref, ntri_ref, o_ref, acc_ref, aft_ref):
    hd = pl.program_id(0)
    i = pl.program_id(1)
    tq = q_ref.shape[0]
    tk = tq
    blk = LANES
    nsub = tk // blk
    c2 = (SB_DH ** -0.5) * LOG2E
    q = q_ref[...]
    ntri = ntri_ref[...]
    acc_ref[...] = jnp.zeros(acc_ref.shape, F32)
    aft_ref[...] = jnp.zeros(aft_ref.shape, F32)
    qf = q.astype(F32)
    qn2 = jnp.sum(qf * qf, axis=-1, keepdims=True)

    def tile(k0, mask):
        kblk = k_ref[pl.ds(k0, tk), :]
        vblk = v_ref[pl.ds(k0, tk), :]
        z2 = _dot_nt(q, kblk) * c2
        sp2 = jnp.maximum(z2, 0.0) + jnp.log(1.0 + jnp.exp2(-jnp.abs(z2))) * LOG2E
        if mask is not None:
            sp2 = jnp.where(mask, sp2, 0.0)
        aft = aft_ref[...]
        revs = [None] * nsub
        for sb in reversed(range(nsub)):
            cum = _dot(sp2[:, sb * blk:(sb + 1) * blk].astype(BF16), ntri)
            revs[sb] = cum[:, 0:blk] + aft
            aft = aft + cum[:, blk:]
        a = jnp.exp2(z2 + jnp.concatenate(revs, axis=1))
        if mask is not None:
            a = jnp.where(mask, a, 0.0)
        acc_ref[...] += _dot(a.astype(BF16), vblk)
        aft_ref[...] = aft

    def live(kt):
        kt = jnp.maximum(kt, 0)
        bound = jnp.sqrt(qn2 * kmax_ref[kt, hd]) * c2 + aft_ref[:, 0:1]
        return jnp.max(bound) > -SB_SKIP_LOG2

    row = lax.broadcasted_iota(jnp.int32, (tq, tk), 0)
    col = lax.broadcasted_iota(jnp.int32, (tq, tk), 1)
    tile(pl.multiple_of(i * tq, tq), row > col)

    def cond(carry):
        n, go = carry
        return jnp.logical_and(n < i, go)

    def body(carry):
        n, _ = carry
        kt = i - 1 - n
        tile(pl.multiple_of(kt * tk, tk), None)
        return n + 1, live(kt - 1)

    lax.while_loop(cond, body, (jnp.int32(0), live(i - 1)))
    o_ref[...] = acc_ref[...].astype(o_ref.dtype)


def _stick_breaking(proj, tq):
    s = proj.shape[0]
    qb = COL_SB_Q // SB_DH
    r = jnp.arange(LANES)
    ntri = -jnp.concatenate([(r[:, None] >= r[None, :]).astype(BF16), jnp.ones((LANES, LANES), BF16)], axis=1)
    kmax = _sb_key_norms(proj, tq)
    return pl.pallas_call(
        _sb_kernel,
        grid=(SB_HEADS, s // tq),
        in_specs=[
            pl.BlockSpec(memory_space=pltpu.SMEM),
            pl.BlockSpec((tq, SB_DH), lambda h, i: (i, qb + h)),
            pl.BlockSpec((s, SB_DH), lambda h, i: (0, qb + SB_HEADS + h)),
            pl.BlockSpec((s, SB_DH), lambda h, i: (0, qb + 2 * SB_HEADS + h)),
            pl.BlockSpec((LANES, 2 * LANES), lambda h, i: (0, 0)),
        ],
        out_specs=pl.BlockSpec((tq, SB_DH), lambda h, i: (i, h)),
        out_shape=jax.ShapeDtypeStruct((s, SB_WIDTH), BF16),
        scratch_shapes=[pltpu.VMEM((tq, SB_DH), F32), pltpu.VMEM((tq, LANES), F32)],
        compiler_params=_cparams(("parallel", "parallel")),
        name="stick_breaking",
    )(kmax, proj, proj, proj, ntri)


def _merge_kernel(h_ref, g0_ref, g1_ref, g2_ref, ydn_ref, yswa_ref, ysb_ref, wdn_ref, wswa_ref, wsb_ref, wo_ref,
                  o_ref):
    merged = _sigmoid(g0_ref[...].astype(F32)) * _dot(ydn_ref[...], wdn_ref[...])
    merged = merged + _sigmoid(g1_ref[...].astype(F32)) * _dot(yswa_ref[...], wswa_ref[...])
    merged = merged + _sigmoid(g2_ref[...].astype(F32)) * _dot(ysb_ref[...], wsb_ref[...])
    o_ref[...] = h_ref[...] + _dot(merged.astype(BF16), wo_ref[...])


def _merge(h, proj, y_dn, y_swa, y_sb, w_dn, w_swa, w_sb, w_o, l, tm):
    s = h.shape[0]
    rows = lambda width, col=0: pl.BlockSpec((tm, width), lambda i: (i, col))
    const = lambda a: pl.BlockSpec((None,) + a.shape[1:], lambda i: (l, 0, 0), pipeline_mode=pl.Buffered(1))
    return pl.pallas_call(
        _merge_kernel,
        grid=(s // tm,),
        in_specs=[
            rows(D_MODEL), rows(D_MODEL, 0), rows(D_MODEL, 1), rows(D_MODEL, 2),
            rows(DN_WIDTH), rows(SWA_WIDTH), rows(SB_WIDTH),
            const(w_dn), const(w_swa), const(w_sb), const(w_o),
        ],
        out_specs=rows(D_MODEL),
        out_shape=jax.ShapeDtypeStruct((s, D_MODEL), F32),
        compiler_params=_cparams(("parallel",)),
        name="merge",
    )(h, proj, proj, proj, y_dn, y_swa, y_sb, w_dn, w_swa, w_sb, w_o)


def _norm_matmul_kernel(x_ref, g_ref, w_ref, o_ref):
    xn = _rms_rows(x_ref[...], g_ref[...]).astype(BF16)
    o_ref[...] = _dot(xn, w_ref[...]).astype(o_ref.dtype)


def _mem_kv(mem, g, w, l):
    m = mem.shape[0]
    n = w.shape[2]
    return pl.pallas_call(
        _norm_matmul_kernel,
        grid=(1,),
        in_specs=[
            pl.BlockSpec((m, D_MODEL), lambda i: (0, 0)),
            pl.BlockSpec((1, D_MODEL), lambda i: (0, 0)),
            pl.BlockSpec((None, D_MODEL, n), lambda i: (l, 0, 0)),
        ],
        out_specs=pl.BlockSpec((m, n), lambda i: (0, 0)),
        out_shape=jax.ShapeDtypeStruct((m, n), BF16),
        compiler_params=_cparams(("arbitrary",)),
        name="mem_kv",
    )(mem, g, w)


def _xattn_kernel(h_ref, g_ref, kv_ref, wq_ref, wo_ref, o_ref):
    h = h_ref[...]
    hn = _rms_rows(h, g_ref[...]).astype(BF16)
    q = _dot(hn, wq_ref[...]).astype(BF16)
    outs = []
    for hd in range(X_HEADS):
        cs = slice(hd * X_DH, (hd + 1) * X_DH)
        kh = kv_ref[:, cs]
        vh = kv_ref[:, X_WIDTH + hd * X_DH:X_WIDTH + (hd + 1) * X_DH]
        sc = _dot_nt(q[:, cs], kh) * (X_DH ** -0.5)
        mx = jnp.max(sc, axis=-1, keepdims=True)
        p = jnp.exp(sc - mx)
        den = jnp.sum(p, axis=-1, keepdims=True)
        outs.append((_dot(p.astype(BF16), vh) / den).astype(BF16))
    o = jnp.concatenate(outs, axis=1)
    o_ref[...] = h + _dot(o, wo_ref[...])


def _xattn(h, g, kv, w_q, w_o, l, tm):
    s = h.shape[0]
    const = lambda a: pl.BlockSpec(a.shape, lambda i: (0, 0))
    layer = lambda a: pl.BlockSpec((None,) + a.shape[1:], lambda i: (l, 0, 0))
    return pl.pallas_call(
        _xattn_kernel,
        grid=(s // tm,),
        in_specs=[pl.BlockSpec((tm, D_MODEL), lambda i: (i, 0)), const(g), const(kv), layer(w_q), layer(w_o)],
        out_specs=pl.BlockSpec((tm, D_MODEL), lambda i: (i, 0)),
        out_shape=jax.ShapeDtypeStruct((s, D_MODEL), F32),
        compiler_params=_cparams(("parallel",)),
        name="xattn",
    )(h, g, kv, w_q, w_o)


def _ffn_kernel(h_ref, g_ref, wg_ref, wv_ref, cg_ref, cv_ref, wd_ref, o_ref, hn_ref, acc_ref, gbuf, vbuf, carry):
    i = pl.program_id(0)
    j = pl.program_id(1)
    tm = h_ref.shape[0]
    halo = SUBLANES

    @pl.when(j == 0)
    def _():
        hn_ref[...] = _rms_rows(h_ref[...], g_ref[...]).astype(BF16)
        acc_ref[...] = jnp.zeros(acc_ref.shape, F32)

    @pl.when(i == 0)
    def _():
        carry[j] = jnp.zeros(carry.shape[1:], F32)

    hn = hn_ref[...]
    gbuf[0:halo, :] = carry[j, 0]
    vbuf[0:halo, :] = carry[j, 1]
    gbuf[halo:, :] = _dot(hn, wg_ref[...])
    vbuf[halo:, :] = _dot(hn, wv_ref[...])
    carry[j, 0] = gbuf[tm:tm + halo, :]
    carry[j, 1] = vbuf[tm:tm + halo, :]

    def conv(buf, cw_ref):
        base = halo - (FFN_CONV - 1)
        out = cw_ref[0:1, :] * buf[base:base + tm, :]
        for tap in range(1, FFN_CONV):
            out = out + cw_ref[tap:tap + 1, :] * buf[base + tap:base + tap + tm, :]
        return out

    gate = conv(gbuf, cg_ref)
    val = conv(vbuf, cv_ref)
    act = (gate * _sigmoid(gate) * val).astype(BF16)
    acc_ref[...] += _dot(act, wd_ref[...])

    @pl.when(j == pl.num_programs(1) - 1)
    def _():
        o_ref[...] = h_ref[...] + acc_ref[...]


def _ffn(h, g, w_up, conv_w, w_down, l, tm, tn):
    s = h.shape[0]
    nj = D_FF // tn
    return pl.pallas_call(
        _ffn_kernel,
        grid=(s // tm, nj),
        in_specs=[
            pl.BlockSpec((tm, D_MODEL), lambda i, j: (i, 0)),
            pl.BlockSpec((1, D_MODEL), lambda i, j: (0, 0)),
            pl.BlockSpec((None, D_MODEL, tn), lambda i, j: (l, 0, j)),
            pl.BlockSpec((None, D_MODEL, tn), lambda i, j: (l, 0, j + nj)),
            pl.BlockSpec((FFN_CONV, tn), lambda i, j: (0, j)),
            pl.BlockSpec((FFN_CONV, tn), lambda i, j: (0, j + nj)),
            pl.BlockSpec((None, tn, D_MODEL), lambda i, j: (l, j, 0)),
        ],
        out_specs=pl.BlockSpec((tm, D_MODEL), lambda i, j: (i, 0)),
        out_shape=jax.ShapeDtypeStruct((s, D_MODEL), F32),
        scratch_shapes=[
            pltpu.VMEM((tm, D_MODEL), BF16),
            pltpu.VMEM((tm, D_MODEL), F32),
            pltpu.VMEM((tm + SUBLANES, tn), F32),
            pltpu.VMEM((tm + SUBLANES, tn), F32),
            pltpu.VMEM((nj, 2, SUBLANES, tn), F32),
        ],
        compiler_params=_cparams(("arbitrary", "arbitrary")),
        name="conv_ffn",
    )(h, g, w_up, w_up, conv_w, conv_w, w_down)


def _final_norm_kernel(h_ref, g_ref, o_ref):
    o_ref[...] = _rms_rows(h_ref[...], g_ref[...])


def _final_norm(h, g, tm):
    s = h.shape[0]
    return pl.pallas_call(
        _final_norm_kernel,
        grid=(s // tm,),
        in_specs=[pl.BlockSpec((tm, D_MODEL), lambda i: (i, 0)), pl.BlockSpec((1, D_MODEL), lambda i: (0, 0))],
        out_specs=pl.BlockSpec((tm, D_MODEL), lambda i: (i, 0)),
        out_shape=jax.ShapeDtypeStruct((s, D_MODEL), F32),
        compiler_params=_cparams(("parallel",)),
        name="final_norm",
    )(h, g)


def _tile(s, want):
    return min(s, want)


def _reorder_w_in(w_in):
    o_z = 2 * DN_QK_WIDTH + DN_WIDTH
    o_a = o_z + DN_WIDTH
    o_swq = o_a + 2 * DN_HEADS
    o_swkv = o_swq + SWA_WIDTH
    o_sb = o_swkv + 2 * SWA_KV_WIDTH
    o_gate = o_sb + 3 * SB_WIDTH
    big = jnp.concatenate(
        [w_in[..., o_gate:], w_in[..., 0:o_a], w_in[..., o_swq:o_swkv], w_in[..., o_sb:o_gate],
         w_in[..., o_swkv:o_sb]], axis=-1).astype(BF16)
    ab = w_in[..., o_a:o_swq]
    ab = jnp.pad(ab, ((0, 0), (0, 0), (0, AB_COLS - ab.shape[-1]))).astype(BF16)
    return big, ab


def kernel(x, mem, norm_mix, w_in, dn_conv, dn_a_log, dn_dt_bias, dn_norm, swa_sinks, w_br_dn, w_br_swa, w_br_sb, w_o, norm_xattn, norm_mem, w_xq, w_xkv, w_xo, norm_ffn, w_up, ffn_conv, w_down, norm_final):
    depth = w_in.shape[0]
    s = x.shape[1]
    assert x.shape[0] == 1 and s % 256 == 0
    h = x[0]
    mem2 = mem[0]

    w_big, w_ab = _reorder_w_in(w_in)
    w_br_dn_b = w_br_dn.astype(BF16)
    w_br_swa_b = w_br_swa.astype(BF16)
    w_br_sb_b = w_br_sb.astype(BF16)
    w_o_b = w_o.astype(BF16)
    w_xq_b = w_xq.astype(BF16)
    w_xkv_b = w_xkv.astype(BF16)
    w_xo_b = w_xo.astype(BF16)
    w_up_b = w_up.astype(BF16)
    w_down_b = w_down.astype(BF16)
    pad8 = lambda a: jnp.pad(a, ((0, 0), (0, AB_COLS - a.shape[-1])))
    a_log_pad = pad8(dn_a_log)
    dt_bias_pad = pad8(dn_dt_bias)

    for l in range(depth):
        proj, ab = _in_proj(h, norm_mix[l][None], w_big, w_ab, l, tm=_tile(s, 512), tn=1792)
        y_dn = _delta_net(proj, ab, dn_conv[l], a_log_pad[l][None], dt_bias_pad[l][None], dn_norm[l][None],
                          t=_tile(s, 256))
        y_swa = _swa(proj, swa_sinks[l], tq=_tile(s, 512))
        y_sb = _stick_breaking(proj, tq=_tile(s, 256))
        h = _merge(h, proj, y_dn, y_swa, y_sb, w_br_dn_b, w_br_swa_b, w_br_sb_b, w_o_b, l, tm=_tile(s, 256))
        kv = _mem_kv(mem2, norm_mem[l][None], w_xkv_b, l)
        h = _xattn(h, norm_xattn[l][None], kv, w_xq_b, w_xo_b, l, tm=_tile(s, 512))
        h = _ffn(h, norm_ffn[l][None], w_up_b, ffn_conv[l], w_down_b, l, tm=_tile(s, 512), tn=512)
    return _final_norm(h, norm_final[None], tm=_tile(s, 1024))[None]
```

```python
import functools

import jax
import jax.numpy as jnp
from jax import lax
from jax.experimental import pallas as pl
from jax.experimental.pallas import tpu as pltpu

F32 = jnp.float32
BF16 = jnp.bfloat16

D_MODEL = 2048
EPS = 1e-6
DN_HEADS = 8
DN_DK = 128
DN_DV = 128
DN_CONV = 4
DN_BLOCK = 128
DN_SOLVE_BASE = 2
SWA_HEADS = 8
SWA_KV_HEADS = 2
SWA_DH = 64
WINDOW = 128
SB_HEADS = 4
SB_DH = 128
X_HEADS = 4
X_DH = 128
D_FF = 4096
FFN_CONV = 3
N_BRANCH = 3

DN_QK_WIDTH = DN_HEADS * DN_DK
DN_WIDTH = DN_HEADS * DN_DV
SWA_WIDTH = SWA_HEADS * SWA_DH
SWA_KV_WIDTH = SWA_KV_HEADS * SWA_DH
SB_WIDTH = SB_HEADS * SB_DH
X_WIDTH = X_HEADS * X_DH

LANES = 128
SUBLANES = 8
VMEM_LIMIT = 56 * 1024 * 1024

PROJ_COLS = N_BRANCH * D_MODEL + 3 * DN_QK_WIDTH + DN_WIDTH + SWA_WIDTH + 3 * SB_WIDTH + 2 * SWA_KV_WIDTH
COL_DN_Q = N_BRANCH * D_MODEL
COL_DN_Z = COL_DN_Q + 3 * DN_QK_WIDTH
COL_SWA_Q = COL_DN_Z + DN_WIDTH
COL_SB_Q = COL_SWA_Q + SWA_WIDTH
COL_SWA_KV = COL_SB_Q + 3 * SB_WIDTH
AB_COLS = LANES

NEG_BIG = -1e30
LOG2E = 1.4426950408889634
SB_SKIP_LOG2 = 170.0


def _cparams(semantics):
    return pltpu.CompilerParams(dimension_semantics=semantics, vmem_limit_bytes=VMEM_LIMIT)


def _dot(a, b):
    return jnp.dot(a, b, preferred_element_type=F32)


def _dot_nt(a, b, precision=None):
    return lax.dot_general(a, b, (((1,), (1,)), ((), ())), preferred_element_type=F32, precision=precision)


def _split3(x):
    x1 = x.astype(BF16)
    r1 = x - x1.astype(F32)
    x2 = r1.astype(BF16)
    x3 = (r1 - x2.astype(F32)).astype(BF16)
    return x1, x2, x3


def _dot_01_lhs(m01, x):
    x1, x2, x3 = _split3(x)
    return _dot(m01, x1) + _dot(m01, x2) + _dot(m01, x3)


def _dot_01_rhs(x, m01):
    x1, x2, x3 = _split3(x)
    return _dot(x1, m01) + _dot(x2, m01) + _dot(x3, m01)


def _dot_solve(a, b):
    return jnp.dot(a.astype(BF16), b.astype(BF16), preferred_element_type=F32)


def _sigmoid(x):
    return 1.0 / (1.0 + jnp.exp(-x))


def _softplus(x):
    return jnp.maximum(x, 0.0) + jnp.log1p(jnp.exp(-jnp.abs(x)))


def _rms_rows(x, g):
    ms = jnp.mean(x * x, axis=-1, keepdims=True)
    return x * lax.rsqrt(ms + EPS) * g


def _in_proj_kernel(h_ref, g_ref, w_ref, wab_ref, o_ref, ab_ref, xn_ref):
    @pl.when(pl.program_id(1) == 0)
    def _():
        xn = _rms_rows(h_ref[...], g_ref[...]).astype(BF16)
        xn_ref[...] = xn
        ab_ref[...] = _dot(xn, wab_ref[...])

    o_ref[...] = _dot(xn_ref[...], w_ref[...]).astype(o_ref.dtype)


def _in_proj(h, g, w, wab, l, tm, tn):
    s = h.shape[0]
    n = w.shape[2]
    return pl.pallas_call(
        _in_proj_kernel,
        grid=(s // tm, n // tn),
        in_specs=[
            pl.BlockSpec((tm, D_MODEL), lambda i, j: (i, 0)),
            pl.BlockSpec((1, D_MODEL), lambda i, j: (0, 0)),
            pl.BlockSpec((None, D_MODEL, tn), lambda i, j: (l, 0, j)),
            pl.BlockSpec((None, D_MODEL, AB_COLS), lambda i, j: (l, 0, 0)),
        ],
        out_specs=[
            pl.BlockSpec((tm, tn), lambda i, j: (i, j)),
            pl.BlockSpec((tm, AB_COLS), lambda i, j: (i, 0)),
        ],
        out_shape=[jax.ShapeDtypeStruct((s, n), BF16), jax.ShapeDtypeStruct((s, AB_COLS), F32)],
        scratch_shapes=[pltpu.VMEM((tm, D_MODEL), BF16)],
        compiler_params=_cparams(("parallel", "arbitrary")),
        name="in_proj",
    )(h, g, w, wab)


def _dn_kernel(q_ref, k_ref, v_ref, z_ref, ab_ref, cw_ref, alog_ref, dtb_ref, nrm_ref, cum_ref, eg_ref, eb_ref,
               o_ref, xbuf, qs, ks, vs, gce, gle, ble, gct, nb, qb, mb, qkb, rbv, rbk, ub, wb, qdb, kdt, state):
    i = pl.program_id(0)
    t = q_ref.shape[0]
    c = DN_BLOCK
    halo = SUBLANES

    @pl.when(i == 0)
    def _():
        xbuf[0:halo, :] = jnp.zeros((halo, xbuf.shape[1]), F32)
        state[...] = jnp.zeros(state.shape, F32)

    @pl.when(i > 0)
    def _():
        xbuf[0:halo, :] = xbuf[t:t + halo, :]

    xbuf[halo:halo + t, 0:DN_QK_WIDTH] = q_ref[...].astype(F32)
    xbuf[halo:halo + t, DN_QK_WIDTH:2 * DN_QK_WIDTH] = k_ref[...].astype(F32)
    xbuf[halo:halo + t, 2 * DN_QK_WIDTH:] = v_ref[...].astype(F32)

    n_slabs = (2 * DN_QK_WIDTH + DN_WIDTH) // LANES
    for s in range(n_slabs):
        cs = slice(s * LANES, (s + 1) * LANES)
        acc = cw_ref[0:1, cs] * xbuf[halo - 3:halo - 3 + t, cs]
        for tap in range(1, DN_CONV):
            acc = acc + cw_ref[tap:tap + 1, cs] * xbuf[halo - 3 + tap:halo - 3 + tap + t, cs]
        y = acc * _sigmoid(acc)
        if s < 2 * DN_HEADS:
            y = y * lax.rsqrt(jnp.sum(y * y, axis=-1, keepdims=True) + EPS)
        if s < DN_HEADS:
            qs[:, cs] = y * (DN_DK ** -0.5)
        elif s < 2 * DN_HEADS:
            ks[:, (s - DN_HEADS) * LANES:(s - DN_HEADS + 1) * LANES] = y
        else:
            vs[:, (s - 2 * DN_HEADS) * LANES:(s - 2 * DN_HEADS + 1) * LANES] = y

    ab = ab_ref[...]
    g = -jnp.exp(alog_ref[...]) * _softplus(ab + dtb_ref[...])
    beta = _sigmoid(ab)
    cums = _dot_01_lhs(cum_ref[...], g)
    gce[...] = _dot_01_rhs(cums[0:t], eg_ref[...])
    gle[...] = _dot_01_rhs(cums[t:2 * t], eg_ref[...])
    ble[...] = _dot_01_rhs(beta, eb_ref[...])
    gct[...] = cums[0:t].T

    row = lax.broadcasted_iota(jnp.int32, (c, c), 0)
    col = lax.broadcasted_iota(jnp.int32, (c, c), 1)
    causal = row >= col
    strict = row > col
    nrm = nrm_ref[...]
    items = [(ci, hd) for ci in range(t // c) for hd in range(DN_HEADS)]

    for it, (ci, hd) in enumerate(items):
        rows = slice(ci * c, (ci + 1) * c)
        cs = slice(hd * LANES, (hd + 1) * LANES)
        k_ = ks[rows, cs]
        gcc = gce[rows, cs]
        g_row = gct[hd:hd + 1, rows]
        dm = jnp.where(causal, gcc - g_row, 0.0)
        decay = jnp.where(causal, jnp.exp(dm), 0.0)
        lhs = jnp.concatenate([k_ * ble[rows, cs], qs[rows, cs]], axis=0).astype(BF16)
        kq = _dot_nt(lhs, k_.astype(BF16))
        nb[it] = jnp.where(strict, kq[0:c] * decay, 0.0)
        qkb[it] = (kq[c:] * decay).astype(BF16)

    for it, (ci, hd) in enumerate(items):
        rows = slice(ci * c, (ci + 1) * c)
        cs = slice(hd * LANES, (hd + 1) * LANES)
        q_ = qs[rows, cs]
        k_ = ks[rows, cs]
        gcc = gce[rows, cs]
        bet = ble[rows, cs]
        egc = jnp.exp(gcc)
        rbv[rows, cs] = vs[rows, cs] * bet
        rbk[rows, cs] = k_ * bet * egc
        qdb[rows, cs] = (q_ * egc).astype(BF16)
        kdt[it] = (k_ * jnp.exp(gle[rows, cs] - gcc)).T.astype(BF16)

    base = DN_SOLVE_BASE
    levels = base.bit_length() - 2
    for it in range(len(items)):
        nd = jnp.where(row // base == col // base, nb[it], 0.0)
        qb[it] = -nd
        if levels > 0:
            mb[it] = _dot_solve(nd, nd)
    for lvl in range(levels):
        for it in range(len(items)):
            qm = qb[it]
            m = mb[it]
            if lvl < levels - 1:
                prod = _dot_solve(jnp.concatenate([qm, m], axis=0), m)
                qb[it] = qm + m + prod[0:c]
                mb[it] = prod[c:]
            else:
                qb[it] = qm + m + _dot_solve(qm, m)
    size = base
    while size < c:
        join = (row // (2 * size) == col // (2 * size)) & (row // size != col // size)
        for it in range(len(items)):
            qm = qb[it]
            lo = jnp.where(join, nb[it], 0.0)
            a = lo + _dot_solve(qm, lo)
            qb[it] = qm - a - _dot_solve(a, qm)
        size *= 2

    for it, (ci, hd) in enumerate(items):
        rows = slice(ci * c, (ci + 1) * c)
        cs = slice(hd * LANES, (hd + 1) * LANES)
        rhs = jnp.concatenate([rbv[rows, cs], rbk[rows, cs]], axis=1)
        uw = rhs + _dot_solve(qb[it], rhs)
        ub[rows, cs] = uw[:, 0:DN_DV]
        wb[rows, cs] = uw[:, DN_DV:].astype(BF16)

    for it, (ci, hd) in enumerate(items):
        rows = slice(ci * c, (ci + 1) * c)
        cs = slice(hd * LANES, (hd + 1) * LANES)
        st = state[hd]
        ws = _dot(jnp.concatenate([wb[rows, cs], qdb[rows, cs]], axis=0), st.astype(BF16))
        vnb = (ub[rows, cs] - ws[0:c]).astype(BF16)
        kv = _dot(jnp.concatenate([qkb[it], kdt[it]], axis=0), vnb)
        o = ws[c:] + kv[0:c]
        state[hd] = st * jnp.exp(gle[rows, cs]) + kv[c:]
        zz = z_ref[rows, cs].astype(F32)
        y = _rms_rows(o, nrm) * (zz * _sigmoid(zz))
        o_ref[rows, cs] = y.astype(o_ref.dtype)


def _dn_constants(t):
    r = jnp.arange(t)
    same = (r[:, None] // DN_BLOCK) == (r[None, :] // DN_BLOCK)
    lower = r[:, None] >= r[None, :]
    cum = jnp.concatenate([(same & lower).astype(BF16), same.astype(BF16)], axis=0)
    lane = jnp.arange(LANES)[:, None]
    head = jnp.arange(DN_WIDTH)[None, :] // DN_DV
    eg = (lane == head).astype(BF16)
    eb = (lane == head + DN_HEADS).astype(BF16)
    return cum, eg, eb


def _delta_net(proj, ab, conv_w, a_log_pad, dt_bias_pad, dn_norm, t):
    s = proj.shape[0]
    cum, eg, eb = _dn_constants(t)
    cb = COL_DN_Q // DN_QK_WIDTH
    n_items = (t // DN_BLOCK) * DN_HEADS
    const = lambda shape: pl.BlockSpec(shape, lambda i: (0, 0))
    return pl.pallas_call(
        _dn_kernel,
        grid=(s // t,),
        in_specs=[
            pl.BlockSpec((t, DN_QK_WIDTH), lambda i: (i, cb)),
            pl.BlockSpec((t, DN_QK_WIDTH), lambda i: (i, cb + 1)),
            pl.BlockSpec((t, DN_WIDTH), lambda i: (i, cb + 2)),
            pl.BlockSpec((t, DN_WIDTH), lambda i: (i, cb + 3)),
            pl.BlockSpec((t, AB_COLS), lambda i: (i, 0)),
            const((DN_CONV, 2 * DN_QK_WIDTH + DN_WIDTH)),
            const((1, AB_COLS)),
            const((1, AB_COLS)),
            const((1, DN_DV)),
            const((2 * t, t)),
            const((LANES, DN_WIDTH)),
            const((LANES, DN_WIDTH)),
        ],
        out_specs=pl.BlockSpec((t, DN_WIDTH), lambda i: (i, 0)),
        out_shape=jax.ShapeDtypeStruct((s, DN_WIDTH), BF16),
        scratch_shapes=[
            pltpu.VMEM((t + SUBLANES, 2 * DN_QK_WIDTH + DN_WIDTH), F32),
            pltpu.VMEM((t, DN_QK_WIDTH), F32),
            pltpu.VMEM((t, DN_QK_WIDTH), F32),
            pltpu.VMEM((t, DN_WIDTH), F32),
            pltpu.VMEM((t, DN_WIDTH), F32),
            pltpu.VMEM((t, DN_WIDTH), F32),
            pltpu.VMEM((t, DN_WIDTH), F32),
            pltpu.VMEM((LANES, t), F32),
            pltpu.VMEM((n_items, DN_BLOCK, DN_BLOCK), F32),
            pltpu.VMEM((n_items, DN_BLOCK, DN_BLOCK), F32),
            pltpu.VMEM((n_items, DN_BLOCK, DN_BLOCK), F32),
            pltpu.VMEM((n_items, DN_BLOCK, DN_BLOCK), BF16),
            pltpu.VMEM((t, DN_WIDTH), F32),
            pltpu.VMEM((t, DN_QK_WIDTH), F32),
            pltpu.VMEM((t, DN_WIDTH), F32),
            pltpu.VMEM((t, DN_QK_WIDTH), BF16),
            pltpu.VMEM((t, DN_QK_WIDTH), BF16),
            pltpu.VMEM((n_items, DN_DK, DN_BLOCK), BF16),
            pltpu.VMEM((DN_HEADS, DN_DK, DN_DV), F32),
        ],
        compiler_params=_cparams(("arbitrary",)),
        name="delta_net",
    )(proj, proj, proj, proj, ab, conv_w, a_log_pad, dt_bias_pad, dn_norm, cum, eg, eb)


def _swa_kernel(sinks_ref, q_ref, kv_ref, kvp_ref, o_ref):
    i = pl.program_id(0)
    tq = q_ref.shape[0]
    w = WINDOW
    nwb = tq // w
    grp = SWA_HEADS // SWA_KV_HEADS
    ext = jnp.concatenate([kvp_ref[...], kv_ref[...]], axis=0)
    qi = lax.broadcasted_iota(jnp.int32, (w, 2 * w), 0)
    kj = lax.broadcasted_iota(jnp.int32, (w, 2 * w), 1)
    dist = qi + w - kj
    in_band = (dist >= 0) & (dist < w)
    distf = dist.astype(F32)
    for wb in range(nwb):
        blk = i * nwb + wb
        valid = in_band & (blk * w + kj - w >= 0)
        win = ext[wb * w:wb * w + 2 * w]
        qwb = q_ref[wb * w:(wb + 1) * w, :]
        outs = []
        for hk in range(SWA_KV_HEADS):
            kh = win[:, hk * SWA_DH:(hk + 1) * SWA_DH]
            vh = win[:, SWA_KV_WIDTH + hk * SWA_DH:SWA_KV_WIDTH + (hk + 1) * SWA_DH]
            for gi in range(grp):
                hd = hk * grp + gi
                slope = 2.0 ** (-8.0 * (hd + 1) / SWA_HEADS)
                qh = qwb[:, hd * SWA_DH:(hd + 1) * SWA_DH]
                sc = _dot_nt(qh, kh) * (SWA_DH ** -0.5) - slope * distf
                sc = jnp.where(valid, sc, NEG_BIG)
                sink = sinks_ref[hd]
                mx = jnp.maximum(jnp.max(sc, axis=-1, keepdims=True), sink)
                p = jnp.exp(sc - mx)
                den = jnp.sum(p, axis=-1, keepdims=True) + jnp.exp(sink - mx)
                outs.append(_dot(p.astype(BF16), vh) / den)
        o_ref[wb * w:(wb + 1) * w, :] = jnp.concatenate(outs, axis=1).astype(o_ref.dtype)


def _swa(proj, sinks, tq):
    s = proj.shape[0]
    nwb = tq // WINDOW
    qb = COL_SWA_Q // SWA_WIDTH
    kvb = COL_SWA_KV // (2 * SWA_KV_WIDTH)
    return pl.pallas_call(
        _swa_kernel,
        grid=(s // tq,),
        in_specs=[
            pl.BlockSpec(memory_space=pltpu.SMEM),
            pl.BlockSpec((tq, SWA_WIDTH), lambda i: (i, qb)),
            pl.BlockSpec((tq, 2 * SWA_KV_WIDTH), lambda i: (i, kvb)),
            pl.BlockSpec((WINDOW, 2 * SWA_KV_WIDTH), lambda i: (jnp.maximum(i * nwb - 1, 0), kvb)),
        ],
        out_specs=pl.BlockSpec((tq, SWA_WIDTH), lambda i: (i, 0)),
        out_shape=jax.ShapeDtypeStruct((s, SWA_WIDTH), BF16),
        compiler_params=_cparams(("parallel",)),
        name="swa",
    )(sinks, proj, proj, proj)


def _sb_knorm_kernel(k_ref, o_ref, run_ref):
    t = pl.program_id(0)

    @pl.when(t == 0)
    def _():
        for hd in range(SB_HEADS):
            run_ref[hd] = 0.0

    k = k_ref[...].astype(F32)
    for hd in range(SB_HEADS):
        kh = k[:, hd * SB_DH:(hd + 1) * SB_DH]
        cur = jnp.maximum(run_ref[hd], jnp.max(jnp.sum(kh * kh, axis=-1, keepdims=True)))
        run_ref[hd] = cur
        o_ref[t, hd] = cur


def _sb_key_norms(proj, tk):
    s = proj.shape[0]
    kb = COL_SB_Q // SB_WIDTH + 1
    return pl.pallas_call(
        _sb_knorm_kernel,
        grid=(s // tk,),
        in_specs=[pl.BlockSpec((tk, SB_WIDTH), lambda t: (t, kb))],
        out_specs=pl.BlockSpec(memory_space=pltpu.SMEM),
        out_shape=jax.ShapeDtypeStruct((s // tk, SB_HEADS), F32),
        scratch_shapes=[pltpu.SMEM((SB_HEADS,), F32)],
        compiler_params=_cparams(("arbitrary",)),
        name="sb_key_norms",
    )(proj)


def _sb_kernel(kmax_ref, q_ref, k_ref, v_ref, ntri_ref, o_ref, acc_ref, aft_ref):
    hd = pl.program_id(0)
    i = pl.program_id(1)
    tq = q_ref.shape[0]
    tk = tq
    blk = LANES
    nsub = tk // blk
    c2 = (SB_DH ** -0.5) * LOG2E
    q = q_ref[...]
    ntri = ntri_ref[...]
    acc_ref[...] = jnp.zeros(acc_ref.shape, F32)
    aft_ref[...] = jnp.zeros(aft_ref.shape, F32)
    qf = q.astype(F32)
    qn2 = jnp.sum(qf * qf, axis=-1, keepdims=True)

    def tile(k0, mask):
        kblk = k_ref[pl.ds(k0, tk), :]
        vblk = v_ref[pl.ds(k0, tk), :]
        z2 = _dot_nt(q, kblk) * c2
        sp2 = jnp.maximum(z2, 0.0) + jnp.log(1.0 + jnp.exp2(-jnp.abs(z2))) * LOG2E
        if mask is not None:
            sp2 = jnp.where(mask, sp2, 0.0)
        aft = aft_ref[...]
        revs = [None] * nsub
        for sb in reversed(range(nsub)):
            cum = _dot(sp2[:, sb * blk:(sb + 1) * blk].astype(BF16), ntri)
            revs[sb] = cum[:, 0:blk] + aft
            aft = aft + cum[:, blk:]
        a = jnp.exp2(z2 + jnp.concatenate(revs, axis=1))
        if mask is not None:
            a = jnp.where(mask, a, 0.0)
        acc_ref[...] += _dot(a.astype(BF16), vblk)
        aft_ref[...] = aft

    def live(kt):
        kt = jnp.maximum(kt, 0)
        bound = jnp.sqrt(qn2 * kmax_ref[kt, hd]) * c2 + aft_ref[:, 0:1]
        return jnp.max(bound) > -SB_SKIP_LOG2

    row = lax.broadcasted_iota(jnp.int32, (tq, tk), 0)
    col = lax.broadcasted_iota(jnp.int32, (tq, tk), 1)
    tile(pl.multiple_of(i * tq, tq), row > col)

    def cond(carry):
        n, go = carry
        return jnp.logical_and(n < i, go)

    def body(carry):
        n, _ = carry
        kt = i - 1 - n
        tile(pl.multiple_of(kt * tk, tk), None)
        return n + 1, live(kt - 1)

    lax.while_loop(cond, body, (jnp.int32(0), live(i - 1)))
    o_ref[...] = acc_ref[...].astype(o_ref.dtype)


def _stick_breaking(proj, tq):
    s = proj.shape[0]
    qb = COL_SB_Q // SB_DH
    r = jnp.arange(LANES)
    ntri = -jnp.concatenate([(r[:, None] >= r[None, :]).astype(BF16), jnp.ones((LANES, LANES), BF16)], axis=1)
    kmax = _sb_key_norms(proj, tq)
    return pl.pallas_call(
        _sb_kernel,
        grid=(SB_HEADS, s // tq),
        in_specs=[
            pl.BlockSpec(memory_space=pltpu.SMEM),
            pl.BlockSpec((tq, SB_DH), lambda h, i: (i, qb + h)),
            pl.BlockSpec((s, SB_DH), lambda h, i: (0, qb + SB_HEADS + h)),
            pl.BlockSpec((s, SB_DH), lambda h, i: (0, qb + 2 * SB_HEADS + h)),
            pl.BlockSpec((LANES, 2 * LANES), lambda h, i: (0, 0)),
        ],
        out_specs=pl.BlockSpec((tq, SB_DH), lambda h, i: (i, h)),
        out_shape=jax.ShapeDtypeStruct((s, SB_WIDTH), BF16),
        scratch_shapes=[pltpu.VMEM((tq, SB_DH), F32), pltpu.VMEM((tq, LANES), F32)],
        compiler_params=_cparams(("parallel", "parallel")),
        name="stick_breaking",
    )(kmax, proj, proj, proj, ntri)


def _merge_kernel(h_ref, g0_ref, g1_ref, g2_ref, ydn_ref, yswa_ref, ysb_ref, wdn_ref, wswa_ref, wsb_ref, wo_ref,
                  o_ref):
    merged = _sigmoid(g0_ref[...].astype(F32)) * _dot(ydn_ref[...], wdn_ref[...])
    merged = merged + _sigmoid(g1_ref[...].astype(F32)) * _dot(yswa_ref[...], wswa_ref[...])
    merged = merged + _sigmoid(g2_ref[...].astype(F32)) * _dot(ysb_ref[...], wsb_ref[...])
    o_ref[...] = h_ref[...] + _dot(merged.astype(BF16), wo_ref[...])


def _merge(h, proj, y_dn, y_swa, y_sb, w_dn, w_swa, w_sb, w_o, l, tm):
    s = h.shape[0]
    rows = lambda width, col=0: pl.BlockSpec((tm, width), lambda i: (i, col))
    const = lambda a: pl.BlockSpec((None,) + a.shape[1:], lambda i: (l, 0, 0), pipeline_mode=pl.Buffered(1))
    return pl.pallas_call(
        _merge_kernel,
        grid=(s // tm,),
        in_specs=[
            rows(D_MODEL), rows(D_MODEL, 0), rows(D_MODEL, 1), rows(D_MODEL, 2),
            rows(DN_WIDTH), rows(SWA_WIDTH), rows(SB_WIDTH),
            const(w_dn), const(w_swa), const(w_sb), const(w_o),
        ],
        out_specs=rows(D_MODEL),
        out_shape=jax.ShapeDtypeStruct((s, D_MODEL), F32),
        compiler_params=_cparams(("parallel",)),
        name="merge",
    )(h, proj, proj, proj, y_dn, y_swa, y_sb, w_dn, w_swa, w_sb, w_o)


def _norm_matmul_kernel(x_ref, g_ref, w_ref, o_ref):
    xn = _rms_rows(x_ref[...], g_ref[...]).astype(BF16)
    o_ref[...] = _dot(xn, w_ref[...]).astype(o_ref.dtype)


def _mem_kv(mem, g, w, l):
    m = mem.shape[0]
    n = w.shape[2]
    return pl.pallas_call(
        _norm_matmul_kernel,
        grid=(1,),
        in_specs=[
            pl.BlockSpec((m, D_MODEL), lambda i: (0, 0)),
            pl.BlockSpec((1, D_MODEL), lambda i: (0, 0)),
            pl.BlockSpec((None, D_MODEL, n), lambda i: (l, 0, 0)),
        ],
        out_specs=pl.BlockSpec((m, n), lambda i: (0, 0)),
        out_shape=jax.ShapeDtypeStruct((m, n), BF16),
        compiler_params=_cparams(("arbitrary",)),
        name="mem_kv",
    )(mem, g, w)


def _xattn_kernel(h_ref, g_ref, kv_ref, wq_ref, wo_ref, o_ref):
    h = h_ref[...]
    hn = _rms_rows(h, g_ref[...]).astype(BF16)
    q = _dot(hn, wq_ref[...]).astype(BF16)
    outs = []
    for hd in range(X_HEADS):
        cs = slice(hd * X_DH, (hd + 1) * X_DH)
        kh = kv_ref[:, cs]
        vh = kv_ref[:, X_WIDTH + hd * X_DH:X_WIDTH + (hd + 1) * X_DH]
        sc = _dot_nt(q[:, cs], kh) * (X_DH ** -0.5)
        mx = jnp.max(sc, axis=-1, keepdims=True)
        p = jnp.exp(sc - mx)
        den = jnp.sum(p, axis=-1, keepdims=True)
        outs.append((_dot(p.astype(BF16), vh) / den).astype(BF16))
    o = jnp.concatenate(outs, axis=1)
    o_ref[...] = h + _dot(o, wo_ref[...])


def _xattn(h, g, kv, w_q, w_o, l, tm):
    s = h.shape[0]
    const = lambda a: pl.BlockSpec(a.shape, lambda i: (0, 0))
    layer = lambda a: pl.BlockSpec((None,) + a.shape[1:], lambda i: (l, 0, 0))
    return pl.pallas_call(
        _xattn_kernel,
        grid=(s // tm,),
        in_specs=[pl.BlockSpec((tm, D_MODEL), lambda i: (i, 0)), const(g), const(kv), layer(w_q), layer(w_o)],
        out_specs=pl.BlockSpec((tm, D_MODEL), lambda i: (i, 0)),
        out_shape=jax.ShapeDtypeStruct((s, D_MODEL), F32),
        compiler_params=_cparams(("parallel",)),
        name="xattn",
    )(h, g, kv, w_q, w_o)


def _ffn_kernel(h_ref, g_ref, wg_ref, wv_ref, cg_ref, cv_ref, wd_ref, o_ref, hn_ref, gbuf, vbuf, carry):
    i = pl.program_id(0)
    j = pl.program_id(1)
    tm = h_ref.shape[0]
    halo = SUBLANES

    @pl.when(j == 0)
    def _():
        h = h_ref[...]
        hn_ref[...] = _rms_rows(h, g_ref[...]).astype(BF16)
        o_ref[...] = h

    @pl.when(i == 0)
    def _():
        carry[j] = jnp.zeros(carry.shape[1:], F32)

    hn = hn_ref[...]
    gbuf[0:halo, :] = carry[j, 0]
    vbuf[0:halo, :] = carry[j, 1]

    gbuf[halo:, :] = _dot(hn, wg_ref[...])
    vbuf[halo:, :] = _dot(hn, wv_ref[...])
    carry[j, 0] = gbuf[tm:tm + halo, :]
    carry[j, 1] = vbuf[tm:tm + halo, :]

    def conv(buf, cw_ref):
        base = halo - (FFN_CONV - 1)
        out = cw_ref[0:1, :] * buf[base:base + tm, :]
        for tap in range(1, FFN_CONV):
            out = out + cw_ref[tap:tap + 1, :] * buf[base + tap:base + tap + tm, :]
        return out

    gate = conv(gbuf, cg_ref)
    val = conv(vbuf, cv_ref)
    act = (gate * _sigmoid(gate) * val).astype(BF16)
    o_ref[...] += _dot(act, wd_ref[...])


def _ffn(h, g, w_up, conv_w, w_down, l, tm, tn):
    s = h.shape[0]
    nj = D_FF // tn
    return pl.pallas_call(
        _ffn_kernel,
        grid=(s // tm, nj),
        in_specs=[
            pl.BlockSpec((tm, D_MODEL), lambda i, j: (i, 0)),
            pl.BlockSpec((1, D_MODEL), lambda i, j: (0, 0)),
            pl.BlockSpec((None, D_MODEL, tn), lambda i, j: (l, 0, j)),
            pl.BlockSpec((None, D_MODEL, tn), lambda i, j: (l, 0, j + nj)),
            pl.BlockSpec((FFN_CONV, tn), lambda i, j: (0, j)),
            pl.BlockSpec((FFN_CONV, tn), lambda i, j: (0, j + nj)),
            pl.BlockSpec((None, tn, D_MODEL), lambda i, j: (l, j, 0)),
        ],
        out_specs=pl.BlockSpec((tm, D_MODEL), lambda i, j: (i, 0)),
        out_shape=jax.ShapeDtypeStruct((s, D_MODEL), F32),
        scratch_shapes=[
            pltpu.VMEM((tm, D_MODEL), BF16),
            pltpu.VMEM((tm + SUBLANES, tn), F32),
            pltpu.VMEM((tm + SUBLANES, tn), F32),
            pltpu.VMEM((nj, 2, SUBLANES, tn), F32),
        ],
        compiler_params=_cparams(("arbitrary", "arbitrary")),
        name="conv_ffn",
    )(h, g, w_up, w_up, conv_w, conv_w, w_down)


def _final_norm_kernel(h_ref, g_ref, o_ref):
    o_ref[...] = _rms_rows(h_ref[...], g_ref[...])


def _final_norm(h, g, tm):
    s = h.shape[0]
    return pl.pallas_call(
        _final_norm_kernel,
        grid=(s // tm,),
        in_specs=[pl.BlockSpec((tm, D_MODEL), lambda i: (i, 0)), pl.BlockSpec((1, D_MODEL), lambda i: (0, 0))],
        out_specs=pl.BlockSpec((tm, D_MODEL), lambda i: (i, 0)),
        out_shape=jax.ShapeDtypeStruct((s, D_MODEL), F32),
        compiler_params=_cparams(("parallel",)),
        name="final_norm",
    )(h, g)


def _tile(s, want):
    return min(s, want)


def _reorder_w_in(w_in):
    o_z = 2 * DN_QK_WIDTH + DN_WIDTH
    o_a = o_z + DN_WIDTH
    o_swq = o_a + 2 * DN_HEADS
    o_swkv = o_swq + SWA_WIDTH
    o_sb = o_swkv + 2 * SWA_KV_WIDTH
    o_gate = o_sb + 3 * SB_WIDTH
    big = jnp.concatenate(
        [w_in[..., o_gate:], w_in[..., 0:o_a], w_in[..., o_swq:o_swkv], w_in[..., o_sb:o_gate],
         w_in[..., o_swkv:o_sb]], axis=-1).astype(BF16)
    ab = w_in[..., o_a:o_swq]
    ab = jnp.pad(ab, ((0, 0), (0, 0), (0, AB_COLS - ab.shape[-1]))).astype(BF16)
    return big, ab


def kernel(x, mem, norm_mix, w_in, dn_conv, dn_a_log, dn_dt_bias, dn_norm, swa_sinks, w_br_dn, w_br_swa, w_br_sb, w_o, norm_xattn, norm_mem, w_xq, w_xkv, w_xo, norm_ffn, w_up, ffn_conv, w_down, norm_final):
    depth = w_in.shape[0]
    s = x.shape[1]
    assert x.shape[0] == 1 and s % 256 == 0
    h = x[0]
    mem2 = mem[0]

    w_big, w_ab = _reorder_w_in(w_in)
    w_br_dn_b = w_br_dn.astype(BF16)
    w_br_swa_b = w_br_swa.astype(BF16)
    w_br_sb_b = w_br_sb.astype(BF16)
    w_o_b = w_o.astype(BF16)
    w_xq_b = w_xq.astype(BF16)
    w_xkv_b = w_xkv.astype(BF16)
    w_xo_b = w_xo.astype(BF16)
    w_up_b = w_up.astype(BF16)
    w_down_b = w_down.astype(BF16)
    pad8 = lambda a: jnp.pad(a, ((0, 0), (0, AB_COLS - a.shape[-1])))
    a_log_pad = pad8(dn_a_log)
    dt_bias_pad = pad8(dn_dt_bias)

    for l in range(depth):
        proj, ab = _in_proj(h, norm_mix[l][None], w_big, w_ab, l, tm=_tile(s, 1024), tn=1792)
        y_dn = _delta_net(proj, ab, dn_conv[l], a_log_pad[l][None], dt_bias_pad[l][None], dn_norm[l][None],
                          t=_tile(s, 256))
        y_swa = _swa(proj, swa_sinks[l], tq=_tile(s, 512))
        y_sb = _stick_breaking(proj, tq=_tile(s, 256))
        h = _merge(h, proj, y_dn, y_swa, y_sb, w_br_dn_b, w_br_swa_b, w_br_sb_b, w_o_b, l, tm=_tile(s, 256))
        kv = _mem_kv(mem2, norm_mem[l][None], w_xkv_b, l)
        h = _xattn(h, norm_xattn[l][None], kv, w_xq_b, w_xo_b, l, tm=_tile(s, 512))
        h = _ffn(h, norm_ffn[l][None], w_up_b, ffn_conv[l], w_down_b, l, tm=_tile(s, 512), tn=1024)
    return _final_norm(h, norm_final[None], tm=_tile(s, 1024))[None]
```

```python
import functools

import jax
import jax.numpy as jnp
from jax import lax
from jax.experimental import pallas as pl
from jax.experimental.pallas import tpu as pltpu

F32 = jnp.float32
BF16 = jnp.bfloat16

D_MODEL = 2048
EPS = 1e-6
DN_HEADS = 8
DN_DK = 128
DN_DV = 128
DN_CONV = 4
DN_BLOCK = 128
DN_SOLVE_BASE = 2
SWA_HEADS = 8
SWA_KV_HEADS = 2
SWA_DH = 64
WINDOW = 128
SB_HEADS = 4
SB_DH = 128
X_HEADS = 4
X_DH = 128
D_FF = 4096
FFN_CONV = 3
N_BRANCH = 3

DN_QK_WIDTH = DN_HEADS * DN_DK
DN_WIDTH = DN_HEADS * DN_DV
SWA_WIDTH = SWA_HEADS * SWA_DH
SWA_KV_WIDTH = SWA_KV_HEADS * SWA_DH
SB_WIDTH = SB_HEADS * SB_DH
X_WIDTH = X_HEADS * X_DH

LANES = 128
SUBLANES = 8
VMEM_LIMIT = 56 * 1024 * 1024

PROJ_COLS = N_BRANCH * D_MODEL + 3 * DN_QK_WIDTH + DN_WIDTH + SWA_WIDTH + 3 * SB_WIDTH + 2 * SWA_KV_WIDTH
COL_DN_Q = N_BRANCH * D_MODEL
COL_DN_Z = COL_DN_Q + 3 * DN_QK_WIDTH
COL_SWA_Q = COL_DN_Z + DN_WIDTH
COL_SB_Q = COL_SWA_Q + SWA_WIDTH
COL_SWA_KV = COL_SB_Q + 3 * SB_WIDTH
AB_COLS = LANES

NEG_BIG = -1e30
LOG2E = 1.4426950408889634
SB_SKIP_LOG2 = 170.0


def _cparams(semantics):
    return pltpu.CompilerParams(dimension_semantics=semantics, vmem_limit_bytes=VMEM_LIMIT)


def _dot(a, b):
    return jnp.dot(a, b, preferred_element_type=F32)


def _dot_nt(a, b, precision=None):
    return lax.dot_general(a, b, (((1,), (1,)), ((), ())), preferred_element_type=F32, precision=precision)


def _split3(x):
    x1 = x.astype(BF16)
    r1 = x - x1.astype(F32)
    x2 = r1.astype(BF16)
    x3 = (r1 - x2.astype(F32)).astype(BF16)
    return x1, x2, x3


def _dot_01_lhs(m01, x):
    x1, x2, x3 = _split3(x)
    return _dot(m01, x1) + _dot(m01, x2) + _dot(m01, x3)


def _dot_01_rhs(x, m01):
    x1, x2, x3 = _split3(x)
    return _dot(x1, m01) + _dot(x2, m01) + _dot(x3, m01)


def _dot_solve(a, b):
    return jnp.dot(a.astype(BF16), b.astype(BF16), preferred_element_type=F32)


def _sigmoid(x):
    return 1.0 / (1.0 + jnp.exp(-x))


def _softplus(x):
    return jnp.maximum(x, 0.0) + jnp.log1p(jnp.exp(-jnp.abs(x)))


def _rms_rows(x, g):
    ms = jnp.mean(x * x, axis=-1, keepdims=True)
    return x * lax.rsqrt(ms + EPS) * g


def _in_proj_kernel(h_ref, g_ref, w_ref, wab_ref, o_ref, ab_ref, xn_ref):
    @pl.when(pl.program_id(1) == 0)
    def _():
        xn = _rms_rows(h_ref[...], g_ref[...]).astype(BF16)
        xn_ref[...] = xn
        ab_ref[...] = _dot(xn, wab_ref[...])

    o_ref[...] = _dot(xn_ref[...], w_ref[...]).astype(o_ref.dtype)


def _in_proj(h, g, w, wab, l, tm, tn):
    s = h.shape[0]
    n = w.shape[2]
    return pl.pallas_call(
        _in_proj_kernel,
        grid=(s // tm, n // tn),
        in_specs=[
            pl.BlockSpec((tm, D_MODEL), lambda i, j: (i, 0)),
            pl.BlockSpec((1, D_MODEL), lambda i, j: (0, 0)),
            pl.BlockSpec((None, D_MODEL, tn), lambda i, j: (l, 0, j)),
            pl.BlockSpec((None, D_MODEL, AB_COLS), lambda i, j: (l, 0, 0)),
        ],
        out_specs=[
            pl.BlockSpec((tm, tn), lambda i, j: (i, j)),
            pl.BlockSpec((tm, AB_COLS), lambda i, j: (i, 0)),
        ],
        out_shape=[jax.ShapeDtypeStruct((s, n), BF16), jax.ShapeDtypeStruct((s, AB_COLS), F32)],
        scratch_shapes=[pltpu.VMEM((tm, D_MODEL), BF16)],
        compiler_params=_cparams(("parallel", "arbitrary")),
        name="in_proj",
    )(h, g, w, wab)


def _dn_kernel(q_ref, k_ref, v_ref, z_ref, ab_ref, cw_ref, alog_ref, dtb_ref, nrm_ref, cum_ref, eg_ref, eb_ref,
               o_ref, xbuf, qs, ks, vs, gce, gle, ble, gct, nb, qb, mb, qkb, rbv, rbk, ub, wb, qdb, kdt, state):
    i = pl.program_id(0)
    t = q_ref.shape[0]
    c = DN_BLOCK
    halo = SUBLANES

    @pl.when(i == 0)
    def _():
        xbuf[0:halo, :] = jnp.zeros((halo, xbuf.shape[1]), F32)
        state[...] = jnp.zeros(state.shape, F32)

    @pl.when(i > 0)
    def _():
        xbuf[0:halo, :] = xbuf[t:t + halo, :]

    xbuf[halo:halo + t, 0:DN_QK_WIDTH] = q_ref[...].astype(F32)
    xbuf[halo:halo + t, DN_QK_WIDTH:2 * DN_QK_WIDTH] = k_ref[...].astype(F32)
    xbuf[halo:halo + t, 2 * DN_QK_WIDTH:] = v_ref[...].astype(F32)

    n_slabs = (2 * DN_QK_WIDTH + DN_WIDTH) // LANES
    for s in range(n_slabs):
        cs = slice(s * LANES, (s + 1) * LANES)
        acc = cw_ref[0:1, cs] * xbuf[halo - 3:halo - 3 + t, cs]
        for tap in range(1, DN_CONV):
            acc = acc + cw_ref[tap:tap + 1, cs] * xbuf[halo - 3 + tap:halo - 3 + tap + t, cs]
        y = acc * _sigmoid(acc)
        if s < 2 * DN_HEADS:
            y = y * lax.rsqrt(jnp.sum(y * y, axis=-1, keepdims=True) + EPS)
        if s < DN_HEADS:
            qs[:, cs] = y * (DN_DK ** -0.5)
        elif s < 2 * DN_HEADS:
            ks[:, (s - DN_HEADS) * LANES:(s - DN_HEADS + 1) * LANES] = y
        else:
            vs[:, (s - 2 * DN_HEADS) * LANES:(s - 2 * DN_HEADS + 1) * LANES] = y

    ab = ab_ref[...]
    g = -jnp.exp(alog_ref[...]) * _softplus(ab + dtb_ref[...])
    beta = _sigmoid(ab)
    cums = _dot_01_lhs(cum_ref[...], g)
    gce[...] = _dot_01_rhs(cums[0:t], eg_ref[...])
    gle[...] = _dot_01_rhs(cums[t:2 * t], eg_ref[...])
    ble[...] = _dot_01_rhs(beta, eb_ref[...])
    gct[...] = cums[0:t].T

    row = lax.broadcasted_iota(jnp.int32, (c, c), 0)
    col = lax.broadcasted_iota(jnp.int32, (c, c), 1)
    causal = row >= col
    strict = row > col
    nrm = nrm_ref[...]
    items = [(ci, hd) for ci in range(t // c) for hd in range(DN_HEADS)]

    for it, (ci, hd) in enumerate(items):
        rows = slice(ci * c, (ci + 1) * c)
        cs = slice(hd * LANES, (hd + 1) * LANES)
        k_ = ks[rows, cs]
        gcc = gce[rows, cs]
        g_row = gct[hd:hd + 1, rows]
        dm = jnp.where(causal, gcc - g_row, 0.0)
        decay = jnp.where(causal, jnp.exp(dm), 0.0)
        lhs = jnp.concatenate([k_ * ble[rows, cs], qs[rows, cs]], axis=0).astype(BF16)
        kq = _dot_nt(lhs, k_.astype(BF16))
        nb[it] = jnp.where(strict, kq[0:c] * decay, 0.0)
        qkb[it] = (kq[c:] * decay).astype(BF16)

    for it, (ci, hd) in enumerate(items):
        rows = slice(ci * c, (ci + 1) * c)
        cs = slice(hd * LANES, (hd + 1) * LANES)
        q_ = qs[rows, cs]
        k_ = ks[rows, cs]
        gcc = gce[rows, cs]
        bet = ble[rows, cs]
        egc = jnp.exp(gcc)
        rbv[rows, cs] = vs[rows, cs] * bet
        rbk[rows, cs] = k_ * bet * egc
        qdb[rows, cs] = (q_ * egc).astype(BF16)
        kdt[it] = (k_ * jnp.exp(gle[rows, cs] - gcc)).T.astype(BF16)

    base = DN_SOLVE_BASE
    levels = base.bit_length() - 2
    for it in range(len(items)):
        nd = jnp.where(row // base == col // base, nb[it], 0.0)
        qb[it] = -nd
        if levels > 0:
            mb[it] = _dot_solve(nd, nd)
    for lvl in range(levels):
        for it in range(len(items)):
            qm = qb[it]
            m = mb[it]
            if lvl < levels - 1:
                prod = _dot_solve(jnp.concatenate([qm, m], axis=0), m)
                qb[it] = qm + m + prod[0:c]
                mb[it] = prod[c:]
            else:
                qb[it] = qm + m + _dot_solve(qm, m)
    size = base
    while size < c:
        join = (row // (2 * size) == col // (2 * size)) & (row // size != col // size)
        for it in range(len(items)):
            qm = qb[it]
            lo = jnp.where(join, nb[it], 0.0)
            a = lo + _dot_solve(qm, lo)
            qb[it] = qm - a - _dot_solve(a, qm)
        size *= 2

    for it, (ci, hd) in enumerate(items):
        rows = slice(ci * c, (ci + 1) * c)
        cs = slice(hd * LANES, (hd + 1) * LANES)
        rhs = jnp.concatenate([rbv[rows, cs], rbk[rows, cs]], axis=1)
        uw = rhs + _dot_solve(qb[it], rhs)
        ub[rows, cs] = uw[:, 0:DN_DV]
        wb[rows, cs] = uw[:, DN_DV:].astype(BF16)

    for it, (ci, hd) in enumerate(items):
        rows = slice(ci * c, (ci + 1) * c)
        cs = slice(hd * LANES, (hd + 1) * LANES)
        st = state[hd]
        ws = _dot(jnp.concatenate([wb[rows, cs], qdb[rows, cs]], axis=0), st.astype(BF16))
        vnb = (ub[rows, cs] - ws[0:c]).astype(BF16)
        kv = _dot(jnp.concatenate([qkb[it], kdt[it]], axis=0), vnb)
        o = ws[c:] + kv[0:c]
        state[hd] = st * jnp.exp(gle[rows, cs]) + kv[c:]
        zz = z_ref[rows, cs].astype(F32)
        y = _rms_rows(o, nrm) * (zz * _sigmoid(zz))
        o_ref[rows, cs] = y.astype(o_ref.dtype)


def _dn_constants(t):
    r = jnp.arange(t)
    same = (r[:, None] // DN_BLOCK) == (r[None, :] // DN_BLOCK)
    lower = r[:, None] >= r[None, :]
    cum = jnp.concatenate([(same & lower).astype(BF16), same.astype(BF16)], axis=0)
    lane = jnp.arange(LANES)[:, None]
    head = jnp.arange(DN_WIDTH)[None, :] // DN_DV
    eg = (lane == head).astype(BF16)
    eb = (lane == head + DN_HEADS).astype(BF16)
    return cum, eg, eb


def _delta_net(proj, ab, conv_w, a_log_pad, dt_bias_pad, dn_norm, t):
    s = proj.shape[0]
    cum, eg, eb = _dn_constants(t)
    cb = COL_DN_Q // DN_QK_WIDTH
    n_items = (t // DN_BLOCK) * DN_HEADS
    const = lambda shape: pl.BlockSpec(shape, lambda i: (0, 0))
    return pl.pallas_call(
        _dn_kernel,
        grid=(s // t,),
        in_specs=[
            pl.BlockSpec((t, DN_QK_WIDTH), lambda i: (i, cb)),
            pl.BlockSpec((t, DN_QK_WIDTH), lambda i: (i, cb + 1)),
            pl.BlockSpec((t, DN_WIDTH), lambda i: (i, cb + 2)),
            pl.BlockSpec((t, DN_WIDTH), lambda i: (i, cb + 3)),
            pl.BlockSpec((t, AB_COLS), lambda i: (i, 0)),
            const((DN_CONV, 2 * DN_QK_WIDTH + DN_WIDTH)),
            const((1, AB_COLS)),
            const((1, AB_COLS)),
            const((1, DN_DV)),
            const((2 * t, t)),
            const((LANES, DN_WIDTH)),
            const((LANES, DN_WIDTH)),
        ],
        out_specs=pl.BlockSpec((t, DN_WIDTH), lambda i: (i, 0)),
        out_shape=jax.ShapeDtypeStruct((s, DN_WIDTH), BF16),
        scratch_shapes=[
            pltpu.VMEM((t + SUBLANES, 2 * DN_QK_WIDTH + DN_WIDTH), F32),
            pltpu.VMEM((t, DN_QK_WIDTH), F32),
            pltpu.VMEM((t, DN_QK_WIDTH), F32),
            pltpu.VMEM((t, DN_WIDTH), F32),
            pltpu.VMEM((t, DN_WIDTH), F32),
            pltpu.VMEM((t, DN_WIDTH), F32),
            pltpu.VMEM((t, DN_WIDTH), F32),
            pltpu.VMEM((LANES, t), F32),
            pltpu.VMEM((n_items, DN_BLOCK, DN_BLOCK), F32),
            pltpu.VMEM((n_items, DN_BLOCK, DN_BLOCK), F32),
            pltpu.VMEM((n_items, DN_BLOCK, DN_BLOCK), F32),
            pltpu.VMEM((n_items, DN_BLOCK, DN_BLOCK), BF16),
            pltpu.VMEM((t, DN_WIDTH), F32),
            pltpu.VMEM((t, DN_QK_WIDTH), F32),
            pltpu.VMEM((t, DN_WIDTH), F32),
            pltpu.VMEM((t, DN_QK_WIDTH), BF16),
            pltpu.VMEM((t, DN_QK_WIDTH), BF16),
            pltpu.VMEM((n_items, DN_DK, DN_BLOCK), BF16),
            pltpu.VMEM((DN_HEADS, DN_DK, DN_DV), F32),
        ],
        compiler_params=_cparams(("arbitrary",)),
        name="delta_net",
    )(proj, proj, proj, proj, ab, conv_w, a_log_pad, dt_bias_pad, dn_norm, cum, eg, eb)


def _swa_kernel(sinks_ref, q_ref, kv_ref, kvp_ref, bias_ref, o_ref):
    i = pl.program_id(0)
    tq = q_ref.shape[0]
    w = WINDOW
    nwb = tq // w
    grp = SWA_HEADS // SWA_KV_HEADS
    ext = jnp.concatenate([kvp_ref[...], kv_ref[...]], axis=0)
    kj = lax.broadcasted_iota(jnp.int32, (w, 2 * w), 1)
    no_prev = jnp.where(jnp.logical_and(kj < w, i == 0), NEG_BIG, 0.0)
    for wb in range(nwb):
        win = ext[wb * w:wb * w + 2 * w]
        qwb = q_ref[wb * w:(wb + 1) * w, :]
        outs = []
        for hk in range(SWA_KV_HEADS):
            kh = win[:, hk * SWA_DH:(hk + 1) * SWA_DH]
            vh = win[:, SWA_KV_WIDTH + hk * SWA_DH:SWA_KV_WIDTH + (hk + 1) * SWA_DH]
            for gi in range(grp):
                hd = hk * grp + gi
                qh = qwb[:, hd * SWA_DH:(hd + 1) * SWA_DH]
                sc = _dot_nt(qh, kh) * (SWA_DH ** -0.5) + bias_ref[hd]
                if wb == 0:
                    sc = sc + no_prev
                sink = sinks_ref[hd]
                mx = jnp.maximum(jnp.max(sc, axis=-1, keepdims=True), sink)
                p = jnp.exp(sc - mx)
                den = jnp.sum(p, axis=-1, keepdims=True) + jnp.exp(sink - mx)
                outs.append(_dot(p.astype(BF16), vh) / den)
        o_ref[wb * w:(wb + 1) * w, :] = jnp.concatenate(outs, axis=1).astype(o_ref.dtype)


def _swa_bias():
    qi = jnp.arange(WINDOW)[:, None]
    kj = jnp.arange(2 * WINDOW)[None, :]
    dist = qi + WINDOW - kj
    slopes = jnp.exp2(-8.0 * (jnp.arange(SWA_HEADS, dtype=F32) + 1.0) / SWA_HEADS)
    band = (dist >= 0) & (dist < WINDOW)
    return jnp.where(band[None], -slopes[:, None, None] * dist.astype(F32)[None], NEG_BIG)


def _swa(proj, sinks, bias, tq):
    s = proj.shape[0]
    nwb = tq // WINDOW
    qb = COL_SWA_Q // SWA_WIDTH
    kvb = COL_SWA_KV // (2 * SWA_KV_WIDTH)
    return pl.pallas_call(
        _swa_kernel,
        grid=(s // tq,),
        in_specs=[
            pl.BlockSpec(memory_space=pltpu.SMEM),
            pl.BlockSpec((tq, SWA_WIDTH), lambda i: (i, qb)),
            pl.BlockSpec((tq, 2 * SWA_KV_WIDTH), lambda i: (i, kvb)),
            pl.BlockSpec((WINDOW, 2 * SWA_KV_WIDTH), lambda i: (jnp.maximum(i * nwb - 1, 0), kvb)),
            pl.BlockSpec((SWA_HEADS, WINDOW, 2 * WINDOW), lambda i: (0, 0, 0)),
        ],
        out_specs=pl.BlockSpec((tq, SWA_WIDTH), lambda i: (i, 0)),
        out_shape=jax.ShapeDtypeStruct((s, SWA_WIDTH), BF16),
        compiler_params=_cparams(("parallel",)),
        name="swa",
    )(sinks, proj, proj, proj, bias)


def _sb_knorm_kernel(k_ref, o_ref, run_ref):
    t = pl.program_id(0)

    @pl.when(t == 0)
    def _():
        for hd in range(SB_HEADS):
            run_ref[hd] = 0.0

    k = k_ref[...].astype(F32)
    for hd in range(SB_HEADS):
        kh = k[:, hd * SB_DH:(hd + 1) * SB_DH]
        cur = jnp.maximum(run_ref[hd], jnp.max(jnp.sqrt(jnp.sum(kh * kh, axis=-1, keepdims=True))))
        run_ref[hd] = cur
        o_ref[t, hd] = cur


def _sb_key_norms(proj, tk):
    s = proj.shape[0]
    kb = COL_SB_Q // SB_WIDTH + 1
    return pl.pallas_call(
        _sb_knorm_kernel,
        grid=(s // tk,),
        in_specs=[pl.BlockSpec((tk, SB_WIDTH), lambda t: (t, kb))],
        out_specs=pl.BlockSpec(memory_space=pltpu.SMEM),
        out_shape=jax.ShapeDtypeStruct((s // tk, SB_HEADS), F32),
        scratch_shapes=[pltpu.SMEM((SB_HEADS,), F32)],
        compiler_params=_cparams(("arbitrary",)),
        name="sb_key_norms",
    )(proj)


def _sb_kernel(kmax_ref, q_ref, k_ref, v_ref, ntri_ref, o_ref, zbuf, spb, abuf, acc_ref, aft_ref):
    i = pl.program_id(0)
    tq = q_ref.shape[0]
    tk = tq
    blk = LANES
    nsub = tk // blk
    c2 = (SB_DH ** -0.5) * LOG2E
    ntri = ntri_ref[...]
    acc_ref[...] = jnp.zeros(acc_ref.shape, F32)
    aft_ref[...] = jnp.zeros(aft_ref.shape, F32)
    hcols = [slice(hd * SB_DH, (hd + 1) * SB_DH) for hd in range(SB_HEADS)]
    qn = []
    for hc in hcols:
        qf = q_ref[:, hc].astype(F32)
        qn.append(jnp.sqrt(jnp.sum(qf * qf, axis=-1, keepdims=True)) * c2)

    def tile(k0, mask):
        keys = pl.ds(k0, tk)
        for hd, hc in enumerate(hcols):
            zbuf[hd] = _dot_nt(q_ref[:, hc], k_ref[keys, hc]) * c2
        for hd in range(SB_HEADS):
            z2 = zbuf[hd]
            sp2 = jnp.maximum(z2, 0.0) + jnp.log(1.0 + jnp.exp2(-jnp.abs(z2))) * LOG2E
            if mask is not None:
                sp2 = jnp.where(mask, sp2, 0.0)
            spb[hd] = sp2.astype(BF16)
        for hd in range(SB_HEADS):
            aft = aft_ref[hd]
            revs = [None] * nsub
            for sb in reversed(range(nsub)):
                cum = _dot(spb[hd, :, sb * blk:(sb + 1) * blk], ntri)
                revs[sb] = cum[:, 0:blk] + aft
                aft = aft + cum[:, blk:]
            a = jnp.exp2(zbuf[hd] + jnp.concatenate(revs, axis=1))
            if mask is not None:
                a = jnp.where(mask, a, 0.0)
            abuf[hd] = a.astype(BF16)
            aft_ref[hd] = aft
        for hd, hc in enumerate(hcols):
            acc_ref[hd] += _dot(abuf[hd], v_ref[keys, hc])

    def live(kt):
        kt = jnp.maximum(kt, 0)
        bound = None
        for hd in range(SB_HEADS):
            b = qn[hd] * kmax_ref[kt, hd] + aft_ref[hd, :, 0:1]
            bound = b if bound is None else jnp.maximum(bound, b)
        return jnp.max(bound) > -SB_SKIP_LOG2

    row = lax.broadcasted_iota(jnp.int32, (tq, tk), 0)
    col = lax.broadcasted_iota(jnp.int32, (tq, tk), 1)
    tile(pl.multiple_of(i * tq, tq), row > col)

    def cond(carry):
        n, go = carry
        return jnp.logical_and(n < i, go)

    def body(carry):
        n, _ = carry
        kt = i - 1 - n
        tile(pl.multiple_of(kt * tk, tk), None)
        return n + 1, live(kt - 1)

    lax.while_loop(cond, body, (jnp.int32(0), live(i - 1)))
    for hd, hc in enumerate(hcols):
        o_ref[:, hc] = acc_ref[hd].astype(o_ref.dtype)


def _stick_breaking(proj, tq):
    s = proj.shape[0]
    qb = COL_SB_Q // SB_WIDTH
    r = jnp.arange(LANES)
    ntri = -jnp.concatenate([(r[:, None] >= r[None, :]).astype(BF16), jnp.ones((LANES, LANES), BF16)], axis=1)
    kmax = _sb_key_norms(proj, tq)
    resident = lambda col: pl.BlockSpec((s, SB_WIDTH), lambda i: (0, col), pipeline_mode=pl.Buffered(1))
    return pl.pallas_call(
        _sb_kernel,
        grid=(s // tq,),
        in_specs=[
            pl.BlockSpec(memory_space=pltpu.SMEM),
            pl.BlockSpec((tq, SB_WIDTH), lambda i: (i, qb)),
            resident(qb + 1),
            resident(qb + 2),
            pl.BlockSpec((LANES, 2 * LANES), lambda i: (0, 0)),
        ],
        out_specs=pl.BlockSpec((tq, SB_WIDTH), lambda i: (i, 0)),
        out_shape=jax.ShapeDtypeStruct((s, SB_WIDTH), BF16),
        scratch_shapes=[
            pltpu.VMEM((SB_HEADS, tq, tq), F32),
            pltpu.VMEM((SB_HEADS, tq, tq), BF16),
            pltpu.VMEM((SB_HEADS, tq, tq), BF16),
            pltpu.VMEM((SB_HEADS, tq, SB_DH), F32),
            pltpu.VMEM((SB_HEADS, tq, LANES), F32),
        ],
        compiler_params=_cparams(("parallel",)),
        name="stick_breaking",
    )(kmax, proj, proj, proj, ntri)


def _merge_kernel(h_ref, g0_ref, g1_ref, g2_ref, ydn_ref, yswa_ref, ysb_ref, wdn_ref, wswa_ref, wsb_ref, wo_ref,
                  o_ref):
    merged = _sigmoid(g0_ref[...].astype(F32)) * _dot(ydn_ref[...], wdn_ref[...])
    merged = merged + _sigmoid(g1_ref[...].astype(F32)) * _dot(yswa_ref[...], wswa_ref[...])
    merged = merged + _sigmoid(g2_ref[...].astype(F32)) * _dot(ysb_ref[...], wsb_ref[...])
    o_ref[...] = h_ref[...] + _dot(merged.astype(BF16), wo_ref[...])


def _merge(h, proj, y_dn, y_swa, y_sb, w_dn, w_swa, w_sb, w_o, l, tm):
    s = h.shape[0]
    rows = lambda width, col=0: pl.BlockSpec((tm, width), lambda i: (i, col))
    const = lambda a: pl.BlockSpec((None,) + a.shape[1:], lambda i: (l, 0, 0), pipeline_mode=pl.Buffered(1))
    return pl.pallas_call(
        _merge_kernel,
        grid=(s // tm,),
        in_specs=[
            rows(D_MODEL), rows(D_MODEL, 0), rows(D_MODEL, 1), rows(D_MODEL, 2),
            rows(DN_WIDTH), rows(SWA_WIDTH), rows(SB_WIDTH),
            const(w_dn), const(w_swa), const(w_sb), const(w_o),
        ],
        out_specs=rows(D_MODEL),
        out_shape=jax.ShapeDtypeStruct((s, D_MODEL), F32),
        compiler_params=_cparams(("parallel",)),
        name="merge",
    )(h, proj, proj, proj, y_dn, y_swa, y_sb, w_dn, w_swa, w_sb, w_o)


def _norm_matmul_kernel(x_ref, g_ref, w_ref, o_ref):
    xn = _rms_rows(x_ref[...], g_ref[...]).astype(BF16)
    o_ref[...] = _dot(xn, w_ref[...]).astype(o_ref.dtype)


def _mem_kv(mem, g, w, l):
    m = mem.shape[0]
    n = w.shape[2]
    return pl.pallas_call(
        _norm_matmul_kernel,
        grid=(1,),
        in_specs=[
            pl.BlockSpec((m, D_MODEL), lambda i: (0, 0)),
            pl.BlockSpec((1, D_MODEL), lambda i: (0, 0)),
            pl.BlockSpec((None, D_MODEL, n), lambda i: (l, 0, 0)),
        ],
        out_specs=pl.BlockSpec((m, n), lambda i: (0, 0)),
        out_shape=jax.ShapeDtypeStruct((m, n), BF16),
        compiler_params=_cparams(("arbitrary",)),
        name="mem_kv",
    )(mem, g, w)


def _xattn_kernel(h_ref, g_ref, kv_ref, wq_ref, wo_ref, o_ref):
    h = h_ref[...]
    hn = _rms_rows(h, g_ref[...]).astype(BF16)
    q = _dot(hn, wq_ref[...]).astype(BF16)
    outs = []
    for hd in range(X_HEADS):
        cs = slice(hd * X_DH, (hd + 1) * X_DH)
        kh = kv_ref[:, cs]
        vh = kv_ref[:, X_WIDTH + hd * X_DH:X_WIDTH + (hd + 1) * X_DH]
        sc = _dot_nt(q[:, cs], kh) * (X_DH ** -0.5)
        mx = jnp.max(sc, axis=-1, keepdims=True)
        p = jnp.exp(sc - mx)
        den = jnp.sum(p, axis=-1, keepdims=True)
        outs.append((_dot(p.astype(BF16), vh) / den).astype(BF16))
    o = jnp.concatenate(outs, axis=1)
    o_ref[...] = h + _dot(o, wo_ref[...])


def _xattn(h, g, kv, w_q, w_o, l, tm):
    s = h.shape[0]
    const = lambda a: pl.BlockSpec(a.shape, lambda i: (0, 0))
    layer = lambda a: pl.BlockSpec((None,) + a.shape[1:], lambda i: (l, 0, 0))
    return pl.pallas_call(
        _xattn_kernel,
        grid=(s // tm,),
        in_specs=[pl.BlockSpec((tm, D_MODEL), lambda i: (i, 0)), const(g), const(kv), layer(w_q), layer(w_o)],
        out_specs=pl.BlockSpec((tm, D_MODEL), lambda i: (i, 0)),
        out_shape=jax.ShapeDtypeStruct((s, D_MODEL), F32),
        compiler_params=_cparams(("parallel",)),
        name="xattn",
    )(h, g, kv, w_q, w_o)


def _ffn_kernel(h_ref, g_ref, wg_ref, wv_ref, cg_ref, cv_ref, wd_ref, gout_ref, o_ref, hn_ref, gbuf, vbuf, carry,
                *, norm_out):
    i = pl.program_id(0)
    j = pl.program_id(1)
    tm = h_ref.shape[0]
    halo = SUBLANES

    @pl.when(j == 0)
    def _():
        h = h_ref[...]
        hn_ref[...] = _rms_rows(h, g_ref[...]).astype(BF16)
        o_ref[...] = h

    @pl.when(i == 0)
    def _():
        carry[j] = jnp.zeros(carry.shape[1:], F32)

    hn = hn_ref[...]
    gbuf[0:halo, :] = carry[j, 0]
    vbuf[0:halo, :] = carry[j, 1]

    gbuf[halo:, :] = _dot(hn, wg_ref[...])
    vbuf[halo:, :] = _dot(hn, wv_ref[...])
    carry[j, 0] = gbuf[tm:tm + halo, :]
    carry[j, 1] = vbuf[tm:tm + halo, :]

    def conv(buf, cw_ref):
        base = halo - (FFN_CONV - 1)
        out = cw_ref[0:1, :] * buf[base:base + tm, :]
        for tap in range(1, FFN_CONV):
            out = out + cw_ref[tap:tap + 1, :] * buf[base + tap:base + tap + tm, :]
        return out

    gate = conv(gbuf, cg_ref)
    val = conv(vbuf, cv_ref)
    act = (gate * _sigmoid(gate) * val).astype(BF16)
    o_ref[...] += _dot(act, wd_ref[...])

    if norm_out:
        @pl.when(j == pl.num_programs(1) - 1)
        def _():
            o_ref[...] = _rms_rows(o_ref[...], gout_ref[...])


def _ffn(h, g, w_up, conv_w, w_down, g_out, l, tm, tn, norm_out):
    s = h.shape[0]
    nj = D_FF // tn
    return pl.pallas_call(
        functools.partial(_ffn_kernel, norm_out=norm_out),
        grid=(s // tm, nj),
        in_specs=[
            pl.BlockSpec((tm, D_MODEL), lambda i, j: (i, 0)),
            pl.BlockSpec((1, D_MODEL), lambda i, j: (0, 0)),
            pl.BlockSpec((None, D_MODEL, tn), lambda i, j: (l, 0, j)),
            pl.BlockSpec((None, D_MODEL, tn), lambda i, j: (l, 0, j + nj)),
            pl.BlockSpec((FFN_CONV, tn), lambda i, j: (0, j)),
            pl.BlockSpec((FFN_CONV, tn), lambda i, j: (0, j + nj)),
            pl.BlockSpec((None, tn, D_MODEL), lambda i, j: (l, j, 0)),
            pl.BlockSpec((1, D_MODEL), lambda i, j: (0, 0)),
        ],
        out_specs=pl.BlockSpec((tm, D_MODEL), lambda i, j: (i, 0)),
        out_shape=jax.ShapeDtypeStruct((s, D_MODEL), F32),
        scratch_shapes=[
            pltpu.VMEM((tm, D_MODEL), BF16),
            pltpu.VMEM((tm + SUBLANES, tn), F32),
            pltpu.VMEM((tm + SUBLANES, tn), F32),
            pltpu.VMEM((nj, 2, SUBLANES, tn), F32),
        ],
        compiler_params=_cparams(("arbitrary", "arbitrary")),
        name="conv_ffn",
    )(h, g, w_up, w_up, conv_w, conv_w, w_down, g_out)


def _tile(s, want):
    return min(s, want)


def _reorder_w_in(w_in):
    o_z = 2 * DN_QK_WIDTH + DN_WIDTH
    o_a = o_z + DN_WIDTH
    o_swq = o_a + 2 * DN_HEADS
    o_swkv = o_swq + SWA_WIDTH
    o_sb = o_swkv + 2 * SWA_KV_WIDTH
    o_gate = o_sb + 3 * SB_WIDTH
    big = jnp.concatenate(
        [w_in[..., o_gate:], w_in[..., 0:o_a], w_in[..., o_swq:o_swkv], w_in[..., o_sb:o_gate],
         w_in[..., o_swkv:o_sb]], axis=-1).astype(BF16)
    ab = w_in[..., o_a:o_swq]
    ab = jnp.pad(ab, ((0, 0), (0, 0), (0, AB_COLS - ab.shape[-1]))).astype(BF16)
    return big, ab


def kernel(x, mem, norm_mix, w_in, dn_conv, dn_a_log, dn_dt_bias, dn_norm, swa_sinks, w_br_dn, w_br_swa, w_br_sb, w_o, norm_xattn, norm_mem, w_xq, w_xkv, w_xo, norm_ffn, w_up, ffn_conv, w_down, norm_final):
    depth = w_in.shape[0]
    s = x.shape[1]
    assert x.shape[0] == 1 and s % 256 == 0
    h = x[0]
    mem2 = mem[0]

    w_big, w_ab = _reorder_w_in(w_in)
    w_br_dn_b = w_br_dn.astype(BF16)
    w_br_swa_b = w_br_swa.astype(BF16)
    w_br_sb_b = w_br_sb.astype(BF16)
    w_o_b = w_o.astype(BF16)
    w_xq_b = w_xq.astype(BF16)
    w_xkv_b = w_xkv.astype(BF16)
    w_xo_b = w_xo.astype(BF16)
    w_up_b = w_up.astype(BF16)
    w_down_b = w_down.astype(BF16)
    pad8 = lambda a: jnp.pad(a, ((0, 0), (0, AB_COLS - a.shape[-1])))
    a_log_pad = pad8(dn_a_log)
    dt_bias_pad = pad8(dn_dt_bias)
    swa_bias = _swa_bias()

    for l in range(depth):
        proj, ab = _in_proj(h, norm_mix[l][None], w_big, w_ab, l, tm=_tile(s, 1024), tn=1792)
        y_dn = _delta_net(proj, ab, dn_conv[l], a_log_pad[l][None], dt_bias_pad[l][None], dn_norm[l][None],
                          t=_tile(s, 256))
        y_swa = _swa(proj, swa_sinks[l], swa_bias, tq=_tile(s, 512))
        y_sb = _stick_breaking(proj, tq=_tile(s, 256))
        h = _merge(h, proj, y_dn, y_swa, y_sb, w_br_dn_b, w_br_swa_b, w_br_sb_b, w_o_b, l, tm=_tile(s, 256))
        kv = _mem_kv(mem2, norm_mem[l][None], w_xkv_b, l)
        h = _xattn(h, norm_xattn[l][None], kv, w_xq_b, w_xo_b, l, tm=_tile(s, 512))
        h = _ffn(h, norm_ffn[l][None], w_up_b, ffn_conv[l], w_down_b, norm_final[None], l, tm=_tile(s, 512),
                 tn=1024, norm_out=(l == depth - 1))
    return h[None]
```

```python
import functools

import jax
import jax.numpy as jnp
from jax import lax
from jax.experimental import pallas as pl
from jax.experimental.pallas import tpu as pltpu

F32 = jnp.float32
BF16 = jnp.bfloat16

D_MODEL = 2048
EPS = 1e-6
DN_HEADS = 8
DN_DK = 128
DN_DV = 128
DN_CONV = 4
DN_BLOCK = 128
DN_SOLVE_BASE = 2
SWA_HEADS = 8
SWA_KV_HEADS = 2
SWA_DH = 64
WINDOW = 128
SB_HEADS = 4
SB_DH = 128
X_HEADS = 4
X_DH = 128
D_FF = 4096
FFN_CONV = 3
N_BRANCH = 3

DN_QK_WIDTH = DN_HEADS * DN_DK
DN_WIDTH = DN_HEADS * DN_DV
SWA_WIDTH = SWA_HEADS * SWA_DH
SWA_KV_WIDTH = SWA_KV_HEADS * SWA_DH
SB_WIDTH = SB_HEADS * SB_DH
X_WIDTH = X_HEADS * X_DH

LANES = 128
SUBLANES = 8
VMEM_LIMIT = 56 * 1024 * 1024

PROJ_COLS = N_BRANCH * D_MODEL + 3 * DN_QK_WIDTH + DN_WIDTH + SWA_WIDTH + 3 * SB_WIDTH + 2 * SWA_KV_WIDTH
COL_DN_Q = N_BRANCH * D_MODEL
COL_DN_Z = COL_DN_Q + 3 * DN_QK_WIDTH
COL_SWA_Q = COL_DN_Z + DN_WIDTH
COL_SB_Q = COL_SWA_Q + SWA_WIDTH
COL_SWA_KV = COL_SB_Q + 3 * SB_WIDTH
AB_COLS = LANES

NEG_BIG = -1e30
LOG2E = 1.4426950408889634
SB_SKIP_LOG2 = 170.0
SB_NORM_ROWS = 1024
W_PREP_ROWS = 128


def _cparams(semantics):
    return pltpu.CompilerParams(dimension_semantics=semantics, vmem_limit_bytes=VMEM_LIMIT)


def _dot(a, b):
    return jnp.dot(a, b, preferred_element_type=F32)


def _dot_nt(a, b, precision=None):
    return lax.dot_general(a, b, (((1,), (1,)), ((), ())), preferred_element_type=F32, precision=precision)


def _split3(x):
    x1 = x.astype(BF16)
    r1 = x - x1.astype(F32)
    x2 = r1.astype(BF16)
    x3 = (r1 - x2.astype(F32)).astype(BF16)
    return x1, x2, x3


def _dot_01_lhs(m01, x):
    x1, x2, x3 = _split3(x)
    return _dot(m01, x1) + _dot(m01, x2) + _dot(m01, x3)


def _dot_01_rhs(x, m01):
    x1, x2, x3 = _split3(x)
    return _dot(x1, m01) + _dot(x2, m01) + _dot(x3, m01)


def _dot_solve(a, b):
    return jnp.dot(a.astype(BF16), b.astype(BF16), preferred_element_type=F32)


def _sigmoid(x):
    return 1.0 / (1.0 + jnp.exp(-x))


def _softplus(x):
    return jnp.maximum(x, 0.0) + jnp.log1p(jnp.exp(-jnp.abs(x)))


def _rms_rows(x, g):
    ms = jnp.mean(x * x, axis=-1, keepdims=True)
    return x * lax.rsqrt(ms + EPS) * g


def _in_proj_kernel(h_ref, g_ref, w_ref, wab_ref, o_ref, ab_ref, xn_ref):
    @pl.when(pl.program_id(1) == 0)
    def _():
        xn = _rms_rows(h_ref[...], g_ref[...]).astype(BF16)
        xn_ref[...] = xn
        ab_ref[...] = _dot(xn, wab_ref[...])

    o_ref[...] = _dot(xn_ref[...], w_ref[...]).astype(o_ref.dtype)


def _in_proj(h, g, w, wab, l, tm, tn):
    s = h.shape[0]
    n = w.shape[2]
    return pl.pallas_call(
        _in_proj_kernel,
        grid=(s // tm, n // tn),
        in_specs=[
            pl.BlockSpec((tm, D_MODEL), lambda i, j: (i, 0)),
            pl.BlockSpec((1, D_MODEL), lambda i, j: (0, 0)),
            pl.BlockSpec((None, D_MODEL, tn), lambda i, j: (l, 0, j)),
            pl.BlockSpec((None, D_MODEL, AB_COLS), lambda i, j: (l, 0, 0)),
        ],
        out_specs=[
            pl.BlockSpec((tm, tn), lambda i, j: (i, j)),
            pl.BlockSpec((tm, AB_COLS), lambda i, j: (i, 0)),
        ],
        out_shape=[jax.ShapeDtypeStruct((s, n), BF16), jax.ShapeDtypeStruct((s, AB_COLS), F32)],
        scratch_shapes=[pltpu.VMEM((tm, D_MODEL), BF16)],
        compiler_params=_cparams(("parallel", "arbitrary")),
        name="in_proj",
    )(h, g, w, wab)


def _dn_kernel(q_ref, k_ref, v_ref, z_ref, ab_ref, cw_ref, alog_ref, dtb_ref, nrm_ref, cum_ref, eg_ref, eb_ref,
               o_ref, xbuf, qs, ks, vs, gce, gle, ble, gct, nb, qb, mb, qkb, rbv, rbk, ub, wb, qdb, kdt, state):
    i = pl.program_id(0)
    t = q_ref.shape[0]
    c = DN_BLOCK
    halo = SUBLANES

    @pl.when(i == 0)
    def _():
        xbuf[0:halo, :] = jnp.zeros((halo, xbuf.shape[1]), F32)
        state[...] = jnp.zeros(state.shape, F32)

    @pl.when(i > 0)
    def _():
        xbuf[0:halo, :] = xbuf[t:t + halo, :]

    xbuf[halo:halo + t, 0:DN_QK_WIDTH] = q_ref[...].astype(F32)
    xbuf[halo:halo + t, DN_QK_WIDTH:2 * DN_QK_WIDTH] = k_ref[...].astype(F32)
    xbuf[halo:halo + t, 2 * DN_QK_WIDTH:] = v_ref[...].astype(F32)

    n_slabs = (2 * DN_QK_WIDTH + DN_WIDTH) // LANES
    for s in range(n_slabs):
        cs = slice(s * LANES, (s + 1) * LANES)
        acc = cw_ref[0:1, cs] * xbuf[halo - 3:halo - 3 + t, cs]
        for tap in range(1, DN_CONV):
            acc = acc + cw_ref[tap:tap + 1, cs] * xbuf[halo - 3 + tap:halo - 3 + tap + t, cs]
        y = acc * _sigmoid(acc)
        if s < 2 * DN_HEADS:
            y = y * lax.rsqrt(jnp.sum(y * y, axis=-1, keepdims=True) + EPS)
        if s < DN_HEADS:
            qs[:, cs] = y * (DN_DK ** -0.5)
        elif s < 2 * DN_HEADS:
            ks[:, (s - DN_HEADS) * LANES:(s - DN_HEADS + 1) * LANES] = y
        else:
            vs[:, (s - 2 * DN_HEADS) * LANES:(s - 2 * DN_HEADS + 1) * LANES] = y

    ab = ab_ref[...]
    g = -jnp.exp(alog_ref[...]) * _softplus(ab + dtb_ref[...])
    beta = _sigmoid(ab)
    cums = _dot_01_lhs(cum_ref[...], g)
    gce[...] = _dot_01_rhs(cums[0:t], eg_ref[...])
    gle[...] = _dot_01_rhs(cums[t:2 * t], eg_ref[...])
    ble[...] = _dot_01_rhs(beta, eb_ref[...])
    gct[...] = cums[0:t].T

    row = lax.broadcasted_iota(jnp.int32, (c, c), 0)
    col = lax.broadcasted_iota(jnp.int32, (c, c), 1)
    causal = row >= col
    strict = row > col
    nrm = nrm_ref[...]
    items = [(ci, hd) for ci in range(t // c) for hd in range(DN_HEADS)]

    for it, (ci, hd) in enumerate(items):
        rows = slice(ci * c, (ci + 1) * c)
        cs = slice(hd * LANES, (hd + 1) * LANES)
        k_ = ks[rows, cs]
        gcc = gce[rows, cs]
        g_row = gct[hd:hd + 1, rows]
        dm = jnp.where(causal, gcc - g_row, 0.0)
        decay = jnp.where(causal, jnp.exp(dm), 0.0)
        lhs = jnp.concatenate([k_ * ble[rows, cs], qs[rows, cs]], axis=0).astype(BF16)
        kq = _dot_nt(lhs, k_.astype(BF16))
        nb[it] = jnp.where(strict, kq[0:c] * decay, 0.0)
        qkb[it] = (kq[c:] * decay).astype(BF16)

    for it, (ci, hd) in enumerate(items):
        rows = slice(ci * c, (ci + 1) * c)
        cs = slice(hd * LANES, (hd + 1) * LANES)
        q_ = qs[rows, cs]
        k_ = ks[rows, cs]
        gcc = gce[rows, cs]
        bet = ble[rows, cs]
        egc = jnp.exp(gcc)
        rbv[rows, cs] = vs[rows, cs] * bet
        rbk[rows, cs] = k_ * bet * egc
        qdb[rows, cs] = (q_ * egc).astype(BF16)
        kdt[it] = (k_ * jnp.exp(gle[rows, cs] - gcc)).T.astype(BF16)

    base = DN_SOLVE_BASE
    levels = base.bit_length() - 2
    for it in range(len(items)):
        nd = jnp.where(row // base == col // base, nb[it], 0.0)
        qb[it] = -nd
        if levels > 0:
            mb[it] = _dot_solve(nd, nd)
    for lvl in range(levels):
        for it in range(len(items)):
            qm = qb[it]
            m = mb[it]
            if lvl < levels - 1:
                prod = _dot_solve(jnp.concatenate([qm, m], axis=0), m)
                qb[it] = qm + m + prod[0:c]
                mb[it] = prod[c:]
            else:
                qb[it] = qm + m + _dot_solve(qm, m)
    size = base
    while size < c:
        join = (row // (2 * size) == col // (2 * size)) & (row // size != col // size)
        for it in range(len(items)):
            qm = qb[it]
            lo = jnp.where(join, nb[it], 0.0)
            a = lo + _dot_solve(qm, lo)
            qb[it] = qm - a - _dot_solve(a, qm)
        size *= 2

    for it, (ci, hd) in enumerate(items):
        rows = slice(ci * c, (ci + 1) * c)
        cs = slice(hd * LANES, (hd + 1) * LANES)
        rhs = jnp.concatenate([rbv[rows, cs], rbk[rows, cs]], axis=1)
        uw = rhs + _dot_solve(qb[it], rhs)
        ub[rows, cs] = uw[:, 0:DN_DV]
        wb[rows, cs] = uw[:, DN_DV:].astype(BF16)

    for it, (ci, hd) in enumerate(items):
        rows = slice(ci * c, (ci + 1) * c)
        cs = slice(hd * LANES, (hd + 1) * LANES)
        st = state[hd]
        ws = _dot(jnp.concatenate([wb[rows, cs], qdb[rows, cs]], axis=0), st.astype(BF16))
        vnb = (ub[rows, cs] - ws[0:c]).astype(BF16)
        kv = _dot(jnp.concatenate([qkb[it], kdt[it]], axis=0), vnb)
        o = ws[c:] + kv[0:c]
        state[hd] = st * jnp.exp(gle[rows, cs]) + kv[c:]
        zz = z_ref[rows, cs].astype(F32)
        y = _rms_rows(o, nrm) * (zz * _sigmoid(zz))
        o_ref[rows, cs] = y.astype(o_ref.dtype)


def _dn_constants(t):
    r = jnp.arange(t)
    same = (r[:, None] // DN_BLOCK) == (r[None, :] // DN_BLOCK)
    lower = r[:, None] >= r[None, :]
    cum = jnp.concatenate([(same & lower).astype(BF16), same.astype(BF16)], axis=0)
    lane = jnp.arange(LANES)[:, None]
    head = jnp.arange(DN_WIDTH)[None, :] // DN_DV
    eg = (lane == head).astype(BF16)
    eb = (lane == head + DN_HEADS).astype(BF16)
    return cum, eg, eb


def _delta_net(proj, ab, conv_w, a_log_pad, dt_bias_pad, dn_norm, t):
    s = proj.shape[0]
    cum, eg, eb = _dn_constants(t)
    cb = COL_DN_Q // DN_QK_WIDTH
    n_items = (t // DN_BLOCK) * DN_HEADS
    const = lambda shape: pl.BlockSpec(shape, lambda i: (0, 0))
    return pl.pallas_call(
        _dn_kernel,
        grid=(s // t,),
        in_specs=[
            pl.BlockSpec((t, DN_QK_WIDTH), lambda i: (i, cb)),
            pl.BlockSpec((t, DN_QK_WIDTH), lambda i: (i, cb + 1)),
            pl.BlockSpec((t, DN_WIDTH), lambda i: (i, cb + 2)),
            pl.BlockSpec((t, DN_WIDTH), lambda i: (i, cb + 3)),
            pl.BlockSpec((t, AB_COLS), lambda i: (i, 0)),
            const((DN_CONV, 2 * DN_QK_WIDTH + DN_WIDTH)),
            const((1, AB_COLS)),
            const((1, AB_COLS)),
            const((1, DN_DV)),
            const((2 * t, t)),
            const((LANES, DN_WIDTH)),
            const((LANES, DN_WIDTH)),
        ],
        out_specs=pl.BlockSpec((t, DN_WIDTH), lambda i: (i, 0)),
        out_shape=jax.ShapeDtypeStruct((s, DN_WIDTH), BF16),
        scratch_shapes=[
            pltpu.VMEM((t + SUBLANES, 2 * DN_QK_WIDTH + DN_WIDTH), F32),
            pltpu.VMEM((t, DN_QK_WIDTH), F32),
            pltpu.VMEM((t, DN_QK_WIDTH), F32),
            pltpu.VMEM((t, DN_WIDTH), F32),
            pltpu.VMEM((t, DN_WIDTH), F32),
            pltpu.VMEM((t, DN_WIDTH), F32),
            pltpu.VMEM((t, DN_WIDTH), F32),
            pltpu.VMEM((LANES, t), F32),
            pltpu.VMEM((n_items, DN_BLOCK, DN_BLOCK), F32),
            pltpu.VMEM((n_items, DN_BLOCK, DN_BLOCK), F32),
            pltpu.VMEM((n_items, DN_BLOCK, DN_BLOCK), F32),
            pltpu.VMEM((n_items, DN_BLOCK, DN_BLOCK), BF16),
            pltpu.VMEM((t, DN_WIDTH), F32),
            pltpu.VMEM((t, DN_QK_WIDTH), F32),
            pltpu.VMEM((t, DN_WIDTH), F32),
            pltpu.VMEM((t, DN_QK_WIDTH), BF16),
            pltpu.VMEM((t, DN_QK_WIDTH), BF16),
            pltpu.VMEM((n_items, DN_DK, DN_BLOCK), BF16),
            pltpu.VMEM((DN_HEADS, DN_DK, DN_DV), F32),
        ],
        compiler_params=_cparams(("arbitrary",)),
        name="delta_net",
    )(proj, proj, proj, proj, ab, conv_w, a_log_pad, dt_bias_pad, dn_norm, cum, eg, eb)


def _swa_kernel(sinks_ref, q_ref, kv_ref, kvp_ref, bias_ref, o_ref, scb, pb, denb):
    i = pl.program_id(0)
    tq = q_ref.shape[0]
    w = WINDOW
    nwb = tq // w
    grp = SWA_HEADS // SWA_KV_HEADS
    ext = jnp.concatenate([kvp_ref[...], kv_ref[...]], axis=0)
    kj = lax.broadcasted_iota(jnp.int32, (w, 2 * w), 1)
    no_prev = jnp.where(jnp.logical_and(kj < w, i == 0), NEG_BIG, 0.0)
    for wb in range(nwb):
        win = ext[wb * w:wb * w + 2 * w]
        qwb = q_ref[wb * w:(wb + 1) * w, :]
        for hd in range(SWA_HEADS):
            kh = win[:, (hd // grp) * SWA_DH:(hd // grp + 1) * SWA_DH]
            sc = _dot_nt(qwb[:, hd * SWA_DH:(hd + 1) * SWA_DH], kh) * (SWA_DH ** -0.5) + bias_ref[hd]
            if wb == 0:
                sc = sc + no_prev
            scb[hd] = sc
        for hd in range(SWA_HEADS):
            sc = scb[hd]
            sink = sinks_ref[hd]
            mx = jnp.maximum(jnp.max(sc, axis=-1, keepdims=True), sink)
            p = jnp.exp(sc - mx)
            pb[hd] = p.astype(BF16)
            den = jnp.sum(p, axis=-1, keepdims=True) + jnp.exp(sink - mx)
            denb[hd] = jnp.broadcast_to(den, (w, SWA_DH))
        outs = []
        for hd in range(SWA_HEADS):
            hk = hd // grp
            vh = win[:, SWA_KV_WIDTH + hk * SWA_DH:SWA_KV_WIDTH + (hk + 1) * SWA_DH]
            outs.append(_dot(pb[hd], vh) / denb[hd])
        o_ref[wb * w:(wb + 1) * w, :] = jnp.concatenate(outs, axis=1).astype(o_ref.dtype)


def _swa_bias():
    qi = jnp.arange(WINDOW)[:, None]
    kj = jnp.arange(2 * WINDOW)[None, :]
    dist = qi + WINDOW - kj
    slopes = jnp.exp2(-8.0 * (jnp.arange(SWA_HEADS, dtype=F32) + 1.0) / SWA_HEADS)
    band = (dist >= 0) & (dist < WINDOW)
    return jnp.where(band[None], -slopes[:, None, None] * dist.astype(F32)[None], NEG_BIG)


def _swa(proj, sinks, bias, tq):
    s = proj.shape[0]
    nwb = tq // WINDOW
    qb = COL_SWA_Q // SWA_WIDTH
    kvb = COL_SWA_KV // (2 * SWA_KV_WIDTH)
    return pl.pallas_call(
        _swa_kernel,
        grid=(s // tq,),
        in_specs=[
            pl.BlockSpec(memory_space=pltpu.SMEM),
            pl.BlockSpec((tq, SWA_WIDTH), lambda i: (i, qb)),
            pl.BlockSpec((tq, 2 * SWA_KV_WIDTH), lambda i: (i, kvb)),
            pl.BlockSpec((WINDOW, 2 * SWA_KV_WIDTH), lambda i: (jnp.maximum(i * nwb - 1, 0), kvb)),
            pl.BlockSpec((SWA_HEADS, WINDOW, 2 * WINDOW), lambda i: (0, 0, 0)),
        ],
        out_specs=pl.BlockSpec((tq, SWA_WIDTH), lambda i: (i, 0)),
        out_shape=jax.ShapeDtypeStruct((s, SWA_WIDTH), BF16),
        scratch_shapes=[
            pltpu.VMEM((SWA_HEADS, WINDOW, 2 * WINDOW), F32),
            pltpu.VMEM((SWA_HEADS, WINDOW, 2 * WINDOW), BF16),
            pltpu.VMEM((SWA_HEADS, WINDOW, SWA_DH), F32),
        ],
        compiler_params=_cparams(("parallel",)),
        name="swa",
    )(sinks, proj, proj, proj, bias)


def _sb_knorm_kernel(k_ref, o_ref, run_ref, *, tk):
    @pl.when(pl.program_id(0) == 0)
    def _():
        run_ref[...] = jnp.zeros(run_ref.shape, F32)

    run = run_ref[...]
    for sub in range(k_ref.shape[0] // tk):
        k = k_ref[sub * tk:(sub + 1) * tk, :].astype(F32)
        cur = []
        for hd in range(SB_HEADS):
            kh = k[:, hd * SB_DH:(hd + 1) * SB_DH]
            norm = jnp.sqrt(jnp.sum(kh * kh, axis=-1, keepdims=True))
            cur.append(jnp.broadcast_to(jnp.max(norm, axis=0, keepdims=True), (1, SB_DH)))
        run = jnp.maximum(run, jnp.concatenate(cur, axis=1))
        o_ref[sub] = run
    run_ref[...] = run


def _sb_key_norms(proj, tk):
    s = proj.shape[0]
    kb = COL_SB_Q // SB_WIDTH + 1
    rows = min(s, SB_NORM_ROWS)
    return pl.pallas_call(
        functools.partial(_sb_knorm_kernel, tk=tk),
        grid=(s // rows,),
        in_specs=[pl.BlockSpec((rows, SB_WIDTH), lambda t: (t, kb))],
        out_specs=pl.BlockSpec((rows // tk, 1, SB_WIDTH), lambda t: (t, 0, 0)),
        out_shape=jax.ShapeDtypeStruct((s // tk, 1, SB_WIDTH), F32),
        scratch_shapes=[pltpu.VMEM((1, SB_WIDTH), F32)],
        compiler_params=_cparams(("arbitrary",)),
        name="sb_key_norms",
    )(proj)


def _sb_kernel(kmax_ref, q_ref, k_ref, v_ref, ntri_ref, o_ref, zbuf, spb, abuf, acc_ref, aft_ref, qn_ref):
    i = pl.program_id(0)
    tq = q_ref.shape[0]
    tk = tq
    blk = LANES
    nsub = tk // blk
    c2 = (SB_DH ** -0.5) * LOG2E
    ntri = ntri_ref[...]
    acc_ref[...] = jnp.zeros(acc_ref.shape, F32)
    aft_ref[...] = jnp.zeros(aft_ref.shape, F32)
    hcols = [slice(hd * SB_DH, (hd + 1) * SB_DH) for hd in range(SB_HEADS)]
    for hd, hc in enumerate(hcols):
        qf = q_ref[:, hc].astype(F32)
        qn = jnp.sqrt(jnp.sum(qf * qf, axis=-1, keepdims=True)) * c2
        qn_ref[hd] = jnp.broadcast_to(qn, (tq, LANES))

    def tile(k0, mask):
        keys = pl.ds(k0, tk)
        for hd, hc in enumerate(hcols):
            zbuf[hd] = _dot_nt(q_ref[:, hc], k_ref[keys, hc]) * c2
        for hd in range(SB_HEADS):
            z2 = zbuf[hd]
            sp2 = jnp.maximum(z2, 0.0) + jnp.log(1.0 + jnp.exp2(-jnp.abs(z2))) * LOG2E
            if mask is not None:
                sp2 = jnp.where(mask, sp2, 0.0)
            spb[hd] = sp2.astype(BF16)
        for hd in range(SB_HEADS):
            aft = aft_ref[hd]
            revs = [None] * nsub
            for sb in reversed(range(nsub)):
                cum = _dot(spb[hd, :, sb * blk:(sb + 1) * blk], ntri)
                revs[sb] = cum[:, 0:blk] + aft
                aft = aft + cum[:, blk:]
            a = jnp.exp2(zbuf[hd] + jnp.concatenate(revs, axis=1))
            if mask is not None:
                a = jnp.where(mask, a, 0.0)
            abuf[hd] = a.astype(BF16)
            aft_ref[hd] = aft
        for hd, hc in enumerate(hcols):
            acc_ref[hd] += _dot(abuf[hd], v_ref[keys, hc])

    def live(kt):
        kmax = kmax_ref[jnp.maximum(kt, 0)]
        bound = None
        for hd, hc in enumerate(hcols):
            b = qn_ref[hd] * kmax[:, hc] + aft_ref[hd]
            bound = b if bound is None else jnp.maximum(bound, b)
        return jnp.max(bound) > -SB_SKIP_LOG2

    row = lax.broadcasted_iota(jnp.int32, (tq, tk), 0)
    col = lax.broadcasted_iota(jnp.int32, (tq, tk), 1)
    tile(pl.multiple_of(i * tq, tq), row > col)

    def cond(carry):
        n, go = carry
        return jnp.logical_and(n < i, go)

    def body(carry):
        n, _ = carry
        kt = i - 1 - n
        tile(pl.multiple_of(kt * tk, tk), None)
        return n + 1, live(kt - 1)

    lax.while_loop(cond, body, (jnp.int32(0), live(i - 1)))
    for hd, hc in enumerate(hcols):
        o_ref[:, hc] = acc_ref[hd].astype(o_ref.dtype)


def _stick_breaking(proj, tq):
    s = proj.shape[0]
    qb = COL_SB_Q // SB_WIDTH
    r = jnp.arange(LANES)
    ntri = -jnp.concatenate([(r[:, None] >= r[None, :]).astype(BF16), jnp.ones((LANES, LANES), BF16)], axis=1)
    kmax = _sb_key_norms(proj, tq)
    resident = lambda col: pl.BlockSpec((s, SB_WIDTH), lambda i: (0, col), pipeline_mode=pl.Buffered(1))
    return pl.pallas_call(
        _sb_kernel,
        grid=(s // tq,),
        in_specs=[
            pl.BlockSpec((s // tq, 1, SB_WIDTH), lambda i: (0, 0, 0)),
            pl.BlockSpec((tq, SB_WIDTH), lambda i: (i, qb)),
            resident(qb + 1),
            resident(qb + 2),
            pl.BlockSpec((LANES, 2 * LANES), lambda i: (0, 0)),
        ],
        out_specs=pl.BlockSpec((tq, SB_WIDTH), lambda i: (i, 0)),
        out_shape=jax.ShapeDtypeStruct((s, SB_WIDTH), BF16),
        scratch_shapes=[
            pltpu.VMEM((SB_HEADS, tq, tq), F32),
            pltpu.VMEM((SB_HEADS, tq, tq), BF16),
            pltpu.VMEM((SB_HEADS, tq, tq), BF16),
            pltpu.VMEM((SB_HEADS, tq, SB_DH), F32),
            pltpu.VMEM((SB_HEADS, tq, LANES), F32),
            pltpu.VMEM((SB_HEADS, tq, LANES), F32),
        ],
        compiler_params=_cparams(("parallel",)),
        name="stick_breaking",
    )(kmax, proj, proj, proj, ntri)


def _merge_kernel(h_ref, g0_ref, g1_ref, g2_ref, ydn_ref, yswa_ref, ysb_ref, wdn_ref, wswa_ref, wsb_ref, wo_ref,
                  o_ref):
    merged = _sigmoid(g0_ref[...].astype(F32)) * _dot(ydn_ref[...], wdn_ref[...])
    merged = merged + _sigmoid(g1_ref[...].astype(F32)) * _dot(yswa_ref[...], wswa_ref[...])
    merged = merged + _sigmoid(g2_ref[...].astype(F32)) * _dot(ysb_ref[...], wsb_ref[...])
    o_ref[...] = h_ref[...] + _dot(merged.astype(BF16), wo_ref[...])


def _merge(h, proj, y_dn, y_swa, y_sb, w_dn, w_swa, w_sb, w_o, l, tm):
    s = h.shape[0]
    rows = lambda width, col=0: pl.BlockSpec((tm, width), lambda i: (i, col))
    const = lambda a: pl.BlockSpec((None,) + a.shape[1:], lambda i: (l, 0, 0), pipeline_mode=pl.Buffered(1))
    return pl.pallas_call(
        _merge_kernel,
        grid=(s // tm,),
        in_specs=[
            rows(D_MODEL), rows(D_MODEL, 0), rows(D_MODEL, 1), rows(D_MODEL, 2),
            rows(DN_WIDTH), rows(SWA_WIDTH), rows(SB_WIDTH),
            const(w_dn), const(w_swa), const(w_sb), const(w_o),
        ],
        out_specs=rows(D_MODEL),
        out_shape=jax.ShapeDtypeStruct((s, D_MODEL), F32),
        compiler_params=_cparams(("parallel",)),
        name="merge",
    )(h, proj, proj, proj, y_dn, y_swa, y_sb, w_dn, w_swa, w_sb, w_o)


def _norm_matmul_kernel(x_ref, g_ref, w_ref, o_ref):
    xn = _rms_rows(x_ref[...], g_ref[...]).astype(BF16)
    o_ref[...] = _dot(xn, w_ref[...]).astype(o_ref.dtype)


def _mem_kv(mem, g, w, l):
    m = mem.shape[0]
    n = w.shape[2]
    return pl.pallas_call(
        _norm_matmul_kernel,
        grid=(1,),
        in_specs=[
            pl.BlockSpec((m, D_MODEL), lambda i: (0, 0)),
            pl.BlockSpec((1, D_MODEL), lambda i: (0, 0)),
            pl.BlockSpec((None, D_MODEL, n), lambda i: (l, 0, 0)),
        ],
        out_specs=pl.BlockSpec((m, n), lambda i: (0, 0)),
        out_shape=jax.ShapeDtypeStruct((m, n), BF16),
        compiler_params=_cparams(("arbitrary",)),
        name="mem_kv",
    )(mem, g, w)


def _xattn_kernel(h_ref, g_ref, kv_ref, wq_ref, wo_ref, o_ref):
    h = h_ref[...]
    hn = _rms_rows(h, g_ref[...]).astype(BF16)
    q = _dot(hn, wq_ref[...]).astype(BF16)
    outs = []
    for hd in range(X_HEADS):
        cs = slice(hd * X_DH, (hd + 1) * X_DH)
        kh = kv_ref[:, cs]
        vh = kv_ref[:, X_WIDTH + hd * X_DH:X_WIDTH + (hd + 1) * X_DH]
        sc = _dot_nt(q[:, cs], kh) * (X_DH ** -0.5)
        mx = jnp.max(sc, axis=-1, keepdims=True)
        p = jnp.exp(sc - mx)
        den = jnp.sum(p, axis=-1, keepdims=True)
        outs.append((_dot(p.astype(BF16), vh) / den).astype(BF16))
    o = jnp.concatenate(outs, axis=1)
    o_ref[...] = h + _dot(o, wo_ref[...])


def _xattn(h, g, kv, w_q, w_o, l, tm):
    s = h.shape[0]
    const = lambda a: pl.BlockSpec(a.shape, lambda i: (0, 0))
    layer = lambda a: pl.BlockSpec((None,) + a.shape[1:], lambda i: (l, 0, 0))
    return pl.pallas_call(
        _xattn_kernel,
        grid=(s // tm,),
        in_specs=[pl.BlockSpec((tm, D_MODEL), lambda i: (i, 0)), const(g), const(kv), layer(w_q), layer(w_o)],
        out_specs=pl.BlockSpec((tm, D_MODEL), lambda i: (i, 0)),
        out_shape=jax.ShapeDtypeStruct((s, D_MODEL), F32),
        compiler_params=_cparams(("parallel",)),
        name="xattn",
    )(h, g, kv, w_q, w_o)


def _ffn_kernel(h_ref, g_ref, wg_ref, wv_ref, cg_ref, cv_ref, wd_ref, gout_ref, o_ref, hn_ref, gbuf, vbuf, carry,
                *, norm_out):
    i = pl.program_id(0)
    j = pl.program_id(1)
    tm = h_ref.shape[0]
    halo = SUBLANES

    @pl.when(j == 0)
    def _():
        h = h_ref[...]
        hn_ref[...] = _rms_rows(h, g_ref[...]).astype(BF16)
        o_ref[...] = h

    @pl.when(i == 0)
    def _():
        carry[j] = jnp.zeros(carry.shape[1:], F32)

    hn = hn_ref[...]
    gbuf[0:halo, :] = carry[j, 0]
    vbuf[0:halo, :] = carry[j, 1]

    gbuf[halo:, :] = _dot(hn, wg_ref[...])
    vbuf[halo:, :] = _dot(hn, wv_ref[...])
    carry[j, 0] = gbuf[tm:tm + halo, :]
    carry[j, 1] = vbuf[tm:tm + halo, :]

    def conv(buf, cw_ref):
        base = halo - (FFN_CONV - 1)
        out = cw_ref[0:1, :] * buf[base:base + tm, :]
        for tap in range(1, FFN_CONV):
            out = out + cw_ref[tap:tap + 1, :] * buf[base + tap:base + tap + tm, :]
        return out

    gate = conv(gbuf, cg_ref)
    val = conv(vbuf, cv_ref)
    act = (gate * _sigmoid(gate) * val).astype(BF16)
    o_ref[...] += _dot(act, wd_ref[...])

    if norm_out:
        @pl.when(j == pl.num_programs(1) - 1)
        def _():
            o_ref[...] = _rms_rows(o_ref[...], gout_ref[...])


def _ffn(h, g, w_up, conv_w, w_down, g_out, l, tm, tn, norm_out):
    s = h.shape[0]
    nj = D_FF // tn
    return pl.pallas_call(
        functools.partial(_ffn_kernel, norm_out=norm_out),
        grid=(s // tm, nj),
        in_specs=[
            pl.BlockSpec((tm, D_MODEL), lambda i, j: (i, 0)),
            pl.BlockSpec((1, D_MODEL), lambda i, j: (0, 0)),
            pl.BlockSpec((None, D_MODEL, tn), lambda i, j: (l, 0, j)),
            pl.BlockSpec((None, D_MODEL, tn), lambda i, j: (l, 0, j + nj)),
            pl.BlockSpec((FFN_CONV, tn), lambda i, j: (0, j)),
            pl.BlockSpec((FFN_CONV, tn), lambda i, j: (0, j + nj)),
            pl.BlockSpec((None, tn, D_MODEL), lambda i, j: (l, j, 0)),
            pl.BlockSpec((1, D_MODEL), lambda i, j: (0, 0)),
        ],
        out_specs=pl.BlockSpec((tm, D_MODEL), lambda i, j: (i, 0)),
        out_shape=jax.ShapeDtypeStruct((s, D_MODEL), F32),
        scratch_shapes=[
            pltpu.VMEM((tm, D_MODEL), BF16),
            pltpu.VMEM((tm + SUBLANES, tn), F32),
            pltpu.VMEM((tm + SUBLANES, tn), F32),
            pltpu.VMEM((nj, 2, SUBLANES, tn), F32),
        ],
        compiler_params=_cparams(("arbitrary", "arbitrary")),
        name="conv_ffn",
    )(h, g, w_up, w_up, conv_w, conv_w, w_down, g_out)


def _tile(s, want):
    return min(s, want)


def _w_in_pieces():
    o_z = 2 * DN_QK_WIDTH + DN_WIDTH
    o_a = o_z + DN_WIDTH
    o_swq = o_a + 2 * DN_HEADS
    o_swkv = o_swq + SWA_WIDTH
    o_sb = o_swkv + 2 * SWA_KV_WIDTH
    o_gate = o_sb + 3 * SB_WIDTH
    o_end = o_gate + N_BRANCH * D_MODEL
    big = [(o_gate, o_end), (0, o_a), (o_swq, o_swkv), (o_sb, o_gate), (o_swkv, o_sb)]
    return big, (o_a, o_swq), o_end


def _w_in_prep_kernel(w_ref, big_ref, ab_ref):
    pieces, (a0, a1), _ = _w_in_pieces()
    w = w_ref[...]
    col = 0
    for lo, hi in pieces:
        big_ref[:, col:col + hi - lo] = w[:, lo:hi].astype(BF16)
        col += hi - lo
    ab_ref[...] = jnp.zeros(ab_ref.shape, BF16)
    ab_ref[:, 0:a1 - a0] = w[:, a0:a1].astype(BF16)


def _reorder_w_in(w_in):
    depth = w_in.shape[0]
    in_cols = _w_in_pieces()[2]
    rows = W_PREP_ROWS
    return pl.pallas_call(
        _w_in_prep_kernel,
        grid=(depth, D_MODEL // rows),
        in_specs=[pl.BlockSpec((None, rows, in_cols), lambda l, r: (l, r, 0))],
        out_specs=[
            pl.BlockSpec((None, rows, PROJ_COLS), lambda l, r: (l, r, 0)),
            pl.BlockSpec((None, rows, AB_COLS), lambda l, r: (l, r, 0)),
        ],
        out_shape=[jax.ShapeDtypeStruct((depth, D_MODEL, PROJ_COLS), BF16),
                   jax.ShapeDtypeStruct((depth, D_MODEL, AB_COLS), BF16)],
        compiler_params=_cparams(("parallel", "parallel")),
        name="w_in_prep",
    )(w_in)


def kernel(x, mem, norm_mix, w_in, dn_conv, dn_a_log, dn_dt_bias, dn_norm, swa_sinks, w_br_dn, w_br_swa, w_br_sb, w_o, norm_xattn, norm_mem, w_xq, w_xkv, w_xo, norm_ffn, w_up, ffn_conv, w_down, norm_final):
    depth = w_in.shape[0]
    s = x.shape[1]
    assert x.shape[0] == 1 and s % 256 == 0
    h = x[0]
    mem2 = mem[0]

    w_big, w_ab = _reorder_w_in(w_in)
    w_br_dn_b = w_br_dn.astype(BF16)
    w_br_swa_b = w_br_swa.astype(BF16)
    w_br_sb_b = w_br_sb.astype(BF16)
    w_o_b = w_o.astype(BF16)
    w_xq_b = w_xq.astype(BF16)
    w_xkv_b = w_xkv.astype(BF16)
    w_xo_b = w_xo.astype(BF16)
    w_up_b = w_up.astype(BF16)
    w_down_b = w_down.astype(BF16)
    pad8 = lambda a: jnp.pad(a, ((0, 0), (0, AB_COLS - a.shape[-1])))
    a_log_pad = pad8(dn_a_log)
    dt_bias_pad = pad8(dn_dt_bias)
    swa_bias = _swa_bias()

    for l in range(depth):
        proj, ab = _in_proj(h, norm_mix[l][None], w_big, w_ab, l, tm=_tile(s, 1024), tn=1792)
        y_dn = _delta_net(proj, ab, dn_conv[l], a_log_pad[l][None], dt_bias_pad[l][None], dn_norm[l][None],
                          t=_tile(s, 256))
        y_swa = _swa(proj, swa_sinks[l], swa_bias, tq=_tile(s, 512))
        y_sb = _stick_breaking(proj, tq=_tile(s, 256))
        h = _merge(h, proj, y_dn, y_swa, y_sb, w_br_dn_b, w_br_swa_b, w_br_sb_b, w_o_b, l, tm=_tile(s, 256))
        kv = _mem_kv(mem2, norm_mem[l][None], w_xkv_b, l)
        h = _xattn(h, norm_xattn[l][None], kv, w_xq_b, w_xo_b, l, tm=_tile(s, 512))
        h = _ffn(h, norm_ffn[l][None], w_up_b, ffn_conv[l], w_down_b, norm_final[None], l, tm=_tile(s, 512),
                 tn=1024, norm_out=(l == depth - 1))
    return h[None]
```

```python
import functools

import jax
import jax.numpy as jnp
from jax import lax
from jax.experimental import pallas as pl
from jax.experimental.pallas import tpu as pltpu

F32 = jnp.float32
BF16 = jnp.bfloat16

D_MODEL = 2048
EPS = 1e-6
DN_HEADS = 8
DN_DK = 128
DN_DV = 128
DN_CONV = 4
DN_BLOCK = 128
SWA_HEADS = 8
SWA_KV_HEADS = 2
SWA_DH = 64
WINDOW = 128
SB_HEADS = 4
SB_DH = 128
X_HEADS = 4
X_DH = 128
D_FF = 4096
FFN_CONV = 3
N_BRANCH = 3

DN_QK_WIDTH = DN_HEADS * DN_DK
DN_WIDTH = DN_HEADS * DN_DV
SWA_WIDTH = SWA_HEADS * SWA_DH
SWA_KV_WIDTH = SWA_KV_HEADS * SWA_DH
SB_WIDTH = SB_HEADS * SB_DH
X_WIDTH = X_HEADS * X_DH

LANES = 128
SUBLANES = 8
VMEM_LIMIT = 56 * 1024 * 1024

PROJ_COLS = N_BRANCH * D_MODEL + 3 * DN_QK_WIDTH + DN_WIDTH + SWA_WIDTH + 3 * SB_WIDTH + 2 * SWA_KV_WIDTH
COL_DN_Q = N_BRANCH * D_MODEL
COL_DN_Z = COL_DN_Q + 3 * DN_QK_WIDTH
COL_SWA_Q = COL_DN_Z + DN_WIDTH
COL_SB_Q = COL_SWA_Q + SWA_WIDTH
COL_SWA_KV = COL_SB_Q + 3 * SB_WIDTH
AB_COLS = LANES

NEG_BIG = -1e30
LOG2E = 1.4426950408889634
SB_SKIP_LOG2 = 170.0
SB_NORM_ROWS = 1024


def _cparams(semantics):
    return pltpu.CompilerParams(dimension_semantics=semantics, vmem_limit_bytes=VMEM_LIMIT)


def _dot(a, b):
    return jnp.dot(a, b, preferred_element_type=F32)


def _dot_nt(a, b, precision=None):
    return lax.dot_general(a, b, (((1,), (1,)), ((), ())), preferred_element_type=F32, precision=precision)


def _split3(x):
    x1 = x.astype(BF16)
    r1 = x - x1.astype(F32)
    x2 = r1.astype(BF16)
    x3 = (r1 - x2.astype(F32)).astype(BF16)
    return x1, x2, x3


def _dot_01_lhs(m01, x):
    x1, x2, x3 = _split3(x)
    return _dot(m01, x1) + _dot(m01, x2) + _dot(m01, x3)


def _dot_01_rhs(x, m01):
    x1, x2, x3 = _split3(x)
    return _dot(x1, m01) + _dot(x2, m01) + _dot(x3, m01)


def _dot_solve(a, b):
    return jnp.dot(a.astype(BF16), b.astype(BF16), preferred_element_type=F32)


def _sigmoid(x):
    return 1.0 / (1.0 + jnp.exp(-x))


def _softplus(x):
    return jnp.maximum(x, 0.0) + jnp.log1p(jnp.exp(-jnp.abs(x)))


def _rms_rows(x, g):
    ms = jnp.mean(x * x, axis=-1, keepdims=True)
    return x * lax.rsqrt(ms + EPS) * g


def _in_proj_kernel(h_ref, g_ref, wt_ref, wabt_ref, o_ref, ab_ref, xn_ref):
    @pl.when(pl.program_id(1) == 0)
    def _():
        xn = _rms_rows(h_ref[...], g_ref[...]).astype(BF16)
        xn_ref[...] = xn
        ab_ref[...] = _dot_nt(xn, wabt_ref[...])

    o_ref[...] = _dot_nt(xn_ref[...], wt_ref[...]).astype(o_ref.dtype)


def _in_proj(h, g, w, wab, l, tm, tn):
    s = h.shape[0]
    n = w.shape[1]
    return pl.pallas_call(
        _in_proj_kernel,
        grid=(s // tm, n // tn),
        in_specs=[
            pl.BlockSpec((tm, D_MODEL), lambda i, j: (i, 0)),
            pl.BlockSpec((1, D_MODEL), lambda i, j: (0, 0)),
            pl.BlockSpec((None, tn, D_MODEL), lambda i, j: (l, j, 0)),
            pl.BlockSpec((None, AB_COLS, D_MODEL), lambda i, j: (l, 0, 0)),
        ],
        out_specs=[
            pl.BlockSpec((tm, tn), lambda i, j: (i, j)),
            pl.BlockSpec((tm, AB_COLS), lambda i, j: (i, 0)),
        ],
        out_shape=[jax.ShapeDtypeStruct((s, n), BF16), jax.ShapeDtypeStruct((s, AB_COLS), F32)],
        scratch_shapes=[pltpu.VMEM((tm, D_MODEL), BF16)],
        compiler_params=_cparams(("parallel", "arbitrary")),
        name="in_proj",
    )(h, g, w, wab)


def _dn_kernel(q_ref, k_ref, v_ref, z_ref, ab_ref, cw_ref, alog_ref, dtb_ref, nrm_ref, cum_ref, eg_ref, eb_ref,
               o_ref, xbuf, qs, ks, vs, gce, gle, ble, gct, nb, qb, qkb, rbv, rbk, ub, wb, qdb, kdt, state):
    i = pl.program_id(0)
    t = q_ref.shape[0]
    c = DN_BLOCK
    halo = SUBLANES

    @pl.when(i == 0)
    def _():
        xbuf[0:halo, :] = jnp.zeros((halo, xbuf.shape[1]), F32)
        state[...] = jnp.zeros(state.shape, F32)

    @pl.when(i > 0)
    def _():
        xbuf[0:halo, :] = xbuf[t:t + halo, :]

    xbuf[halo:halo + t, 0:DN_QK_WIDTH] = q_ref[...].astype(F32)
    xbuf[halo:halo + t, DN_QK_WIDTH:2 * DN_QK_WIDTH] = k_ref[...].astype(F32)
    xbuf[halo:halo + t, 2 * DN_QK_WIDTH:] = v_ref[...].astype(F32)

    n_slabs = (2 * DN_QK_WIDTH + DN_WIDTH) // LANES
    for s in range(n_slabs):
        cs = slice(s * LANES, (s + 1) * LANES)
        acc = cw_ref[0:1, cs] * xbuf[halo - 3:halo - 3 + t, cs]
        for tap in range(1, DN_CONV):
            acc = acc + cw_ref[tap:tap + 1, cs] * xbuf[halo - 3 + tap:halo - 3 + tap + t, cs]
        y = acc * _sigmoid(acc)
        if s < 2 * DN_HEADS:
            y = y * lax.rsqrt(jnp.sum(y * y, axis=-1, keepdims=True) + EPS)
        if s < DN_HEADS:
            qs[:, cs] = y * (DN_DK ** -0.5)
        elif s < 2 * DN_HEADS:
            ks[:, (s - DN_HEADS) * LANES:(s - DN_HEADS + 1) * LANES] = y
        else:
            vs[:, (s - 2 * DN_HEADS) * LANES:(s - 2 * DN_HEADS + 1) * LANES] = y

    ab = ab_ref[...]
    g = -jnp.exp(alog_ref[...]) * _softplus(ab + dtb_ref[...])
    beta = _sigmoid(ab)
    cums = _dot_01_lhs(cum_ref[...], g)
    gce[...] = _dot_01_rhs(cums[0:t], eg_ref[...])
    gle[...] = _dot_01_rhs(cums[t:2 * t], eg_ref[...])
    ble[...] = _dot_01_rhs(beta, eb_ref[...])
    gct[...] = cums[0:t].T

    row = lax.broadcasted_iota(jnp.int32, (c, c), 0)
    col = lax.broadcasted_iota(jnp.int32, (c, c), 1)
    causal = row >= col
    strict = row > col
    pair = row // 2 == col // 2
    nrm = nrm_ref[...]
    items = [(ci, hd) for ci in range(t // c) for hd in range(DN_HEADS)]

    for it, (ci, hd) in enumerate(items):
        rows = slice(ci * c, (ci + 1) * c)
        cs = slice(hd * LANES, (hd + 1) * LANES)
        k_ = ks[rows, cs]
        gcc = gce[rows, cs]
        g_row = gct[hd:hd + 1, rows]
        dm = jnp.where(causal, gcc - g_row, 0.0)
        decay = jnp.where(causal, jnp.exp(dm), 0.0)
        lhs = jnp.concatenate([k_ * ble[rows, cs], qs[rows, cs]], axis=0).astype(BF16)
        kq = _dot_nt(lhs, k_.astype(BF16))
        nmat = jnp.where(strict, kq[0:c] * decay, 0.0)
        nb[it] = nmat
        qb[it] = jnp.where(pair, -nmat, 0.0)
        qkb[it] = (kq[c:] * decay).astype(BF16)

    for it, (ci, hd) in enumerate(items):
        rows = slice(ci * c, (ci + 1) * c)
        cs = slice(hd * LANES, (hd + 1) * LANES)
        q_ = qs[rows, cs]
        k_ = ks[rows, cs]
        gcc = gce[rows, cs]
        bet = ble[rows, cs]
        egc = jnp.exp(gcc)
        rbv[rows, cs] = vs[rows, cs] * bet
        rbk[rows, cs] = k_ * bet * egc
        qdb[rows, cs] = (q_ * egc).astype(BF16)
        kdt[it] = (k_ * jnp.exp(gle[rows, cs] - gcc)).T.astype(BF16)

    size = 2
    while size < c:
        join = (row // (2 * size) == col // (2 * size)) & (row // size != col // size)
        for it in range(len(items)):
            qm = qb[it]
            lo = jnp.where(join, nb[it], 0.0)
            a = lo + _dot_solve(qm, lo)
            qb[it] = qm - a - _dot_solve(a, qm)
        size *= 2

    for it, (ci, hd) in enumerate(items):
        rows = slice(ci * c, (ci + 1) * c)
        cs = slice(hd * LANES, (hd + 1) * LANES)
        rhs = jnp.concatenate([rbv[rows, cs], rbk[rows, cs]], axis=1)
        uw = rhs + _dot_solve(qb[it], rhs)
        ub[rows, cs] = uw[:, 0:DN_DV]
        wb[rows, cs] = uw[:, DN_DV:].astype(BF16)

    for it, (ci, hd) in enumerate(items):
        rows = slice(ci * c, (ci + 1) * c)
        cs = slice(hd * LANES, (hd + 1) * LANES)
        st = state[hd]
        ws = _dot(jnp.concatenate([wb[rows, cs], qdb[rows, cs]], axis=0), st.astype(BF16))
        vnb = (ub[rows, cs] - ws[0:c]).astype(BF16)
        kv = _dot(jnp.concatenate([qkb[it], kdt[it]], axis=0), vnb)
        o = ws[c:] + kv[0:c]
        state[hd] = st * jnp.exp(gle[rows, cs]) + kv[c:]
        zz = z_ref[rows, cs].astype(F32)
        y = _rms_rows(o, nrm) * (zz * _sigmoid(zz))
        o_ref[rows, cs] = y.astype(o_ref.dtype)


def _dn_constants(t):
    r = jnp.arange(t)
    same = (r[:, None] // DN_BLOCK) == (r[None, :] // DN_BLOCK)
    lower = r[:, None] >= r[None, :]
    cum = jnp.concatenate([(same & lower).astype(BF16), same.astype(BF16)], axis=0)
    lane = jnp.arange(LANES)[:, None]
    head = jnp.arange(DN_WIDTH)[None, :] // DN_DV
    eg = (lane == head).astype(BF16)
    eb = (lane == head + DN_HEADS).astype(BF16)
    return cum, eg, eb


def _delta_net(proj, ab, conv_w, a_log_pad, dt_bias_pad, dn_norm, t):
    s = proj.shape[0]
    cum, eg, eb = _dn_constants(t)
    cb = COL_DN_Q // DN_QK_WIDTH
    n_items = (t // DN_BLOCK) * DN_HEADS
    const = lambda shape: pl.BlockSpec(shape, lambda i: (0, 0))
    return pl.pallas_call(
        _dn_kernel,
        grid=(s // t,),
        in_specs=[
            pl.BlockSpec((t, DN_QK_WIDTH), lambda i: (i, cb)),
            pl.BlockSpec((t, DN_QK_WIDTH), lambda i: (i, cb + 1)),
            pl.BlockSpec((t, DN_WIDTH), lambda i: (i, cb + 2)),
            pl.BlockSpec((t, DN_WIDTH), lambda i: (i, cb + 3)),
            pl.BlockSpec((t, AB_COLS), lambda i: (i, 0)),
            const((DN_CONV, 2 * DN_QK_WIDTH + DN_WIDTH)),
            const((1, AB_COLS)),
            const((1, AB_COLS)),
            const((1, DN_DV)),
            const((2 * t, t)),
            const((LANES, DN_WIDTH)),
            const((LANES, DN_WIDTH)),
        ],
        out_specs=pl.BlockSpec((t, DN_WIDTH), lambda i: (i, 0)),
        out_shape=jax.ShapeDtypeStruct((s, DN_WIDTH), BF16),
        scratch_shapes=[
            pltpu.VMEM((t + SUBLANES, 2 * DN_QK_WIDTH + DN_WIDTH), F32),
            pltpu.VMEM((t, DN_QK_WIDTH), F32),
            pltpu.VMEM((t, DN_QK_WIDTH), F32),
            pltpu.VMEM((t, DN_WIDTH), F32),
            pltpu.VMEM((t, DN_WIDTH), F32),
            pltpu.VMEM((t, DN_WIDTH), F32),
            pltpu.VMEM((t, DN_WIDTH), F32),
            pltpu.VMEM((LANES, t), F32),
            pltpu.VMEM((n_items, DN_BLOCK, DN_BLOCK), F32),
            pltpu.VMEM((n_items, DN_BLOCK, DN_BLOCK), F32),
            pltpu.VMEM((n_items, DN_BLOCK, DN_BLOCK), BF16),
            pltpu.VMEM((t, DN_WIDTH), F32),
            pltpu.VMEM((t, DN_QK_WIDTH), F32),
            pltpu.VMEM((t, DN_WIDTH), F32),
            pltpu.VMEM((t, DN_QK_WIDTH), BF16),
            pltpu.VMEM((t, DN_QK_WIDTH), BF16),
            pltpu.VMEM((n_items, DN_DK, DN_BLOCK), BF16),
            pltpu.VMEM((DN_HEADS, DN_DK, DN_DV), F32),
        ],
        compiler_params=_cparams(("arbitrary",)),
        name="delta_net",
    )(proj, proj, proj, proj, ab, conv_w, a_log_pad, dt_bias_pad, dn_norm, cum, eg, eb)


def _swa_kernel(sinks_ref, q_ref, kv_ref, kvp_ref, bias_ref, o_ref, scb, pb, denb):
    i = pl.program_id(0)
    tq = q_ref.shape[0]
    w = WINDOW
    nwb = tq // w
    grp = SWA_HEADS // SWA_KV_HEADS
    ext = jnp.concatenate([kvp_ref[...], kv_ref[...]], axis=0)
    kj = lax.broadcasted_iota(jnp.int32, (w, 2 * w), 1)
    no_prev = jnp.where(jnp.logical_and(kj < w, i == 0), NEG_BIG, 0.0)
    for wb in range(nwb):
        win = ext[wb * w:wb * w + 2 * w]
        qwb = q_ref[wb * w:(wb + 1) * w, :]
        for hd in range(SWA_HEADS):
            kh = win[:, (hd // grp) * SWA_DH:(hd // grp + 1) * SWA_DH]
            sc = _dot_nt(qwb[:, hd * SWA_DH:(hd + 1) * SWA_DH], kh) * (SWA_DH ** -0.5) + bias_ref[hd]
            if wb == 0:
                sc = sc + no_prev
            scb[hd] = sc
        for hd in range(SWA_HEADS):
            sc = scb[hd]
            sink = sinks_ref[hd]
            mx = jnp.maximum(jnp.max(sc, axis=-1, keepdims=True), sink)
            p = jnp.exp(sc - mx)
            pb[hd] = p.astype(BF16)
            den = jnp.sum(p, axis=-1, keepdims=True) + jnp.exp(sink - mx)
            denb[hd] = jnp.broadcast_to(den, (w, SWA_DH))
        outs = []
        for hd in range(SWA_HEADS):
            hk = hd // grp
            vh = win[:, SWA_KV_WIDTH + hk * SWA_DH:SWA_KV_WIDTH + (hk + 1) * SWA_DH]
            outs.append(_dot(pb[hd], vh) / denb[hd])
        o_ref[wb * w:(wb + 1) * w, :] = jnp.concatenate(outs, axis=1).astype(o_ref.dtype)


def _swa_bias():
    qi = jnp.arange(WINDOW)[:, None]
    kj = jnp.arange(2 * WINDOW)[None, :]
    dist = qi + WINDOW - kj
    slopes = jnp.exp2(-8.0 * (jnp.arange(SWA_HEADS, dtype=F32) + 1.0) / SWA_HEADS)
    band = (dist >= 0) & (dist < WINDOW)
    return jnp.where(band[None], -slopes[:, None, None] * dist.astype(F32)[None], NEG_BIG)


def _swa(proj, sinks, bias, tq):
    s = proj.shape[0]
    nwb = tq // WINDOW
    qb = COL_SWA_Q // SWA_WIDTH
    kvb = COL_SWA_KV // (2 * SWA_KV_WIDTH)
    return pl.pallas_call(
        _swa_kernel,
        grid=(s // tq,),
        in_specs=[
            pl.BlockSpec(memory_space=pltpu.SMEM),
            pl.BlockSpec((tq, SWA_WIDTH), lambda i: (i, qb)),
            pl.BlockSpec((tq, 2 * SWA_KV_WIDTH), lambda i: (i, kvb)),
            pl.BlockSpec((WINDOW, 2 * SWA_KV_WIDTH), lambda i: (jnp.maximum(i * nwb - 1, 0), kvb)),
            pl.BlockSpec((SWA_HEADS, WINDOW, 2 * WINDOW), lambda i: (0, 0, 0)),
        ],
        out_specs=pl.BlockSpec((tq, SWA_WIDTH), lambda i: (i, 0)),
        out_shape=jax.ShapeDtypeStruct((s, SWA_WIDTH), BF16),
        scratch_shapes=[
            pltpu.VMEM((SWA_HEADS, WINDOW, 2 * WINDOW), F32),
            pltpu.VMEM((SWA_HEADS, WINDOW, 2 * WINDOW), BF16),
            pltpu.VMEM((SWA_HEADS, WINDOW, SWA_DH), F32),
        ],
        compiler_params=_cparams(("parallel",)),
        name="swa",
    )(sinks, proj, proj, proj, bias)


def _sb_knorm_kernel(k_ref, o_ref, run_ref, *, tk):
    @pl.when(pl.program_id(0) == 0)
    def _():
        run_ref[...] = jnp.zeros(run_ref.shape, F32)

    run = run_ref[...]
    for sub in range(k_ref.shape[0] // tk):
        k = k_ref[sub * tk:(sub + 1) * tk, :].astype(F32)
        cur = []
        for hd in range(SB_HEADS):
            kh = k[:, hd * SB_DH:(hd + 1) * SB_DH]
            norm = jnp.sqrt(jnp.sum(kh * kh, axis=-1, keepdims=True))
            cur.append(jnp.broadcast_to(jnp.max(norm, axis=0, keepdims=True), (1, SB_DH)))
        run = jnp.maximum(run, jnp.concatenate(cur, axis=1))
        o_ref[sub] = run
    run_ref[...] = run


def _sb_key_norms(proj, tk):
    s = proj.shape[0]
    kb = COL_SB_Q // SB_WIDTH + 1
    rows = min(s, SB_NORM_ROWS)
    return pl.pallas_call(
        functools.partial(_sb_knorm_kernel, tk=tk),
        grid=(s // rows,),
        in_specs=[pl.BlockSpec((rows, SB_WIDTH), lambda t: (t, kb))],
        out_specs=pl.BlockSpec((rows // tk, 1, SB_WIDTH), lambda t: (t, 0, 0)),
        out_shape=jax.ShapeDtypeStruct((s // tk, 1, SB_WIDTH), F32),
        scratch_shapes=[pltpu.VMEM((1, SB_WIDTH), F32)],
        compiler_params=_cparams(("arbitrary",)),
        name="sb_key_norms",
    )(proj)


def _sb_kernel(kmax_ref, q_ref, k_ref, v_ref, ntri_ref, o_ref, zbuf, spb, abuf, acc_ref, aft_ref, qn_ref):
    i = pl.program_id(0)
    tq = q_ref.shape[0]
    tk = tq
    blk = LANES
    nsub = tk // blk
    c2 = (SB_DH ** -0.5) * LOG2E
    ntri = ntri_ref[...]
    acc_ref[...] = jnp.zeros(acc_ref.shape, F32)
    aft_ref[...] = jnp.zeros(aft_ref.shape, F32)
    hcols = [slice(hd * SB_DH, (hd + 1) * SB_DH) for hd in range(SB_HEADS)]
    for hd, hc in enumerate(hcols):
        qf = q_ref[:, hc].astype(F32)
        qn = jnp.sqrt(jnp.sum(qf * qf, axis=-1, keepdims=True)) * c2
        qn_ref[hd] = jnp.broadcast_to(qn, (tq, LANES))

    def tile(k0, mask):
        keys = pl.ds(k0, tk)
        for hd, hc in enumerate(hcols):
            zbuf[hd] = _dot_nt(q_ref[:, hc], k_ref[keys, hc]) * c2
        for hd in range(SB_HEADS):
            z2 = zbuf[hd]
            sp2 = jnp.maximum(z2, 0.0) + jnp.log(1.0 + jnp.exp2(-jnp.abs(z2))) * LOG2E
            if mask is not None:
                sp2 = jnp.where(mask, sp2, 0.0)
            spb[hd] = sp2.astype(BF16)
        for hd in range(SB_HEADS):
            aft = aft_ref[hd]
            revs = [None] * nsub
            for sb in reversed(range(nsub)):
                cum = _dot(spb[hd, :, sb * blk:(sb + 1) * blk], ntri)
                revs[sb] = cum[:, 0:blk] + aft
                aft = aft + cum[:, blk:]
            a = jnp.exp2(zbuf[hd] + jnp.concatenate(revs, axis=1))
            if mask is not None:
                a = jnp.where(mask, a, 0.0)
            abuf[hd] = a.astype(BF16)
            aft_ref[hd] = aft
        for hd, hc in enumerate(hcols):
            acc_ref[hd] += _dot(abuf[hd], v_ref[keys, hc])

    def live(kt):
        kmax = kmax_ref[jnp.maximum(kt, 0)]
        bound = None
        for hd, hc in enumerate(hcols):
            b = qn_ref[hd] * kmax[:, hc] + aft_ref[hd]
            bound = b if bound is None else jnp.maximum(bound, b)
        return jnp.max(bound) > -SB_SKIP_LOG2

    row = lax.broadcasted_iota(jnp.int32, (tq, tk), 0)
    col = lax.broadcasted_iota(jnp.int32, (tq, tk), 1)
    tile(pl.multiple_of(i * tq, tq), row > col)

    def cond(carry):
        n, go = carry
        return jnp.logical_and(n < i, go)

    def body(carry):
        n, _ = carry
        kt = i - 1 - n
        tile(pl.multiple_of(kt * tk, tk), None)
        return n + 1, live(kt - 1)

    lax.while_loop(cond, body, (jnp.int32(0), live(i - 1)))
    for hd, hc in enumerate(hcols):
        o_ref[:, hc] = acc_ref[hd].astype(o_ref.dtype)


def _stick_breaking(proj, tq):
    s = proj.shape[0]
    qb = COL_SB_Q // SB_WIDTH
    r = jnp.arange(LANES)
    ntri = -jnp.concatenate([(r[:, None] >= r[None, :]).astype(BF16), jnp.ones((LANES, LANES), BF16)], axis=1)
    kmax = _sb_key_norms(proj, tq)
    resident = lambda col: pl.BlockSpec((s, SB_WIDTH), lambda i: (0, col), pipeline_mode=pl.Buffered(1))
    return pl.pallas_call(
        _sb_kernel,
        grid=(s // tq,),
        in_specs=[
            pl.BlockSpec((s // tq, 1, SB_WIDTH), lambda i: (0, 0, 0)),
            pl.BlockSpec((tq, SB_WIDTH), lambda i: (i, qb)),
            resident(qb + 1),
            resident(qb + 2),
            pl.BlockSpec((LANES, 2 * LANES), lambda i: (0, 0)),
        ],
        out_specs=pl.BlockSpec((tq, SB_WIDTH), lambda i: (i, 0)),
        out_shape=jax.ShapeDtypeStruct((s, SB_WIDTH), BF16),
        scratch_shapes=[
            pltpu.VMEM((SB_HEADS, tq, tq), F32),
            pltpu.VMEM((SB_HEADS, tq, tq), BF16),
            pltpu.VMEM((SB_HEADS, tq, tq), BF16),
            pltpu.VMEM((SB_HEADS, tq, SB_DH), F32),
            pltpu.VMEM((SB_HEADS, tq, LANES), F32),
            pltpu.VMEM((SB_HEADS, tq, LANES), F32),
        ],
        compiler_params=_cparams(("parallel",)),
        name="stick_breaking",
    )(kmax, proj, proj, proj, ntri)


def _merge_kernel(h_ref, g0_ref, g1_ref, g2_ref, ydn_ref, yswa_ref, ysb_ref, wdn_ref, wswa_ref, wsb_ref, wo_ref,
                  o_ref):
    merged = _sigmoid(g0_ref[...].astype(F32)) * _dot(ydn_ref[...], wdn_ref[...])
    merged = merged + _sigmoid(g1_ref[...].astype(F32)) * _dot(yswa_ref[...], wswa_ref[...])
    merged = merged + _sigmoid(g2_ref[...].astype(F32)) * _dot(ysb_ref[...], wsb_ref[...])
    o_ref[...] = h_ref[...] + _dot(merged.astype(BF16), wo_ref[...])


def _merge(h, proj, y_dn, y_swa, y_sb, w_dn, w_swa, w_sb, w_o, l, tm):
    s = h.shape[0]
    rows = lambda width, col=0: pl.BlockSpec((tm, width), lambda i: (i, col))
    const = lambda a: pl.BlockSpec((None,) + a.shape[1:], lambda i: (l, 0, 0), pipeline_mode=pl.Buffered(1))
    return pl.pallas_call(
        _merge_kernel,
        grid=(s // tm,),
        in_specs=[
            rows(D_MODEL), rows(D_MODEL, 0), rows(D_MODEL, 1), rows(D_MODEL, 2),
            rows(DN_WIDTH), rows(SWA_WIDTH), rows(SB_WIDTH),
            const(w_dn), const(w_swa), const(w_sb), const(w_o),
        ],
        out_specs=rows(D_MODEL),
        out_shape=jax.ShapeDtypeStruct((s, D_MODEL), F32),
        compiler_params=_cparams(("parallel",)),
        name="merge",
    )(h, proj, proj, proj, y_dn, y_swa, y_sb, w_dn, w_swa, w_sb, w_o)


def _norm_matmul_kernel(x_ref, g_ref, w_ref, o_ref):
    xn = _rms_rows(x_ref[...], g_ref[...]).astype(BF16)
    o_ref[...] = _dot(xn, w_ref[...]).astype(o_ref.dtype)


def _mem_kv(mem, g, w, l):
    m = mem.shape[0]
    n = w.shape[2]
    return pl.pallas_call(
        _norm_matmul_kernel,
        grid=(1,),
        in_specs=[
            pl.BlockSpec((m, D_MODEL), lambda i: (0, 0)),
            pl.BlockSpec((1, D_MODEL), lambda i: (0, 0)),
            pl.BlockSpec((None, D_MODEL, n), lambda i: (l, 0, 0)),
        ],
        out_specs=pl.BlockSpec((m, n), lambda i: (0, 0)),
        out_shape=jax.ShapeDtypeStruct((m, n), BF16),
        compiler_params=_cparams(("arbitrary",)),
        name="mem_kv",
    )(mem, g, w)


def _xattn_kernel(h_ref, g_ref, kv_ref, wq_ref, wo_ref, o_ref):
    h = h_ref[...]
    hn = _rms_rows(h, g_ref[...]).astype(BF16)
    q = _dot(hn, wq_ref[...]).astype(BF16)
    outs = []
    for hd in range(X_HEADS):
        cs = slice(hd * X_DH, (hd + 1) * X_DH)
        kh = kv_ref[:, cs]
        vh = kv_ref[:, X_WIDTH + hd * X_DH:X_WIDTH + (hd + 1) * X_DH]
        sc = _dot_nt(q[:, cs], kh) * (X_DH ** -0.5)
        mx = jnp.max(sc, axis=-1, keepdims=True)
        p = jnp.exp(sc - mx)
        den = jnp.sum(p, axis=-1, keepdims=True)
        outs.append((_dot(p.astype(BF16), vh) / den).astype(BF16))
    o = jnp.concatenate(outs, axis=1)
    o_ref[...] = h + _dot(o, wo_ref[...])


def _xattn(h, g, kv, w_q, w_o, l, tm):
    s = h.shape[0]
    const = lambda a: pl.BlockSpec(a.shape, lambda i: (0, 0))
    layer = lambda a: pl.BlockSpec((None,) + a.shape[1:], lambda i: (l, 0, 0))
    return pl.pallas_call(
        _xattn_kernel,
        grid=(s // tm,),
        in_specs=[pl.BlockSpec((tm, D_MODEL), lambda i: (i, 0)), const(g), const(kv), layer(w_q), layer(w_o)],
        out_specs=pl.BlockSpec((tm, D_MODEL), lambda i: (i, 0)),
        out_shape=jax.ShapeDtypeStruct((s, D_MODEL), F32),
        compiler_params=_cparams(("parallel",)),
        name="xattn",
    )(h, g, kv, w_q, w_o)


def _ffn_kernel(h_ref, g_ref, wg_ref, wv_ref, cg_ref, cv_ref, wd_ref, gout_ref, o_ref, hn_ref, gbuf, vbuf, carry,
                *, norm_out):
    i = pl.program_id(0)
    j = pl.program_id(1)
    tm = h_ref.shape[0]
    halo = SUBLANES

    @pl.when(j == 0)
    def _():
        h = h_ref[...]
        hn_ref[...] = _rms_rows(h, g_ref[...]).astype(BF16)
        o_ref[...] = h

    @pl.when(i == 0)
    def _():
        carry[j] = jnp.zeros(carry.shape[1:], F32)

    hn = hn_ref[...]
    gbuf[0:halo, :] = carry[j, 0]
    vbuf[0:halo, :] = carry[j, 1]

    gbuf[halo:, :] = _dot(hn, wg_ref[...])
    vbuf[halo:, :] = _dot(hn, wv_ref[...])
    carry[j, 0] = gbuf[tm:tm + halo, :]
    carry[j, 1] = vbuf[tm:tm + halo, :]

    def conv(buf, cw_ref):
        base = halo - (FFN_CONV - 1)
        out = cw_ref[0:1, :] * buf[base:base + tm, :]
        for tap in range(1, FFN_CONV):
            out = out + cw_ref[tap:tap + 1, :] * buf[base + tap:base + tap + tm, :]
        return out

    gate = conv(gbuf, cg_ref)
    val = conv(vbuf, cv_ref)
    act = (gate * _sigmoid(gate) * val).astype(BF16)
    o_ref[...] += _dot(act, wd_ref[...])

    if norm_out:
        @pl.when(j == pl.num_programs(1) - 1)
        def _():
            o_ref[...] = _rms_rows(o_ref[...], gout_ref[...])


def _ffn(h, g, w_up, conv_w, w_down, g_out, l, tm, tn, norm_out):
    s = h.shape[0]
    nj = D_FF // tn
    return pl.pallas_call(
        functools.partial(_ffn_kernel, norm_out=norm_out),
        grid=(s // tm, nj),
        in_specs=[
            pl.BlockSpec((tm, D_MODEL), lambda i, j: (i, 0)),
            pl.BlockSpec((1, D_MODEL), lambda i, j: (0, 0)),
            pl.BlockSpec((None, D_MODEL, tn), lambda i, j: (l, 0, j)),
            pl.BlockSpec((None, D_MODEL, tn), lambda i, j: (l, 0, j + nj)),
            pl.BlockSpec((FFN_CONV, tn), lambda i, j: (0, j)),
            pl.BlockSpec((FFN_CONV, tn), lambda i, j: (0, j + nj)),
            pl.BlockSpec((None, tn, D_MODEL), lambda i, j: (l, j, 0)),
            pl.BlockSpec((1, D_MODEL), lambda i, j: (0, 0)),
        ],
        out_specs=pl.BlockSpec((tm, D_MODEL), lambda i, j: (i, 0)),
        out_shape=jax.ShapeDtypeStruct((s, D_MODEL), F32),
        scratch_shapes=[
            pltpu.VMEM((tm, D_MODEL), BF16),
            pltpu.VMEM((tm + SUBLANES, tn), F32),
            pltpu.VMEM((tm + SUBLANES, tn), F32),
            pltpu.VMEM((nj, 2, SUBLANES, tn), F32),
        ],
        compiler_params=_cparams(("arbitrary", "arbitrary")),
        name="conv_ffn",
    )(h, g, w_up, w_up, conv_w, conv_w, w_down, g_out)


def _tile(s, want):
    return min(s, want)


def _w_in_pieces():
    o_z = 2 * DN_QK_WIDTH + DN_WIDTH
    o_a = o_z + DN_WIDTH
    o_swq = o_a + 2 * DN_HEADS
    o_swkv = o_swq + SWA_WIDTH
    o_sb = o_swkv + 2 * SWA_KV_WIDTH
    o_gate = o_sb + 3 * SB_WIDTH
    o_end = o_gate + N_BRANCH * D_MODEL
    big = [(o_gate, o_end), (0, o_a), (o_swq, o_swkv), (o_sb, o_gate), (o_swkv, o_sb)]
    return big, (o_a, o_swq), o_end


def _reorder_w_in(w_in):
    pieces, (a0, a1), _ = _w_in_pieces()
    wt = jnp.swapaxes(w_in, 1, 2)
    big = jnp.concatenate([wt[:, lo:hi].astype(BF16) for lo, hi in pieces], axis=1)
    ab = jnp.pad(wt[:, a0:a1], ((0, 0), (0, AB_COLS - (a1 - a0)), (0, 0))).astype(BF16)
    return big, ab


def kernel(x, mem, norm_mix, w_in, dn_conv, dn_a_log, dn_dt_bias, dn_norm, swa_sinks, w_br_dn, w_br_swa, w_br_sb, w_o, norm_xattn, norm_mem, w_xq, w_xkv, w_xo, norm_ffn, w_up, ffn_conv, w_down, norm_final):
    depth = w_in.shape[0]
    s = x.shape[1]
    assert x.shape[0] == 1 and s % 256 == 0
    h = x[0]
    mem2 = mem[0]

    w_big, w_ab = _reorder_w_in(w_in)
    w_br_dn_b = w_br_dn.astype(BF16)
    w_br_swa_b = w_br_swa.astype(BF16)
    w_br_sb_b = w_br_sb.astype(BF16)
    w_o_b = w_o.astype(BF16)
    w_xq_b = w_xq.astype(BF16)
    w_xkv_b = w_xkv.astype(BF16)
    w_xo_b = w_xo.astype(BF16)
    w_up_b = w_up.astype(BF16)
    w_down_b = w_down.astype(BF16)
    pad8 = lambda a: jnp.pad(a, ((0, 0), (0, AB_COLS - a.shape[-1])))
    a_log_pad = pad8(dn_a_log)
    dt_bias_pad = pad8(dn_dt_bias)
    swa_bias = _swa_bias()

    for l in range(depth):
        proj, ab = _in_proj(h, norm_mix[l][None], w_big, w_ab, l, tm=_tile(s, 1024), tn=1792)
        y_dn = _delta_net(proj, ab, dn_conv[l], a_log_pad[l][None], dt_bias_pad[l][None], dn_norm[l][None],
                          t=_tile(s, 512))
        y_swa = _swa(proj, swa_sinks[l], swa_bias, tq=_tile(s, 512))
        y_sb = _stick_breaking(proj, tq=_tile(s, 256))
        h = _merge(h, proj, y_dn, y_swa, y_sb, w_br_dn_b, w_br_swa_b, w_br_sb_b, w_o_b, l, tm=_tile(s, 256))
        kv = _mem_kv(mem2, norm_mem[l][None], w_xkv_b, l)
        h = _xattn(h, norm_xattn[l][None], kv, w_xq_b, w_xo_b, l, tm=_tile(s, 1024))
        h = _ffn(h, norm_ffn[l][None], w_up_b, ffn_conv[l], w_down_b, norm_final[None], l, tm=_tile(s, 512),
                 tn=1024, norm_out=(l == depth - 1))
    return h[None]
```

```python
import functools

import jax
import jax.numpy as jnp
from jax import lax
from jax.experimental import pallas as pl
from jax.experimental.pallas import tpu as pltpu

F32 = jnp.float32
BF16 = jnp.bfloat16

D_MODEL = 2048
EPS = 1e-6
DN_HEADS = 8
DN_DK = 128
DN_DV = 128
DN_CONV = 4
DN_BLOCK = 128
SWA_HEADS = 8
SWA_KV_HEADS = 2
SWA_DH = 64
WINDOW = 128
SB_HEADS = 4
SB_DH = 128
X_HEADS = 4
X_DH = 128
D_FF = 4096
FFN_CONV = 3
N_BRANCH = 3

DN_QK_WIDTH = DN_HEADS * DN_DK
DN_WIDTH = DN_HEADS * DN_DV
SWA_WIDTH = SWA_HEADS * SWA_DH
SWA_KV_WIDTH = SWA_KV_HEADS * SWA_DH
SB_WIDTH = SB_HEADS * SB_DH
X_WIDTH = X_HEADS * X_DH

LANES = 128
SUBLANES = 8
VMEM_LIMIT = 56 * 1024 * 1024

PROJ_COLS = N_BRANCH * D_MODEL + 3 * DN_QK_WIDTH + DN_WIDTH + SWA_WIDTH + 3 * SB_WIDTH + 2 * SWA_KV_WIDTH
COL_DN_Q = N_BRANCH * D_MODEL
COL_DN_Z = COL_DN_Q + 3 * DN_QK_WIDTH
COL_SWA_Q = COL_DN_Z + DN_WIDTH
COL_SB_Q = COL_SWA_Q + SWA_WIDTH
COL_SWA_KV = COL_SB_Q + 3 * SB_WIDTH
AB_COLS = LANES

NEG_BIG = -1e30
LOG2E = 1.4426950408889634
SB_SKIP_LOG2 = 170.0
SB_NORM_ROWS = 1024


def _cparams(semantics):
    return pltpu.CompilerParams(dimension_semantics=semantics, vmem_limit_bytes=VMEM_LIMIT)


def _dot(a, b):
    return jnp.dot(a, b, preferred_element_type=F32)


def _dot_nt(a, b):
    return lax.dot_general(a, b, (((1,), (1,)), ((), ())), preferred_element_type=F32)


def _split3(x):
    x1 = x.astype(BF16)
    r1 = x - x1.astype(F32)
    x2 = r1.astype(BF16)
    x3 = (r1 - x2.astype(F32)).astype(BF16)
    return x1, x2, x3


def _dot_01_lhs(m01, x):
    x1, x2, x3 = _split3(x)
    return _dot(m01, x1) + _dot(m01, x2) + _dot(m01, x3)


def _dot_01_rhs(x, m01):
    x1, x2, x3 = _split3(x)
    return _dot(x1, m01) + _dot(x2, m01) + _dot(x3, m01)


def _dot_solve(a, b):
    return jnp.dot(a.astype(BF16), b.astype(BF16), preferred_element_type=F32)


def _sigmoid(x):
    return 1.0 / (1.0 + jnp.exp(-x))


def _softplus(x):
    return jnp.maximum(x, 0.0) + jnp.log1p(jnp.exp(-jnp.abs(x)))


def _rms_rows(x, g):
    ms = jnp.mean(x * x, axis=-1, keepdims=True)
    return x * lax.rsqrt(ms + EPS) * g


def _in_proj_kernel(h_ref, g_ref, wt_ref, wabt_ref, o_ref, ab_ref, xn_ref):
    @pl.when(pl.program_id(1) == 0)
    def _():
        xn = _rms_rows(h_ref[...], g_ref[...]).astype(BF16)
        xn_ref[...] = xn
        ab_ref[...] = _dot_nt(xn, wabt_ref[...])

    o_ref[...] = _dot_nt(xn_ref[...], wt_ref[...]).astype(o_ref.dtype)


def _in_proj(h, g, w, wab, l, tm, tn):
    s = h.shape[0]
    n = w.shape[1]
    return pl.pallas_call(
        _in_proj_kernel,
        grid=(s // tm, n // tn),
        in_specs=[
            pl.BlockSpec((tm, D_MODEL), lambda i, j: (i, 0)),
            pl.BlockSpec((1, D_MODEL), lambda i, j: (0, 0)),
            pl.BlockSpec((None, tn, D_MODEL), lambda i, j: (l, j, 0)),
            pl.BlockSpec((None, AB_COLS, D_MODEL), lambda i, j: (l, 0, 0)),
        ],
        out_specs=[
            pl.BlockSpec((tm, tn), lambda i, j: (i, j)),
            pl.BlockSpec((tm, AB_COLS), lambda i, j: (i, 0)),
        ],
        out_shape=[jax.ShapeDtypeStruct((s, n), BF16), jax.ShapeDtypeStruct((s, AB_COLS), F32)],
        scratch_shapes=[pltpu.VMEM((tm, D_MODEL), BF16)],
        compiler_params=_cparams(("parallel", "arbitrary")),
        name="in_proj",
    )(h, g, w, wab)


def _dn_kernel(q_ref, k_ref, v_ref, z_ref, ab_ref, cw_ref, alog_ref, dtb_ref, nrm_ref, cum_ref, eg_ref, eb_ref,
               o_ref, xbuf, qs, ks, vs, gce, gle, ble, gct, nb, qb, qkb, rbv, rbk, ub, wb, qdb, kdt, state):
    i = pl.program_id(0)
    t = q_ref.shape[0]
    c = DN_BLOCK
    halo = SUBLANES

    @pl.when(i == 0)
    def _():
        xbuf[0:halo, :] = jnp.zeros((halo, xbuf.shape[1]), F32)
        state[...] = jnp.zeros(state.shape, F32)

    @pl.when(i > 0)
    def _():
        xbuf[0:halo, :] = xbuf[t:t + halo, :]

    xbuf[halo:halo + t, 0:DN_QK_WIDTH] = q_ref[...].astype(F32)
    xbuf[halo:halo + t, DN_QK_WIDTH:2 * DN_QK_WIDTH] = k_ref[...].astype(F32)
    xbuf[halo:halo + t, 2 * DN_QK_WIDTH:] = v_ref[...].astype(F32)

    n_slabs = (2 * DN_QK_WIDTH + DN_WIDTH) // LANES
    for s in range(n_slabs):
        cs = slice(s * LANES, (s + 1) * LANES)
        acc = cw_ref[0:1, cs] * xbuf[halo - 3:halo - 3 + t, cs]
        for tap in range(1, DN_CONV):
            acc = acc + cw_ref[tap:tap + 1, cs] * xbuf[halo - 3 + tap:halo - 3 + tap + t, cs]
        y = acc * _sigmoid(acc)
        if s < 2 * DN_HEADS:
            y = y * lax.rsqrt(jnp.sum(y * y, axis=-1, keepdims=True) + EPS)
        if s < DN_HEADS:
            qs[:, cs] = y * (DN_DK ** -0.5)
        elif s < 2 * DN_HEADS:
            ks[:, (s - DN_HEADS) * LANES:(s - DN_HEADS + 1) * LANES] = y
        else:
            vs[:, (s - 2 * DN_HEADS) * LANES:(s - 2 * DN_HEADS + 1) * LANES] = y

    ab = ab_ref[...]
    g = -jnp.exp(alog_ref[...]) * _softplus(ab + dtb_ref[...])
    beta = _sigmoid(ab)
    cums = _dot_01_lhs(cum_ref[...], g)
    gce[...] = _dot_01_rhs(cums[0:t], eg_ref[...])
    gle[...] = _dot_01_rhs(cums[t:2 * t], eg_ref[...])
    ble[...] = _dot_01_rhs(beta, eb_ref[...])
    gct[...] = cums[0:t].T

    row = lax.broadcasted_iota(jnp.int32, (c, c), 0)
    col = lax.broadcasted_iota(jnp.int32, (c, c), 1)
    causal = row >= col
    strict = row > col
    pair = row // 2 == col // 2
    nrm = nrm_ref[...]
    items = [(ci, hd) for ci in range(t // c) for hd in range(DN_HEADS)]

    for it, (ci, hd) in enumerate(items):
        rows = slice(ci * c, (ci + 1) * c)
        cs = slice(hd * LANES, (hd + 1) * LANES)
        k_ = ks[rows, cs]
        gcc = gce[rows, cs]
        g_row = gct[hd:hd + 1, rows]
        dm = jnp.where(causal, gcc - g_row, 0.0)
        decay = jnp.where(causal, jnp.exp(dm), 0.0)
        lhs = jnp.concatenate([k_ * ble[rows, cs], qs[rows, cs]], axis=0).astype(BF16)
        kq = _dot_nt(lhs, k_.astype(BF16))
        nmat = jnp.where(strict, kq[0:c] * decay, 0.0)
        nb[it] = nmat
        qb[it] = jnp.where(pair, -nmat, 0.0)
        qkb[it] = (kq[c:] * decay).astype(BF16)

    for it, (ci, hd) in enumerate(items):
        rows = slice(ci * c, (ci + 1) * c)
        cs = slice(hd * LANES, (hd + 1) * LANES)
        q_ = qs[rows, cs]
        k_ = ks[rows, cs]
        gcc = gce[rows, cs]
        bet = ble[rows, cs]
        egc = jnp.exp(gcc)
        rbv[rows, cs] = vs[rows, cs] * bet
        rbk[rows, cs] = k_ * bet * egc
        qdb[rows, cs] = (q_ * egc).astype(BF16)
        kdt[it] = (k_ * jnp.exp(gle[rows, cs] - gcc)).T.astype(BF16)

    size = 2
    while size < c:
        join = (row // (2 * size) == col // (2 * size)) & (row // size != col // size)
        for it in range(len(items)):
            qm = qb[it]
            lo = jnp.where(join, nb[it], 0.0)
            a = lo + _dot_solve(qm, lo)
            qb[it] = qm - a - _dot_solve(a, qm)
        size *= 2

    for it, (ci, hd) in enumerate(items):
        rows = slice(ci * c, (ci + 1) * c)
        cs = slice(hd * LANES, (hd + 1) * LANES)
        rhs = jnp.concatenate([rbv[rows, cs], rbk[rows, cs]], axis=1)
        uw = rhs + _dot_solve(qb[it], rhs)
        ub[rows, cs] = uw[:, 0:DN_DV]
        wb[rows, cs] = uw[:, DN_DV:].astype(BF16)

    for it, (ci, hd) in enumerate(items):
        rows = slice(ci * c, (ci + 1) * c)
        cs = slice(hd * LANES, (hd + 1) * LANES)
        st = state[hd]
        ws = _dot(jnp.concatenate([wb[rows, cs], qdb[rows, cs]], axis=0), st.astype(BF16))
        vnb = (ub[rows, cs] - ws[0:c]).astype(BF16)
        kv = _dot(jnp.concatenate([qkb[it], kdt[it]], axis=0), vnb)
        o = ws[c:] + kv[0:c]
        state[hd] = st * jnp.exp(gle[rows, cs]) + kv[c:]
        zz = z_ref[rows, cs].astype(F32)
        y = _rms_rows(o, nrm) * (zz * _sigmoid(zz))
        o_ref[rows, cs] = y.astype(o_ref.dtype)


def _dn_constants(t):
    r = jnp.arange(t)
    same = (r[:, None] // DN_BLOCK) == (r[None, :] // DN_BLOCK)
    lower = r[:, None] >= r[None, :]
    cum = jnp.concatenate([(same & lower).astype(BF16), same.astype(BF16)], axis=0)
    lane = jnp.arange(LANES)[:, None]
    head = jnp.arange(DN_WIDTH)[None, :] // DN_DV
    eg = (lane == head).astype(BF16)
    eb = (lane == head + DN_HEADS).astype(BF16)
    return cum, eg, eb


def _delta_net(proj, ab, conv_w, a_log_pad, dt_bias_pad, dn_norm, t):
    s = proj.shape[0]
    cum, eg, eb = _dn_constants(t)
    cb = COL_DN_Q // DN_QK_WIDTH
    n_items = (t // DN_BLOCK) * DN_HEADS
    const = lambda shape: pl.BlockSpec(shape, lambda i: (0, 0))
    return pl.pallas_call(
        _dn_kernel,
        grid=(s // t,),
        in_specs=[
            pl.BlockSpec((t, DN_QK_WIDTH), lambda i: (i, cb)),
            pl.BlockSpec((t, DN_QK_WIDTH), lambda i: (i, cb + 1)),
            pl.BlockSpec((t, DN_WIDTH), lambda i: (i, cb + 2)),
            pl.BlockSpec((t, DN_WIDTH), lambda i: (i, cb + 3)),
            pl.BlockSpec((t, AB_COLS), lambda i: (i, 0)),
            const((DN_CONV, 2 * DN_QK_WIDTH + DN_WIDTH)),
            const((1, AB_COLS)),
            const((1, AB_COLS)),
            const((1, DN_DV)),
            const((2 * t, t)),
            const((LANES, DN_WIDTH)),
            const((LANES, DN_WIDTH)),
        ],
        out_specs=pl.BlockSpec((t, DN_WIDTH), lambda i: (i, 0)),
        out_shape=jax.ShapeDtypeStruct((s, DN_WIDTH), BF16),
        scratch_shapes=[
            pltpu.VMEM((t + SUBLANES, 2 * DN_QK_WIDTH + DN_WIDTH), F32),
            pltpu.VMEM((t, DN_QK_WIDTH), F32),
            pltpu.VMEM((t, DN_QK_WIDTH), F32),
            pltpu.VMEM((t, DN_WIDTH), F32),
            pltpu.VMEM((t, DN_WIDTH), F32),
            pltpu.VMEM((t, DN_WIDTH), F32),
            pltpu.VMEM((t, DN_WIDTH), F32),
            pltpu.VMEM((LANES, t), F32),
            pltpu.VMEM((n_items, DN_BLOCK, DN_BLOCK), F32),
            pltpu.VMEM((n_items, DN_BLOCK, DN_BLOCK), F32),
            pltpu.VMEM((n_items, DN_BLOCK, DN_BLOCK), BF16),
            pltpu.VMEM((t, DN_WIDTH), F32),
            pltpu.VMEM((t, DN_QK_WIDTH), F32),
            pltpu.VMEM((t, DN_WIDTH), F32),
            pltpu.VMEM((t, DN_QK_WIDTH), BF16),
            pltpu.VMEM((t, DN_QK_WIDTH), BF16),
            pltpu.VMEM((n_items, DN_DK, DN_BLOCK), BF16),
            pltpu.VMEM((DN_HEADS, DN_DK, DN_DV), F32),
        ],
        compiler_params=_cparams(("arbitrary",)),
        name="delta_net",
    )(proj, proj, proj, proj, ab, conv_w, a_log_pad, dt_bias_pad, dn_norm, cum, eg, eb)


def _swa_kernel(sinks_ref, q_ref, kv_ref, kvp_ref, bias_ref, o_ref, scb, pb, denb):
    i = pl.program_id(0)
    tq = q_ref.shape[0]
    w = WINDOW
    nwb = tq // w
    grp = SWA_HEADS // SWA_KV_HEADS
    ext = jnp.concatenate([kvp_ref[...], kv_ref[...]], axis=0)
    kj = lax.broadcasted_iota(jnp.int32, (w, 2 * w), 1)
    no_prev = jnp.where(jnp.logical_and(kj < w, i == 0), NEG_BIG, 0.0)
    for wb in range(nwb):
        win = ext[wb * w:wb * w + 2 * w]
        qwb = q_ref[wb * w:(wb + 1) * w, :]
        for hd in range(SWA_HEADS):
            kh = win[:, (hd // grp) * SWA_DH:(hd // grp + 1) * SWA_DH]
            sc = _dot_nt(qwb[:, hd * SWA_DH:(hd + 1) * SWA_DH], kh) * (SWA_DH ** -0.5) + bias_ref[hd]
            if wb == 0:
                sc = sc + no_prev
            scb[hd] = sc
        for hd in range(SWA_HEADS):
            sc = scb[hd]
            sink = sinks_ref[hd]
            mx = jnp.maximum(jnp.max(sc, axis=-1, keepdims=True), sink)
            p = jnp.exp(sc - mx)
            pb[hd] = p.astype(BF16)
            den = jnp.sum(p, axis=-1, keepdims=True) + jnp.exp(sink - mx)
            denb[hd] = jnp.broadcast_to(den, (w, SWA_DH))
        outs = []
        for hd in range(SWA_HEADS):
            hk = hd // grp
            vh = win[:, SWA_KV_WIDTH + hk * SWA_DH:SWA_KV_WIDTH + (hk + 1) * SWA_DH]
            outs.append(_dot(pb[hd], vh) / denb[hd])
        o_ref[wb * w:(wb + 1) * w, :] = jnp.concatenate(outs, axis=1).astype(o_ref.dtype)


def _swa_bias():
    qi = jnp.arange(WINDOW)[:, None]
    kj = jnp.arange(2 * WINDOW)[None, :]
    dist = qi + WINDOW - kj
    slopes = jnp.exp2(-8.0 * (jnp.arange(SWA_HEADS, dtype=F32) + 1.0) / SWA_HEADS)
    band = (dist >= 0) & (dist < WINDOW)
    return jnp.where(band[None], -slopes[:, None, None] * dist.astype(F32)[None], NEG_BIG)


def _swa(proj, sinks, bias, tq):
    s = proj.shape[0]
    nwb = tq // WINDOW
    qb = COL_SWA_Q // SWA_WIDTH
    kvb = COL_SWA_KV // (2 * SWA_KV_WIDTH)
    return pl.pallas_call(
        _swa_kernel,
        grid=(s // tq,),
        in_specs=[
            pl.BlockSpec(memory_space=pltpu.SMEM),
            pl.BlockSpec((tq, SWA_WIDTH), lambda i: (i, qb)),
            pl.BlockSpec((tq, 2 * SWA_KV_WIDTH), lambda i: (i, kvb)),
            pl.BlockSpec((WINDOW, 2 * SWA_KV_WIDTH), lambda i: (jnp.maximum(i * nwb - 1, 0), kvb)),
            pl.BlockSpec((SWA_HEADS, WINDOW, 2 * WINDOW), lambda i: (0, 0, 0)),
        ],
        out_specs=pl.BlockSpec((tq, SWA_WIDTH), lambda i: (i, 0)),
        out_shape=jax.ShapeDtypeStruct((s, SWA_WIDTH), BF16),
        scratch_shapes=[
            pltpu.VMEM((SWA_HEADS, WINDOW, 2 * WINDOW), F32),
            pltpu.VMEM((SWA_HEADS, WINDOW, 2 * WINDOW), BF16),
            pltpu.VMEM((SWA_HEADS, WINDOW, SWA_DH), F32),
        ],
        compiler_params=_cparams(("parallel",)),
        name="swa",
    )(sinks, proj, proj, proj, bias)


def _sb_knorm_kernel(k_ref, o_ref, run_ref, *, tk):
    @pl.when(pl.program_id(0) == 0)
    def _():
        run_ref[...] = jnp.zeros(run_ref.shape, F32)

    run = run_ref[...]
    for sub in range(k_ref.shape[0] // tk):
        k = k_ref[sub * tk:(sub + 1) * tk, :].astype(F32)
        cur = []
        for hd in range(SB_HEADS):
            kh = k[:, hd * SB_DH:(hd + 1) * SB_DH]
            norm = jnp.sqrt(jnp.sum(kh * kh, axis=-1, keepdims=True))
            cur.append(jnp.broadcast_to(jnp.max(norm, axis=0, keepdims=True), (1, SB_DH)))
        run = jnp.maximum(run, jnp.concatenate(cur, axis=1))
        o_ref[sub] = run
    run_ref[...] = run


def _sb_key_norms(proj, tk):
    s = proj.shape[0]
    kb = COL_SB_Q // SB_WIDTH + 1
    rows = min(s, SB_NORM_ROWS)
    return pl.pallas_call(
        functools.partial(_sb_knorm_kernel, tk=tk),
        grid=(s // rows,),
        in_specs=[pl.BlockSpec((rows, SB_WIDTH), lambda t: (t, kb))],
        out_specs=pl.BlockSpec((rows // tk, 1, SB_WIDTH), lambda t: (t, 0, 0)),
        out_shape=jax.ShapeDtypeStruct((s // tk, 1, SB_WIDTH), F32),
        scratch_shapes=[pltpu.VMEM((1, SB_WIDTH), F32)],
        compiler_params=_cparams(("arbitrary",)),
        name="sb_key_norms",
    )(proj)


def _sb_kernel(kmax_ref, q_ref, k_ref, v_ref, ntri_ref, o_ref, zbuf, spb, abuf, acc_ref, aft_ref, qn_ref):
    i = pl.program_id(0)
    tq = q_ref.shape[0]
    tk = tq
    blk = LANES
    nsub = tk // blk
    c2 = (SB_DH ** -0.5) * LOG2E
    ntri = ntri_ref[...]
    acc_ref[...] = jnp.zeros(acc_ref.shape, F32)
    aft_ref[...] = jnp.zeros(aft_ref.shape, F32)
    hcols = [slice(hd * SB_DH, (hd + 1) * SB_DH) for hd in range(SB_HEADS)]
    for hd, hc in enumerate(hcols):
        qf = q_ref[:, hc].astype(F32)
        qn = jnp.sqrt(jnp.sum(qf * qf, axis=-1, keepdims=True)) * c2
        qn_ref[hd] = jnp.broadcast_to(qn, (tq, LANES))

    def tile(k0, mask):
        keys = pl.ds(k0, tk)
        for hd, hc in enumerate(hcols):
            zbuf[hd] = _dot_nt(q_ref[:, hc], k_ref[keys, hc]) * c2
        for hd in range(SB_HEADS):
            z2 = zbuf[hd]
            sp2 = jnp.maximum(z2, 0.0) + jnp.log(1.0 + jnp.exp2(-jnp.abs(z2))) * LOG2E
            if mask is not None:
                sp2 = jnp.where(mask, sp2, 0.0)
            spb[hd] = sp2.astype(BF16)
        for hd in range(SB_HEADS):
            aft = aft_ref[hd]
            revs = [None] * nsub
            for sb in reversed(range(nsub)):
                cum = _dot(spb[hd, :, sb * blk:(sb + 1) * blk], ntri)
                revs[sb] = cum[:, 0:blk] + aft
                aft = aft + cum[:, blk:]
            a = jnp.exp2(zbuf[hd] + jnp.concatenate(revs, axis=1))
            if mask is not None:
                a = jnp.where(mask, a, 0.0)
            abuf[hd] = a.astype(BF16)
            aft_ref[hd] = aft
        for hd, hc in enumerate(hcols):
            acc_ref[hd] += _dot(abuf[hd], v_ref[keys, hc])

    def live(kt):
        kmax = kmax_ref[jnp.maximum(kt, 0)]
        bound = None
        for hd, hc in enumerate(hcols):
            b = qn_ref[hd] * kmax[:, hc] + aft_ref[hd]
            bound = b if bound is None else jnp.maximum(bound, b)
        return jnp.max(bound) > -SB_SKIP_LOG2

    row = lax.broadcasted_iota(jnp.int32, (tq, tk), 0)
    col = lax.broadcasted_iota(jnp.int32, (tq, tk), 1)
    tile(pl.multiple_of(i * tq, tq), row > col)

    def cond(carry):
        n, go = carry
        return jnp.logical_and(n < i, go)

    def body(carry):
        n, _ = carry
        kt = i - 1 - n
        tile(pl.multiple_of(kt * tk, tk), None)
        return n + 1, live(kt - 1)

    lax.while_loop(cond, body, (jnp.int32(0), live(i - 1)))
    for hd, hc in enumerate(hcols):
        o_ref[:, hc] = acc_ref[hd].astype(o_ref.dtype)


def _stick_breaking(proj, tq):
    s = proj.shape[0]
    qb = COL_SB_Q // SB_WIDTH
    r = jnp.arange(LANES)
    ntri = -jnp.concatenate([(r[:, None] >= r[None, :]).astype(BF16), jnp.ones((LANES, LANES), BF16)], axis=1)
    kmax = _sb_key_norms(proj, tq)
    resident = lambda col: pl.BlockSpec((s, SB_WIDTH), lambda i: (0, col), pipeline_mode=pl.Buffered(1))
    return pl.pallas_call(
        _sb_kernel,
        grid=(s // tq,),
        in_specs=[
            pl.BlockSpec((s // tq, 1, SB_WIDTH), lambda i: (0, 0, 0)),
            pl.BlockSpec((tq, SB_WIDTH), lambda i: (i, qb)),
            resident(qb + 1),
            resident(qb + 2),
            pl.BlockSpec((LANES, 2 * LANES), lambda i: (0, 0)),
        ],
        out_specs=pl.BlockSpec((tq, SB_WIDTH), lambda i: (i, 0)),
        out_shape=jax.ShapeDtypeStruct((s, SB_WIDTH), BF16),
        scratch_shapes=[
            pltpu.VMEM((SB_HEADS, tq, tq), F32),
            pltpu.VMEM((SB_HEADS, tq, tq), BF16),
            pltpu.VMEM((SB_HEADS, tq, tq), BF16),
            pltpu.VMEM((SB_HEADS, tq, SB_DH), F32),
            pltpu.VMEM((SB_HEADS, tq, LANES), F32),
            pltpu.VMEM((SB_HEADS, tq, LANES), F32),
        ],
        compiler_params=_cparams(("parallel",)),
        name="stick_breaking",
    )(kmax, proj, proj, proj, ntri)


def _merge_kernel(h_ref, g0_ref, g1_ref, g2_ref, ydn_ref, yswa_ref, ysb_ref, wdn_ref, wswa_ref, wsb_ref, wo_ref,
                  o_ref):
    merged = _sigmoid(g0_ref[...].astype(F32)) * _dot(ydn_ref[...], wdn_ref[...])
    merged = merged + _sigmoid(g1_ref[...].astype(F32)) * _dot(yswa_ref[...], wswa_ref[...])
    merged = merged + _sigmoid(g2_ref[...].astype(F32)) * _dot(ysb_ref[...], wsb_ref[...])
    o_ref[...] = h_ref[...] + _dot(merged.astype(BF16), wo_ref[...])


def _merge(h, proj, y_dn, y_swa, y_sb, w_dn, w_swa, w_sb, w_o, l, tm):
    s = h.shape[0]
    rows = lambda width, col=0: pl.BlockSpec((tm, width), lambda i: (i, col))
    const = lambda a: pl.BlockSpec((None,) + a.shape[1:], lambda i: (l, 0, 0), pipeline_mode=pl.Buffered(1))
    return pl.pallas_call(
        _merge_kernel,
        grid=(s // tm,),
        in_specs=[
            rows(D_MODEL), rows(D_MODEL, 0), rows(D_MODEL, 1), rows(D_MODEL, 2),
            rows(DN_WIDTH), rows(SWA_WIDTH), rows(SB_WIDTH),
            const(w_dn), const(w_swa), const(w_sb), const(w_o),
        ],
        out_specs=rows(D_MODEL),
        out_shape=jax.ShapeDtypeStruct((s, D_MODEL), F32),
        compiler_params=_cparams(("parallel",)),
        name="merge",
    )(h, proj, proj, proj, y_dn, y_swa, y_sb, w_dn, w_swa, w_sb, w_o)


def _norm_matmul_kernel(x_ref, g_ref, w_ref, o_ref):
    xn = _rms_rows(x_ref[...], g_ref[...]).astype(BF16)
    o_ref[...] = _dot(xn, w_ref[...]).astype(o_ref.dtype)


def _mem_kv(mem, g, w, l):
    m = mem.shape[0]
    n = w.shape[2]
    return pl.pallas_call(
        _norm_matmul_kernel,
        grid=(1,),
        in_specs=[
            pl.BlockSpec((m, D_MODEL), lambda i: (0, 0)),
            pl.BlockSpec((1, D_MODEL), lambda i: (0, 0)),
            pl.BlockSpec((None, D_MODEL, n), lambda i: (l, 0, 0)),
        ],
        out_specs=pl.BlockSpec((m, n), lambda i: (0, 0)),
        out_shape=jax.ShapeDtypeStruct((m, n), BF16),
        compiler_params=_cparams(("arbitrary",)),
        name="mem_kv",
    )(mem, g, w)


def _xattn_kernel(h_ref, g_ref, kv_ref, wq_ref, wo_ref, o_ref):
    h = h_ref[...]
    hn = _rms_rows(h, g_ref[...]).astype(BF16)
    q = _dot(hn, wq_ref[...]).astype(BF16)
    outs = []
    for hd in range(X_HEADS):
        cs = slice(hd * X_DH, (hd + 1) * X_DH)
        kh = kv_ref[:, cs]
        vh = kv_ref[:, X_WIDTH + hd * X_DH:X_WIDTH + (hd + 1) * X_DH]
        sc = _dot_nt(q[:, cs], kh) * (X_DH ** -0.5)
        mx = jnp.max(sc, axis=-1, keepdims=True)
        p = jnp.exp(sc - mx)
        den = jnp.sum(p, axis=-1, keepdims=True)
        outs.append((_dot(p.astype(BF16), vh) / den).astype(BF16))
    o = jnp.concatenate(outs, axis=1)
    o_ref[...] = h + _dot(o, wo_ref[...])


def _xattn(h, g, kv, w_q, w_o, l, tm):
    s = h.shape[0]
    const = lambda a: pl.BlockSpec(a.shape, lambda i: (0, 0))
    layer = lambda a: pl.BlockSpec((None,) + a.shape[1:], lambda i: (l, 0, 0))
    return pl.pallas_call(
        _xattn_kernel,
        grid=(s // tm,),
        in_specs=[pl.BlockSpec((tm, D_MODEL), lambda i: (i, 0)), const(g), const(kv), layer(w_q), layer(w_o)],
        out_specs=pl.BlockSpec((tm, D_MODEL), lambda i: (i, 0)),
        out_shape=jax.ShapeDtypeStruct((s, D_MODEL), F32),
        compiler_params=_cparams(("parallel",)),
        name="xattn",
    )(h, g, kv, w_q, w_o)


def _ffn_kernel(h_ref, g_ref, wg_ref, wv_ref, cg_ref, cv_ref, wd_ref, gout_ref, o_ref, hn_ref, gbuf, vbuf, carry,
                *, norm_out):
    i = pl.program_id(0)
    j = pl.program_id(1)
    tm = h_ref.shape[0]
    halo = SUBLANES

    @pl.when(j == 0)
    def _():
        h = h_ref[...]
        hn_ref[...] = _rms_rows(h, g_ref[...]).astype(BF16)
        o_ref[...] = h

    @pl.when(i == 0)
    def _():
        carry[j] = jnp.zeros(carry.shape[1:], F32)

    hn = hn_ref[...]
    gbuf[0:halo, :] = carry[j, 0]
    vbuf[0:halo, :] = carry[j, 1]

    gbuf[halo:, :] = _dot(hn, wg_ref[...])
    vbuf[halo:, :] = _dot(hn, wv_ref[...])
    carry[j, 0] = gbuf[tm:tm + halo, :]
    carry[j, 1] = vbuf[tm:tm + halo, :]

    def conv(buf, cw_ref):
        base = halo - (FFN_CONV - 1)
        out = cw_ref[0:1, :] * buf[base:base + tm, :]
        for tap in range(1, FFN_CONV):
            out = out + cw_ref[tap:tap + 1, :] * buf[base + tap:base + tap + tm, :]
        return out

    gate = conv(gbuf, cg_ref)
    val = conv(vbuf, cv_ref)
    act = (gate * _sigmoid(gate) * val).astype(BF16)
    o_ref[...] += _dot(act, wd_ref[...])

    if norm_out:
        @pl.when(j == pl.num_programs(1) - 1)
        def _():
            o_ref[...] = _rms_rows(o_ref[...], gout_ref[...])


def _ffn(h, g, w_up, conv_w, w_down, g_out, l, tm, tn, norm_out):
    s = h.shape[0]
    nj = D_FF // tn
    return pl.pallas_call(
        functools.partial(_ffn_kernel, norm_out=norm_out),
        grid=(s // tm, nj),
        in_specs=[
            pl.BlockSpec((tm, D_MODEL), lambda i, j: (i, 0)),
            pl.BlockSpec((1, D_MODEL), lambda i, j: (0, 0)),
            pl.BlockSpec((None, D_MODEL, tn), lambda i, j: (l, 0, j)),
            pl.BlockSpec((None, D_MODEL, tn), lambda i, j: (l, 0, j + nj)),
            pl.BlockSpec((FFN_CONV, tn), lambda i, j: (0, j)),
            pl.BlockSpec((FFN_CONV, tn), lambda i, j: (0, j + nj)),
            pl.BlockSpec((None, tn, D_MODEL), lambda i, j: (l, j, 0)),
            pl.BlockSpec((1, D_MODEL), lambda i, j: (0, 0)),
        ],
        out_specs=pl.BlockSpec((tm, D_MODEL), lambda i, j: (i, 0)),
        out_shape=jax.ShapeDtypeStruct((s, D_MODEL), F32),
        scratch_shapes=[
            pltpu.VMEM((tm, D_MODEL), BF16),
            pltpu.VMEM((tm + SUBLANES, tn), F32),
            pltpu.VMEM((tm + SUBLANES, tn), F32),
            pltpu.VMEM((nj, 2, SUBLANES, tn), F32),
        ],
        compiler_params=_cparams(("arbitrary", "arbitrary")),
        name="conv_ffn",
    )(h, g, w_up, w_up, conv_w, conv_w, w_down, g_out)


def _tile(s, want):
    return min(s, want)


IN_PROJ_ROWS, IN_PROJ_COLS = 1024, 1792
DN_ROWS = 512
SWA_ROWS = 512
SB_ROWS = 256
MERGE_ROWS = 512
XATTN_ROWS = 1024
FFN_ROWS, FFN_COLS = 512, 1024


def _w_in_pieces():
    o_z = 2 * DN_QK_WIDTH + DN_WIDTH
    o_a = o_z + DN_WIDTH
    o_swq = o_a + 2 * DN_HEADS
    o_swkv = o_swq + SWA_WIDTH
    o_sb = o_swkv + 2 * SWA_KV_WIDTH
    o_gate = o_sb + 3 * SB_WIDTH
    o_end = o_gate + N_BRANCH * D_MODEL
    big = [(o_gate, o_end), (0, o_a), (o_swq, o_swkv), (o_sb, o_gate), (o_swkv, o_sb)]
    return big, (o_a, o_swq), o_end


def _reorder_w_in(w_in):
    pieces, (a0, a1), _ = _w_in_pieces()
    wt = jnp.swapaxes(w_in, 1, 2)
    big = jnp.concatenate([wt[:, lo:hi].astype(BF16) for lo, hi in pieces], axis=1)
    ab = jnp.pad(wt[:, a0:a1], ((0, 0), (0, AB_COLS - (a1 - a0)), (0, 0))).astype(BF16)
    return big, ab


def kernel(x, mem, norm_mix, w_in, dn_conv, dn_a_log, dn_dt_bias, dn_norm, swa_sinks, w_br_dn, w_br_swa, w_br_sb, w_o, norm_xattn, norm_mem, w_xq, w_xkv, w_xo, norm_ffn, w_up, ffn_conv, w_down, norm_final):
    depth = w_in.shape[0]
    s = x.shape[1]
    assert x.shape[0] == 1 and s % 256 == 0
    h = x[0]
    mem2 = mem[0]

    w_big, w_ab = _reorder_w_in(w_in)
    w_br_dn_b = w_br_dn.astype(BF16)
    w_br_swa_b = w_br_swa.astype(BF16)
    w_br_sb_b = w_br_sb.astype(BF16)
    w_o_b = w_o.astype(BF16)
    w_xq_b = w_xq.astype(BF16)
    w_xkv_b = w_xkv.astype(BF16)
    w_xo_b = w_xo.astype(BF16)
    w_up_b = w_up.astype(BF16)
    w_down_b = w_down.astype(BF16)
    pad8 = lambda a: jnp.pad(a, ((0, 0), (0, AB_COLS - a.shape[-1])))
    a_log_pad = pad8(dn_a_log)
    dt_bias_pad = pad8(dn_dt_bias)
    swa_bias = _swa_bias()

    for l in range(depth):
        proj, ab = _in_proj(h, norm_mix[l][None], w_big, w_ab, l, tm=_tile(s, IN_PROJ_ROWS), tn=IN_PROJ_COLS)
        y_dn = _delta_net(proj, ab, dn_conv[l], a_log_pad[l][None], dt_bias_pad[l][None], dn_norm[l][None],
                          t=_tile(s, DN_ROWS))
        y_swa = _swa(proj, swa_sinks[l], swa_bias, tq=_tile(s, SWA_ROWS))
        y_sb = _stick_breaking(proj, tq=_tile(s, SB_ROWS))
        h = _merge(h, proj, y_dn, y_swa, y_sb, w_br_dn_b, w_br_swa_b, w_br_sb_b, w_o_b, l,
                   tm=_tile(s, MERGE_ROWS))
        kv = _mem_kv(mem2, norm_mem[l][None], w_xkv_b, l)
        h = _xattn(h, norm_xattn[l][None], kv, w_xq_b, w_xo_b, l, tm=_tile(s, XATTN_ROWS))
        h = _ffn(h, norm_ffn[l][None], w_up_b, ffn_conv[l], w_down_b, norm_final[None], l,
                 tm=_tile(s, FFN_ROWS), tn=FFN_COLS, norm_out=(l == depth - 1))
    return h[None]
```

```python
import functools

import jax
import jax.numpy as jnp
from jax import lax
from jax.experimental import pallas as pl
from jax.experimental.pallas import tpu as pltpu

F32 = jnp.float32
BF16 = jnp.bfloat16

D_MODEL = 2048
EPS = 1e-6
DN_HEADS = 8
DN_DK = 128
DN_DV = 128
DN_CONV = 4
DN_BLOCK = 128
SWA_HEADS = 8
SWA_KV_HEADS = 2
SWA_DH = 64
WINDOW = 128
SB_HEADS = 4
SB_DH = 128
X_HEADS = 4
X_DH = 128
D_FF = 4096
FFN_CONV = 3
N_BRANCH = 3

DN_QK_WIDTH = DN_HEADS * DN_DK
DN_WIDTH = DN_HEADS * DN_DV
SWA_WIDTH = SWA_HEADS * SWA_DH
SWA_KV_WIDTH = SWA_KV_HEADS * SWA_DH
SB_WIDTH = SB_HEADS * SB_DH
X_WIDTH = X_HEADS * X_DH

LANES = 128
SUBLANES = 8
VMEM_LIMIT = 56 * 1024 * 1024

PROJ_COLS = N_BRANCH * D_MODEL + 3 * DN_QK_WIDTH + DN_WIDTH + SWA_WIDTH + 3 * SB_WIDTH + 2 * SWA_KV_WIDTH
COL_DN_Q = N_BRANCH * D_MODEL
COL_DN_Z = COL_DN_Q + 3 * DN_QK_WIDTH
COL_SWA_Q = COL_DN_Z + DN_WIDTH
COL_SB_Q = COL_SWA_Q + SWA_WIDTH
COL_SWA_KV = COL_SB_Q + 3 * SB_WIDTH
AB_COLS = LANES

NEG_BIG = -1e30
LOG2E = 1.4426950408889634
SB_SKIP_LOG2 = 170.0
SB_NORM_ROWS = 1024


def _cparams(semantics):
    return pltpu.CompilerParams(dimension_semantics=semantics, vmem_limit_bytes=VMEM_LIMIT)


def _dot(a, b):
    return jnp.dot(a, b, preferred_element_type=F32)


def _dot_nt(a, b):
    return lax.dot_general(a, b, (((1,), (1,)), ((), ())), preferred_element_type=F32)


def _split3(x):
    x1 = x.astype(BF16)
    r1 = x - x1.astype(F32)
    x2 = r1.astype(BF16)
    x3 = (r1 - x2.astype(F32)).astype(BF16)
    return x1, x2, x3


def _dot_01_lhs(m01, x):
    x1, x2, x3 = _split3(x)
    return _dot(m01, x1) + _dot(m01, x2) + _dot(m01, x3)


def _dot_01_rhs(x, m01):
    x1, x2, x3 = _split3(x)
    return _dot(x1, m01) + _dot(x2, m01) + _dot(x3, m01)


def _dot_solve(a, b):
    return jnp.dot(a.astype(BF16), b.astype(BF16), preferred_element_type=F32)


def _sigmoid(x):
    return 1.0 / (1.0 + jnp.exp(-x))


def _softplus(x):
    return jnp.maximum(x, 0.0) + jnp.log1p(jnp.exp(-jnp.abs(x)))


def _rms_rows(x, g):
    ms = jnp.mean(x * x, axis=-1, keepdims=True)
    return x * lax.rsqrt(ms + EPS) * g


def _in_proj_kernel(h_ref, g_ref, wt_ref, wabt_ref, o_ref, ab_ref, xn_ref):
    @pl.when(pl.program_id(1) == 0)
    def _():
        xn = _rms_rows(h_ref[...], g_ref[...]).astype(BF16)
        xn_ref[...] = xn
        ab_ref[...] = _dot_nt(xn, wabt_ref[...])

    o_ref[...] = _dot_nt(xn_ref[...], wt_ref[...]).astype(o_ref.dtype)


def _in_proj(h, g, w, wab, l, tm, tn):
    s = h.shape[0]
    n = w.shape[1]
    return pl.pallas_call(
        _in_proj_kernel,
        grid=(s // tm, n // tn),
        in_specs=[
            pl.BlockSpec((tm, D_MODEL), lambda i, j: (i, 0)),
            pl.BlockSpec((1, D_MODEL), lambda i, j: (0, 0)),
            pl.BlockSpec((None, tn, D_MODEL), lambda i, j: (l, j, 0)),
            pl.BlockSpec((None, AB_COLS, D_MODEL), lambda i, j: (l, 0, 0)),
        ],
        out_specs=[
            pl.BlockSpec((tm, tn), lambda i, j: (i, j)),
            pl.BlockSpec((tm, AB_COLS), lambda i, j: (i, 0)),
        ],
        out_shape=[jax.ShapeDtypeStruct((s, n), BF16), jax.ShapeDtypeStruct((s, AB_COLS), F32)],
        scratch_shapes=[pltpu.VMEM((tm, D_MODEL), BF16)],
        compiler_params=_cparams(("parallel", "arbitrary")),
        name="in_proj",
    )(h, g, w, wab)


def _dn_kernel(q_ref, k_ref, v_ref, z_ref, ab_ref, cw_ref, alog_ref, dtb_ref, nrm_ref, cum_ref, eg_ref, eb_ref,
               o_ref, xbuf, qs, ks, vs, gce, gle, ble, gct, nb, qb, qkb, rbv, rbk, ub, wb, qdb, kdt, state):
    i = pl.program_id(0)
    t = q_ref.shape[0]
    c = DN_BLOCK
    halo = SUBLANES

    @pl.when(i == 0)
    def _():
        xbuf[0:halo, :] = jnp.zeros((halo, xbuf.shape[1]), F32)
        state[...] = jnp.zeros(state.shape, F32)

    @pl.when(i > 0)
    def _():
        xbuf[0:halo, :] = xbuf[t:t + halo, :]

    xbuf[halo:halo + t, 0:DN_QK_WIDTH] = q_ref[...].astype(F32)
    xbuf[halo:halo + t, DN_QK_WIDTH:2 * DN_QK_WIDTH] = k_ref[...].astype(F32)
    xbuf[halo:halo + t, 2 * DN_QK_WIDTH:] = v_ref[...].astype(F32)

    n_slabs = (2 * DN_QK_WIDTH + DN_WIDTH) // LANES
    for s in range(n_slabs):
        cs = slice(s * LANES, (s + 1) * LANES)
        acc = cw_ref[0:1, cs] * xbuf[halo - 3:halo - 3 + t, cs]
        for tap in range(1, DN_CONV):
            acc = acc + cw_ref[tap:tap + 1, cs] * xbuf[halo - 3 + tap:halo - 3 + tap + t, cs]
        y = acc * _sigmoid(acc)
        if s < 2 * DN_HEADS:
            y = y * lax.rsqrt(jnp.sum(y * y, axis=-1, keepdims=True) + EPS)
        if s < DN_HEADS:
            qs[:, cs] = y * (DN_DK ** -0.5)
        elif s < 2 * DN_HEADS:
            ks[:, (s - DN_HEADS) * LANES:(s - DN_HEADS + 1) * LANES] = y
        else:
            vs[:, (s - 2 * DN_HEADS) * LANES:(s - 2 * DN_HEADS + 1) * LANES] = y

    ab = ab_ref[...]
    g = -jnp.exp(alog_ref[...]) * _softplus(ab + dtb_ref[...])
    beta = _sigmoid(ab)
    cums = _dot_01_lhs(cum_ref[...], g)
    gce[...] = _dot_01_rhs(cums[0:t], eg_ref[...])
    gle[...] = _dot_01_rhs(cums[t:2 * t], eg_ref[...])
    ble[...] = _dot_01_rhs(beta, eb_ref[...])
    gct[...] = cums[0:t].T

    row = lax.broadcasted_iota(jnp.int32, (c, c), 0)
    col = lax.broadcasted_iota(jnp.int32, (c, c), 1)
    causal = row >= col
    strict = row > col
    pair = row // 2 == col // 2
    nrm = nrm_ref[...]
    items = [(ci, hd) for ci in range(t // c) for hd in range(DN_HEADS)]

    for it, (ci, hd) in enumerate(items):
        rows = slice(ci * c, (ci + 1) * c)
        cs = slice(hd * LANES, (hd + 1) * LANES)
        k_ = ks[rows, cs]
        gcc = gce[rows, cs]
        g_row = gct[hd:hd + 1, rows]
        dm = jnp.where(causal, gcc - g_row, 0.0)
        decay = jnp.where(causal, jnp.exp(dm), 0.0)
        lhs = jnp.concatenate([k_ * ble[rows, cs], qs[rows, cs]], axis=0).astype(BF16)
        kq = _dot_nt(lhs, k_.astype(BF16))
        nmat = jnp.where(strict, kq[0:c] * decay, 0.0)
        nb[it] = nmat
        qb[it] = jnp.where(pair, -nmat, 0.0)
        qkb[it] = (kq[c:] * decay).astype(BF16)

    for it, (ci, hd) in enumerate(items):
        rows = slice(ci * c, (ci + 1) * c)
        cs = slice(hd * LANES, (hd + 1) * LANES)
        q_ = qs[rows, cs]
        k_ = ks[rows, cs]
        gcc = gce[rows, cs]
        bet = ble[rows, cs]
        egc = jnp.exp(gcc)
        rbv[rows, cs] = vs[rows, cs] * bet
        rbk[rows, cs] = k_ * bet * egc
        qdb[rows, cs] = (q_ * egc).astype(BF16)
        kdt[it] = (k_ * jnp.exp(gle[rows, cs] - gcc)).T.astype(BF16)

    size = 2
    while size < c:
        join = (row // (2 * size) == col // (2 * size)) & (row // size != col // size)
        for it in range(len(items)):
            qm = qb[it]
            lo = jnp.where(join, nb[it], 0.0)
            a = lo + _dot_solve(qm, lo)
            qb[it] = qm - a - _dot_solve(a, qm)
        size *= 2

    for it, (ci, hd) in enumerate(items):
        rows = slice(ci * c, (ci + 1) * c)
        cs = slice(hd * LANES, (hd + 1) * LANES)
        rhs = jnp.concatenate([rbv[rows, cs], rbk[rows, cs]], axis=1)
        uw = rhs + _dot_solve(qb[it], rhs)
        ub[rows, cs] = uw[:, 0:DN_DV]
        wb[rows, cs] = uw[:, DN_DV:].astype(BF16)

    for it, (ci, hd) in enumerate(items):
        rows = slice(ci * c, (ci + 1) * c)
        cs = slice(hd * LANES, (hd + 1) * LANES)
        st = state[hd]
        ws = _dot(jnp.concatenate([wb[rows, cs], qdb[rows, cs]], axis=0), st.astype(BF16))
        vnb = (ub[rows, cs] - ws[0:c]).astype(BF16)
        kv = _dot(jnp.concatenate([qkb[it], kdt[it]], axis=0), vnb)
        o = ws[c:] + kv[0:c]
        state[hd] = st * jnp.exp(gle[rows, cs]) + kv[c:]
        zz = z_ref[rows, cs].astype(F32)
        y = _rms_rows(o, nrm) * (zz * _sigmoid(zz))
        o_ref[rows, cs] = y.astype(o_ref.dtype)


def _dn_constants(t):
    r = jnp.arange(t)
    same = (r[:, None] // DN_BLOCK) == (r[None, :] // DN_BLOCK)
    lower = r[:, None] >= r[None, :]
    cum = jnp.concatenate([(same & lower).astype(BF16), same.astype(BF16)], axis=0)
    lane = jnp.arange(LANES)[:, None]
    head = jnp.arange(DN_WIDTH)[None, :] // DN_DV
    eg = (lane == head).astype(BF16)
    eb = (lane == head + DN_HEADS).astype(BF16)
    return cum, eg, eb


def _delta_net(proj, ab, conv_w, a_log_pad, dt_bias_pad, dn_norm, t):
    s = proj.shape[0]
    cum, eg, eb = _dn_constants(t)
    cb = COL_DN_Q // DN_QK_WIDTH
    n_items = (t // DN_BLOCK) * DN_HEADS
    const = lambda shape: pl.BlockSpec(shape, lambda i: (0, 0))
    return pl.pallas_call(
        _dn_kernel,
        grid=(s // t,),
        in_specs=[
            pl.BlockSpec((t, DN_QK_WIDTH), lambda i: (i, cb)),
            pl.BlockSpec((t, DN_QK_WIDTH), lambda i: (i, cb + 1)),
            pl.BlockSpec((t, DN_WIDTH), lambda i: (i, cb + 2)),
            pl.BlockSpec((t, DN_WIDTH), lambda i: (i, cb + 3)),
            pl.BlockSpec((t, AB_COLS), lambda i: (i, 0)),
            const((DN_CONV, 2 * DN_QK_WIDTH + DN_WIDTH)),
            const((1, AB_COLS)),
            const((1, AB_COLS)),
            const((1, DN_DV)),
            const((2 * t, t)),
            const((LANES, DN_WIDTH)),
            const((LANES, DN_WIDTH)),
        ],
        out_specs=pl.BlockSpec((t, DN_WIDTH), lambda i: (i, 0)),
        out_shape=jax.ShapeDtypeStruct((s, DN_WIDTH), BF16),
        scratch_shapes=[
            pltpu.VMEM((t + SUBLANES, 2 * DN_QK_WIDTH + DN_WIDTH), F32),
            pltpu.VMEM((t, DN_QK_WIDTH), F32),
            pltpu.VMEM((t, DN_QK_WIDTH), F32),
            pltpu.VMEM((t, DN_WIDTH), F32),
            pltpu.VMEM((t, DN_WIDTH), F32),
            pltpu.VMEM((t, DN_WIDTH), F32),
            pltpu.VMEM((t, DN_WIDTH), F32),
            pltpu.VMEM((LANES, t), F32),
            pltpu.VMEM((n_items, DN_BLOCK, DN_BLOCK), F32),
            pltpu.VMEM((n_items, DN_BLOCK, DN_BLOCK), F32),
            pltpu.VMEM((n_items, DN_BLOCK, DN_BLOCK), BF16),
            pltpu.VMEM((t, DN_WIDTH), F32),
            pltpu.VMEM((t, DN_QK_WIDTH), F32),
            pltpu.VMEM((t, DN_WIDTH), F32),
            pltpu.VMEM((t, DN_QK_WIDTH), BF16),
            pltpu.VMEM((t, DN_QK_WIDTH), BF16),
            pltpu.VMEM((n_items, DN_DK, DN_BLOCK), BF16),
            pltpu.VMEM((DN_HEADS, DN_DK, DN_DV), F32),
        ],
        compiler_params=_cparams(("arbitrary",)),
        name="delta_net",
    )(proj, proj, proj, proj, ab, conv_w, a_log_pad, dt_bias_pad, dn_norm, cum, eg, eb)


def _swa_kernel(sinks_ref, q_ref, kv_ref, kvp_ref, bias_ref, o_ref, scb, pb, denb):
    i = pl.program_id(0)
    tq = q_ref.shape[0]
    w = WINDOW
    nwb = tq // w
    grp = SWA_HEADS // SWA_KV_HEADS
    ext = jnp.concatenate([kvp_ref[...], kv_ref[...]], axis=0)
    kj = lax.broadcasted_iota(jnp.int32, (w, 2 * w), 1)
    no_prev = jnp.where(jnp.logical_and(kj < w, i == 0), NEG_BIG, 0.0)
    for wb in range(nwb):
        win = ext[wb * w:wb * w + 2 * w]
        qwb = q_ref[wb * w:(wb + 1) * w, :]
        for hd in range(SWA_HEADS):
            kh = win[:, (hd // grp) * SWA_DH:(hd // grp + 1) * SWA_DH]
            sc = _dot_nt(qwb[:, hd * SWA_DH:(hd + 1) * SWA_DH], kh) * (SWA_DH ** -0.5) + bias_ref[hd]
            if wb == 0:
                sc = sc + no_prev
            scb[hd] = sc
        for hd in range(SWA_HEADS):
            sc = scb[hd]
            sink = sinks_ref[hd]
            mx = jnp.maximum(jnp.max(sc, axis=-1, keepdims=True), sink)
            p = jnp.exp(sc - mx)
            pb[hd] = p.astype(BF16)
            den = jnp.sum(p, axis=-1, keepdims=True) + jnp.exp(sink - mx)
            denb[hd] = jnp.broadcast_to(den, (w, SWA_DH))
        outs = []
        for hd in range(SWA_HEADS):
            hk = hd // grp
            vh = win[:, SWA_KV_WIDTH + hk * SWA_DH:SWA_KV_WIDTH + (hk + 1) * SWA_DH]
            outs.append(_dot(pb[hd], vh) / denb[hd])
        o_ref[wb * w:(wb + 1) * w, :] = jnp.concatenate(outs, axis=1).astype(o_ref.dtype)


def _swa_bias():
    qi = jnp.arange(WINDOW)[:, None]
    kj = jnp.arange(2 * WINDOW)[None, :]
    dist = qi + WINDOW - kj
    slopes = jnp.exp2(-8.0 * (jnp.arange(SWA_HEADS, dtype=F32) + 1.0) / SWA_HEADS)
    band = (dist >= 0) & (dist < WINDOW)
    return jnp.where(band[None], -slopes[:, None, None] * dist.astype(F32)[None], NEG_BIG)


def _swa(proj, sinks, bias, tq):
    s = proj.shape[0]
    nwb = tq // WINDOW
    qb = COL_SWA_Q // SWA_WIDTH
    kvb = COL_SWA_KV // (2 * SWA_KV_WIDTH)
    return pl.pallas_call(
        _swa_kernel,
        grid=(s // tq,),
        in_specs=[
            pl.BlockSpec(memory_space=pltpu.SMEM),
            pl.BlockSpec((tq, SWA_WIDTH), lambda i: (i, qb)),
            pl.BlockSpec((tq, 2 * SWA_KV_WIDTH), lambda i: (i, kvb)),
            pl.BlockSpec((WINDOW, 2 * SWA_KV_WIDTH), lambda i: (jnp.maximum(i * nwb - 1, 0), kvb)),
            pl.BlockSpec((SWA_HEADS, WINDOW, 2 * WINDOW), lambda i: (0, 0, 0)),
        ],
        out_specs=pl.BlockSpec((tq, SWA_WIDTH), lambda i: (i, 0)),
        out_shape=jax.ShapeDtypeStruct((s, SWA_WIDTH), BF16),
        scratch_shapes=[
            pltpu.VMEM((SWA_HEADS, WINDOW, 2 * WINDOW), F32),
            pltpu.VMEM((SWA_HEADS, WINDOW, 2 * WINDOW), BF16),
            pltpu.VMEM((SWA_HEADS, WINDOW, SWA_DH), F32),
        ],
        compiler_params=_cparams(("parallel",)),
        name="swa",
    )(sinks, proj, proj, proj, bias)


def _sb_knorm_kernel(k_ref, o_ref, run_ref, *, tk):
    @pl.when(pl.program_id(0) == 0)
    def _():
        run_ref[...] = jnp.zeros(run_ref.shape, F32)

    run = run_ref[...]
    for sub in range(k_ref.shape[0] // tk):
        k = k_ref[sub * tk:(sub + 1) * tk, :].astype(F32)
        cur = []
        for hd in range(SB_HEADS):
            kh = k[:, hd * SB_DH:(hd + 1) * SB_DH]
            norm = jnp.sqrt(jnp.sum(kh * kh, axis=-1, keepdims=True))
            cur.append(jnp.broadcast_to(jnp.max(norm, axis=0, keepdims=True), (1, SB_DH)))
        run = jnp.maximum(run, jnp.concatenate(cur, axis=1))
        o_ref[sub] = run
    run_ref[...] = run


def _sb_key_norms(proj, tk):
    s = proj.shape[0]
    kb = COL_SB_Q // SB_WIDTH + 1
    rows = min(s, SB_NORM_ROWS)
    return pl.pallas_call(
        functools.partial(_sb_knorm_kernel, tk=tk),
        grid=(s // rows,),
        in_specs=[pl.BlockSpec((rows, SB_WIDTH), lambda t: (t, kb))],
        out_specs=pl.BlockSpec((rows // tk, 1, SB_WIDTH), lambda t: (t, 0, 0)),
        out_shape=jax.ShapeDtypeStruct((s // tk, 1, SB_WIDTH), F32),
        scratch_shapes=[pltpu.VMEM((1, SB_WIDTH), F32)],
        compiler_params=_cparams(("arbitrary",)),
        name="sb_key_norms",
    )(proj)


def _sb_kernel(kmax_ref, q_ref, k_ref, v_ref, ntri_ref, o_ref, zbuf, spb, abuf, acc_ref, aft_ref, qn_ref):
    i = pl.program_id(0)
    tq = q_ref.shape[0]
    tk = tq
    blk = LANES
    nsub = tk // blk
    c2 = (SB_DH ** -0.5) * LOG2E
    ntri = ntri_ref[...]
    acc_ref[...] = jnp.zeros(acc_ref.shape, F32)
    aft_ref[...] = jnp.zeros(aft_ref.shape, F32)
    hcols = [slice(hd * SB_DH, (hd + 1) * SB_DH) for hd in range(SB_HEADS)]
    for hd, hc in enumerate(hcols):
        qf = q_ref[:, hc].astype(F32)
        qn = jnp.sqrt(jnp.sum(qf * qf, axis=-1, keepdims=True)) * c2
        qn_ref[hd] = jnp.broadcast_to(qn, (tq, LANES))

    def tile(k0, mask):
        keys = pl.ds(k0, tk)
        for hd, hc in enumerate(hcols):
            zbuf[hd] = _dot_nt(q_ref[:, hc], k_ref[keys, hc]) * c2
        for hd in range(SB_HEADS):
            z2 = zbuf[hd]
            sp2 = jnp.maximum(z2, 0.0) + jnp.log(1.0 + jnp.exp2(-jnp.abs(z2))) * LOG2E
            if mask is not None:
                sp2 = jnp.where(mask, sp2, 0.0)
            spb[hd] = sp2.astype(BF16)
        for hd in range(SB_HEADS):
            aft = aft_ref[hd]
            revs = [None] * nsub
            for sb in reversed(range(nsub)):
                cum = _dot(spb[hd, :, sb * blk:(sb + 1) * blk], ntri)
                revs[sb] = cum[:, 0:blk] + aft
                aft = aft + cum[:, blk:]
            a = jnp.exp2(zbuf[hd] + jnp.concatenate(revs, axis=1))
            if mask is not None:
                a = jnp.where(mask, a, 0.0)
            abuf[hd] = a.astype(BF16)
            aft_ref[hd] = aft
        for hd, hc in enumerate(hcols):
            acc_ref[hd] += _dot(abuf[hd], v_ref[keys, hc])

    def live(kt):
        kmax = kmax_ref[jnp.maximum(kt, 0)]
        bound = None
        for hd, hc in enumerate(hcols):
            b = qn_ref[hd] * kmax[:, hc] + aft_ref[hd]
            bound = b if bound is None else jnp.maximum(bound, b)
        return jnp.max(bound) > -SB_SKIP_LOG2

    row = lax.broadcasted_iota(jnp.int32, (tq, tk), 0)
    col = lax.broadcasted_iota(jnp.int32, (tq, tk), 1)
    tile(pl.multiple_of(i * tq, tq), row > col)

    def cond(carry):
        n, go = carry
        return jnp.logical_and(n < i, go)

    def body(carry):
        n, _ = carry
        kt = i - 1 - n
        tile(pl.multiple_of(kt * tk, tk), None)
        return n + 1, live(kt - 1)

    lax.while_loop(cond, body, (jnp.int32(0), jnp.asarray(True)))
    for hd, hc in enumerate(hcols):
        o_ref[:, hc] = acc_ref[hd].astype(o_ref.dtype)


def _stick_breaking(proj, tq):
    s = proj.shape[0]
    qb = COL_SB_Q // SB_WIDTH
    r = jnp.arange(LANES)
    ntri = -jnp.concatenate([(r[:, None] >= r[None, :]).astype(BF16), jnp.ones((LANES, LANES), BF16)], axis=1)
    kmax = _sb_key_norms(proj, tq)
    resident = lambda col: pl.BlockSpec((s, SB_WIDTH), lambda i: (0, col), pipeline_mode=pl.Buffered(1))
    return pl.pallas_call(
        _sb_kernel,
        grid=(s // tq,),
        in_specs=[
            pl.BlockSpec((s // tq, 1, SB_WIDTH), lambda i: (0, 0, 0)),
            pl.BlockSpec((tq, SB_WIDTH), lambda i: (i, qb)),
            resident(qb + 1),
            resident(qb + 2),
            pl.BlockSpec((LANES, 2 * LANES), lambda i: (0, 0)),
        ],
        out_specs=pl.BlockSpec((tq, SB_WIDTH), lambda i: (i, 0)),
        out_shape=jax.ShapeDtypeStruct((s, SB_WIDTH), BF16),
        scratch_shapes=[
            pltpu.VMEM((SB_HEADS, tq, tq), F32),
            pltpu.VMEM((SB_HEADS, tq, tq), BF16),
            pltpu.VMEM((SB_HEADS, tq, tq), BF16),
            pltpu.VMEM((SB_HEADS, tq, SB_DH), F32),
            pltpu.VMEM((SB_HEADS, tq, LANES), F32),
            pltpu.VMEM((SB_HEADS, tq, LANES), F32),
        ],
        compiler_params=_cparams(("parallel",)),
        name="stick_breaking",
    )(kmax, proj, proj, proj, ntri)


def _merge_kernel(h_ref, g0_ref, g1_ref, g2_ref, ydn_ref, yswa_ref, ysb_ref, wdn_ref, wswa_ref, wsb_ref, wo_ref,
                  o_ref):
    merged = _sigmoid(g0_ref[...].astype(F32)) * _dot(ydn_ref[...], wdn_ref[...])
    merged = merged + _sigmoid(g1_ref[...].astype(F32)) * _dot(yswa_ref[...], wswa_ref[...])
    merged = merged + _sigmoid(g2_ref[...].astype(F32)) * _dot(ysb_ref[...], wsb_ref[...])
    o_ref[...] = h_ref[...] + _dot(merged.astype(BF16), wo_ref[...])


def _merge(h, proj, y_dn, y_swa, y_sb, w_dn, w_swa, w_sb, w_o, l, tm):
    s = h.shape[0]
    rows = lambda width, col=0: pl.BlockSpec((tm, width), lambda i: (i, col))
    const = lambda a: pl.BlockSpec((None,) + a.shape[1:], lambda i: (l, 0, 0), pipeline_mode=pl.Buffered(1))
    return pl.pallas_call(
        _merge_kernel,
        grid=(s // tm,),
        in_specs=[
            rows(D_MODEL), rows(D_MODEL, 0), rows(D_MODEL, 1), rows(D_MODEL, 2),
            rows(DN_WIDTH), rows(SWA_WIDTH), rows(SB_WIDTH),
            const(w_dn), const(w_swa), const(w_sb), const(w_o),
        ],
        out_specs=rows(D_MODEL),
        out_shape=jax.ShapeDtypeStruct((s, D_MODEL), F32),
        compiler_params=_cparams(("parallel",)),
        name="merge",
    )(h, proj, proj, proj, y_dn, y_swa, y_sb, w_dn, w_swa, w_sb, w_o)


def _norm_matmul_kernel(x_ref, g_ref, w_ref, o_ref):
    xn = _rms_rows(x_ref[...], g_ref[...]).astype(BF16)
    o_ref[...] = _dot(xn, w_ref[...]).astype(o_ref.dtype)


def _mem_kv(mem, g, w, l):
    m = mem.shape[0]
    n = w.shape[2]
    return pl.pallas_call(
        _norm_matmul_kernel,
        grid=(1,),
        in_specs=[
            pl.BlockSpec((m, D_MODEL), lambda i: (0, 0)),
            pl.BlockSpec((1, D_MODEL), lambda i: (0, 0)),
            pl.BlockSpec((None, D_MODEL, n), lambda i: (l, 0, 0)),
        ],
        out_specs=pl.BlockSpec((m, n), lambda i: (0, 0)),
        out_shape=jax.ShapeDtypeStruct((m, n), BF16),
        compiler_params=_cparams(("arbitrary",)),
        name="mem_kv",
    )(mem, g, w)


def _xattn_kernel(h_ref, g_ref, kv_ref, wq_ref, wo_ref, o_ref):
    h = h_ref[...]
    hn = _rms_rows(h, g_ref[...]).astype(BF16)
    q = _dot(hn, wq_ref[...]).astype(BF16)
    outs = []
    for hd in range(X_HEADS):
        cs = slice(hd * X_DH, (hd + 1) * X_DH)
        kh = kv_ref[:, cs]
        vh = kv_ref[:, X_WIDTH + hd * X_DH:X_WIDTH + (hd + 1) * X_DH]
        sc = _dot_nt(q[:, cs], kh) * (X_DH ** -0.5)
        mx = jnp.max(sc, axis=-1, keepdims=True)
        p = jnp.exp(sc - mx)
        den = jnp.sum(p, axis=-1, keepdims=True)
        outs.append((_dot(p.astype(BF16), vh) / den).astype(BF16))
    o = jnp.concatenate(outs, axis=1)
    o_ref[...] = h + _dot(o, wo_ref[...])


def _xattn(h, g, kv, w_q, w_o, l, tm):
    s = h.shape[0]
    const = lambda a: pl.BlockSpec(a.shape, lambda i: (0, 0))
    layer = lambda a: pl.BlockSpec((None,) + a.shape[1:], lambda i: (l, 0, 0))
    return pl.pallas_call(
        _xattn_kernel,
        grid=(s // tm,),
        in_specs=[pl.BlockSpec((tm, D_MODEL), lambda i: (i, 0)), const(g), const(kv), layer(w_q), layer(w_o)],
        out_specs=pl.BlockSpec((tm, D_MODEL), lambda i: (i, 0)),
        out_shape=jax.ShapeDtypeStruct((s, D_MODEL), F32),
        compiler_params=_cparams(("parallel",)),
        name="xattn",
    )(h, g, kv, w_q, w_o)


def _ffn_kernel(h_ref, g_ref, wg_ref, wv_ref, cg_ref, cv_ref, wd_ref, gout_ref, o_ref, hn_ref, gbuf, vbuf, carry,
                *, norm_out):
    i = pl.program_id(0)
    j = pl.program_id(1)
    tm = h_ref.shape[0]
    halo = SUBLANES

    @pl.when(j == 0)
    def _():
        h = h_ref[...]
        hn_ref[...] = _rms_rows(h, g_ref[...]).astype(BF16)
        o_ref[...] = h

    @pl.when(i == 0)
    def _():
        carry[j] = jnp.zeros(carry.shape[1:], F32)

    hn = hn_ref[...]
    gbuf[0:halo, :] = carry[j, 0]
    vbuf[0:halo, :] = carry[j, 1]

    gbuf[halo:, :] = _dot(hn, wg_ref[...])
    vbuf[halo:, :] = _dot(hn, wv_ref[...])
    carry[j, 0] = gbuf[tm:tm + halo, :]
    carry[j, 1] = vbuf[tm:tm + halo, :]

    def conv(buf, cw_ref):
        base = halo - (FFN_CONV - 1)
        out = cw_ref[0:1, :] * buf[base:base + tm, :]
        for tap in range(1, FFN_CONV):
            out = out + cw_ref[tap:tap + 1, :] * buf[base + tap:base + tap + tm, :]
        return out

    gate = conv(gbuf, cg_ref)
    val = conv(vbuf, cv_ref)
    act = (gate * _sigmoid(gate) * val).astype(BF16)
    o_ref[...] += _dot(act, wd_ref[...])

    if norm_out:
        @pl.when(j == pl.num_programs(1) - 1)
        def _():
            o_ref[...] = _rms_rows(o_ref[...], gout_ref[...])


def _ffn(h, g, w_up, conv_w, w_down, g_out, l, tm, tn, norm_out):
    s = h.shape[0]
    nj = D_FF // tn
    return pl.pallas_call(
        functools.partial(_ffn_kernel, norm_out=norm_out),
        grid=(s // tm, nj),
        in_specs=[
            pl.BlockSpec((tm, D_MODEL), lambda i, j: (i, 0)),
            pl.BlockSpec((1, D_MODEL), lambda i, j: (0, 0)),
            pl.BlockSpec((None, D_MODEL, tn), lambda i, j: (l, 0, j)),
            pl.BlockSpec((None, D_MODEL, tn), lambda i, j: (l, 0, j + nj)),
            pl.BlockSpec((FFN_CONV, tn), lambda i, j: (0, j)),
            pl.BlockSpec((FFN_CONV, tn), lambda i, j: (0, j + nj)),
            pl.BlockSpec((None, tn, D_MODEL), lambda i, j: (l, j, 0)),
            pl.BlockSpec((1, D_MODEL), lambda i, j: (0, 0)),
        ],
        out_specs=pl.BlockSpec((tm, D_MODEL), lambda i, j: (i, 0)),
        out_shape=jax.ShapeDtypeStruct((s, D_MODEL), F32),
        scratch_shapes=[
            pltpu.VMEM((tm, D_MODEL), BF16),
            pltpu.VMEM((tm + SUBLANES, tn), F32),
            pltpu.VMEM((tm + SUBLANES, tn), F32),
            pltpu.VMEM((nj, 2, SUBLANES, tn), F32),
        ],
        compiler_params=_cparams(("arbitrary", "arbitrary")),
        name="conv_ffn",
    )(h, g, w_up, w_up, conv_w, conv_w, w_down, g_out)


def _tile(s, want):
    return min(s, want)


IN_PROJ_ROWS, IN_PROJ_COLS = 1024, 1792
DN_ROWS = 512
SWA_ROWS = 512
SB_ROWS = 256
MERGE_ROWS = 512
XATTN_ROWS = 1024
FFN_ROWS, FFN_COLS = 512, 1024


def _w_in_pieces():
    o_z = 2 * DN_QK_WIDTH + DN_WIDTH
    o_a = o_z + DN_WIDTH
    o_swq = o_a + 2 * DN_HEADS
    o_swkv = o_swq + SWA_WIDTH
    o_sb = o_swkv + 2 * SWA_KV_WIDTH
    o_gate = o_sb + 3 * SB_WIDTH
    o_end = o_gate + N_BRANCH * D_MODEL
    big = [(o_gate, o_end), (0, o_a), (o_swq, o_swkv), (o_sb, o_gate), (o_swkv, o_sb)]
    return big, (o_a, o_swq), o_end


def _reorder_w_in(w_in):
    pieces, (a0, a1), _ = _w_in_pieces()
    wt = jnp.swapaxes(w_in, 1, 2)
    big = jnp.concatenate([wt[:, lo:hi].astype(BF16) for lo, hi in pieces], axis=1)
    ab = jnp.pad(wt[:, a0:a1], ((0, 0), (0, AB_COLS - (a1 - a0)), (0, 0))).astype(BF16)
    return big, ab


def kernel(x, mem, norm_mix, w_in, dn_conv, dn_a_log, dn_dt_bias, dn_norm, swa_sinks, w_br_dn, w_br_swa, w_br_sb, w_o, norm_xattn, norm_mem, w_xq, w_xkv, w_xo, norm_ffn, w_up, ffn_conv, w_down, norm_final):
    depth = w_in.shape[0]
    s = x.shape[1]
    assert x.shape[0] == 1 and s % 256 == 0
    h = x[0]
    mem2 = mem[0]

    w_big, w_ab = _reorder_w_in(w_in)
    w_br_dn_b = w_br_dn.astype(BF16)
    w_br_swa_b = w_br_swa.astype(BF16)
    w_br_sb_b = w_br_sb.astype(BF16)
    w_o_b = w_o.astype(BF16)
    w_xq_b = w_xq.astype(BF16)
    w_xkv_b = w_xkv.astype(BF16)
    w_xo_b = w_xo.astype(BF16)
    w_up_b = w_up.astype(BF16)
    w_down_b = w_down.astype(BF16)
    pad8 = lambda a: jnp.pad(a, ((0, 0), (0, AB_COLS - a.shape[-1])))
    a_log_pad = pad8(dn_a_log)
    dt_bias_pad = pad8(dn_dt_bias)
    swa_bias = _swa_bias()

    for l in range(depth):
        proj, ab = _in_proj(h, norm_mix[l][None], w_big, w_ab, l, tm=_tile(s, IN_PROJ_ROWS), tn=IN_PROJ_COLS)
        y_dn = _delta_net(proj, ab, dn_conv[l], a_log_pad[l][None], dt_bias_pad[l][None], dn_norm[l][None],
                          t=_tile(s, DN_ROWS))
        y_swa = _swa(proj, swa_sinks[l], swa_bias, tq=_tile(s, SWA_ROWS))
        y_sb = _stick_breaking(proj, tq=_tile(s, SB_ROWS))
        h = _merge(h, proj, y_dn, y_swa, y_sb, w_br_dn_b, w_br_swa_b, w_br_sb_b, w_o_b, l,
                   tm=_tile(s, MERGE_ROWS))
        kv = _mem_kv(mem2, norm_mem[l][None], w_xkv_b, l)
        h = _xattn(h, norm_xattn[l][None], kv, w_xq_b, w_xo_b, l, tm=_tile(s, XATTN_ROWS))
        h = _ffn(h, norm_ffn[l][None], w_up_b, ffn_conv[l], w_down_b, norm_final[None], l,
                 tm=_tile(s, FFN_ROWS), tn=FFN_COLS, norm_out=(l == depth - 1))
    return h[None]
```

```python
import functools

import jax
import jax.numpy as jnp
from jax import lax
from jax.experimental import pallas as pl
from jax.experimental.pallas import tpu as pltpu

F32 = jnp.float32
BF16 = jnp.bfloat16

D_MODEL = 2048
EPS = 1e-6
DN_HEADS = 8
DN_DK = 128
DN_DV = 128
DN_CONV = 4
DN_BLOCK = 128
SWA_HEADS = 8
SWA_KV_HEADS = 2
SWA_DH = 64
WINDOW = 128
SB_HEADS = 4
SB_DH = 128
X_HEADS = 4
X_DH = 128
D_FF = 4096
FFN_CONV = 3
N_BRANCH = 3

DN_QK_WIDTH = DN_HEADS * DN_DK
DN_WIDTH = DN_HEADS * DN_DV
SWA_WIDTH = SWA_HEADS * SWA_DH
SWA_KV_WIDTH = SWA_KV_HEADS * SWA_DH
SB_WIDTH = SB_HEADS * SB_DH
X_WIDTH = X_HEADS * X_DH

LANES = 128
SUBLANES = 8
VMEM_LIMIT = 56 * 1024 * 1024

PROJ_COLS = N_BRANCH * D_MODEL + 3 * DN_QK_WIDTH + DN_WIDTH + SWA_WIDTH + 3 * SB_WIDTH + 2 * SWA_KV_WIDTH
COL_DN_Q = N_BRANCH * D_MODEL
COL_DN_Z = COL_DN_Q + 3 * DN_QK_WIDTH
COL_SWA_Q = COL_DN_Z + DN_WIDTH
COL_SB_Q = COL_SWA_Q + SWA_WIDTH
COL_SWA_KV = COL_SB_Q + 3 * SB_WIDTH
AB_COLS = LANES

NEG_BIG = -1e30
LOG2E = 1.4426950408889634
SB_SKIP_LOG2 = 170.0
SB_NORM_ROWS = 4096


def _cparams(semantics):
    return pltpu.CompilerParams(dimension_semantics=semantics, vmem_limit_bytes=VMEM_LIMIT)


def _dot(a, b):
    return jnp.dot(a, b, preferred_element_type=F32)


def _dot_nt(a, b):
    return lax.dot_general(a, b, (((1,), (1,)), ((), ())), preferred_element_type=F32)


def _split3(x):
    x1 = x.astype(BF16)
    r1 = x - x1.astype(F32)
    x2 = r1.astype(BF16)
    x3 = (r1 - x2.astype(F32)).astype(BF16)
    return x1, x2, x3


def _dot_01_lhs(m01, x):
    x1, x2, x3 = _split3(x)
    return _dot(m01, x1) + _dot(m01, x2) + _dot(m01, x3)


def _dot_01_rhs(x, m01):
    x1, x2, x3 = _split3(x)
    return _dot(x1, m01) + _dot(x2, m01) + _dot(x3, m01)


def _dot_solve(a, b):
    return jnp.dot(a.astype(BF16), b.astype(BF16), preferred_element_type=F32)


def _sigmoid(x):
    return 1.0 / (1.0 + jnp.exp(-x))


def _softplus(x):
    return jnp.maximum(x, 0.0) + jnp.log1p(jnp.exp(-jnp.abs(x)))


def _rms_rows(x, g):
    ms = jnp.mean(x * x, axis=-1, keepdims=True)
    return x * lax.rsqrt(ms + EPS) * g


def _in_proj_kernel(h_ref, g_ref, wt_ref, wabt_ref, o_ref, ab_ref, xn_ref):
    @pl.when(pl.program_id(1) == 0)
    def _():
        xn = _rms_rows(h_ref[...], g_ref[...]).astype(BF16)
        xn_ref[...] = xn
        ab_ref[...] = _dot_nt(xn, wabt_ref[...])

    o_ref[...] = _dot_nt(xn_ref[...], wt_ref[...]).astype(o_ref.dtype)


def _in_proj(h, g, w, wab, l, tm, tn):
    s = h.shape[0]
    n = w.shape[1]
    return pl.pallas_call(
        _in_proj_kernel,
        grid=(s // tm, n // tn),
        in_specs=[
            pl.BlockSpec((tm, D_MODEL), lambda i, j: (i, 0)),
            pl.BlockSpec((1, D_MODEL), lambda i, j: (0, 0)),
            pl.BlockSpec((None, tn, D_MODEL), lambda i, j: (l, j, 0)),
            pl.BlockSpec((None, AB_COLS, D_MODEL), lambda i, j: (l, 0, 0)),
        ],
        out_specs=[
            pl.BlockSpec((tm, tn), lambda i, j: (i, j)),
            pl.BlockSpec((tm, AB_COLS), lambda i, j: (i, 0)),
        ],
        out_shape=[jax.ShapeDtypeStruct((s, n), BF16), jax.ShapeDtypeStruct((s, AB_COLS), F32)],
        scratch_shapes=[pltpu.VMEM((tm, D_MODEL), BF16)],
        compiler_params=_cparams(("parallel", "arbitrary")),
        name="in_proj",
    )(h, g, w, wab)


def _dn_kernel(q_ref, k_ref, v_ref, z_ref, ab_ref, cw_ref, alog_ref, dtb_ref, nrm_ref, cum_ref, eg_ref, eb_ref,
               o_ref, xbuf, qs, ks, vs, gce, gle, ble, gct, nb, qb, qkb, rbv, rbk, ub, wb, qdb, kdt, state):
    i = pl.program_id(0)
    t = q_ref.shape[0]
    c = DN_BLOCK
    halo = SUBLANES

    @pl.when(i == 0)
    def _():
        xbuf[0:halo, :] = jnp.zeros((halo, xbuf.shape[1]), F32)
        state[...] = jnp.zeros(state.shape, F32)

    @pl.when(i > 0)
    def _():
        xbuf[0:halo, :] = xbuf[t:t + halo, :]

    xbuf[halo:halo + t, 0:DN_QK_WIDTH] = q_ref[...].astype(F32)
    xbuf[halo:halo + t, DN_QK_WIDTH:2 * DN_QK_WIDTH] = k_ref[...].astype(F32)
    xbuf[halo:halo + t, 2 * DN_QK_WIDTH:] = v_ref[...].astype(F32)

    n_slabs = (2 * DN_QK_WIDTH + DN_WIDTH) // LANES
    for s in range(n_slabs):
        cs = slice(s * LANES, (s + 1) * LANES)
        acc = cw_ref[0:1, cs] * xbuf[halo - 3:halo - 3 + t, cs]
        for tap in range(1, DN_CONV):
            acc = acc + cw_ref[tap:tap + 1, cs] * xbuf[halo - 3 + tap:halo - 3 + tap + t, cs]
        y = acc * _sigmoid(acc)
        if s < 2 * DN_HEADS:
            y = y * lax.rsqrt(jnp.sum(y * y, axis=-1, keepdims=True) + EPS)
        if s < DN_HEADS:
            qs[:, cs] = y * (DN_DK ** -0.5)
        elif s < 2 * DN_HEADS:
            ks[:, (s - DN_HEADS) * LANES:(s - DN_HEADS + 1) * LANES] = y
        else:
            vs[:, (s - 2 * DN_HEADS) * LANES:(s - 2 * DN_HEADS + 1) * LANES] = y

    ab = ab_ref[...]
    g = -jnp.exp(alog_ref[...]) * _softplus(ab + dtb_ref[...])
    beta = _sigmoid(ab)
    cums = _dot_01_lhs(cum_ref[...], g)
    gce[...] = _dot_01_rhs(cums[0:t], eg_ref[...])
    gle[...] = _dot_01_rhs(cums[t:2 * t], eg_ref[...])
    ble[...] = _dot_01_rhs(beta, eb_ref[...])
    gct[...] = cums[0:t].T

    row = lax.broadcasted_iota(jnp.int32, (c, c), 0)
    col = lax.broadcasted_iota(jnp.int32, (c, c), 1)
    causal = row >= col
    strict = row > col
    pair = row // 2 == col // 2
    nrm = nrm_ref[...]
    items = [(ci, hd) for ci in range(t // c) for hd in range(DN_HEADS)]

    for it, (ci, hd) in enumerate(items):
        rows = slice(ci * c, (ci + 1) * c)
        cs = slice(hd * LANES, (hd + 1) * LANES)
        k_ = ks[rows, cs]
        gcc = gce[rows, cs]
        g_row = gct[hd:hd + 1, rows]
        dm = jnp.where(causal, gcc - g_row, 0.0)
        decay = jnp.where(causal, jnp.exp(dm), 0.0)
        lhs = jnp.concatenate([k_ * ble[rows, cs], qs[rows, cs]], axis=0).astype(BF16)
        kq = _dot_nt(lhs, k_.astype(BF16))
        nmat = jnp.where(strict, kq[0:c] * decay, 0.0)
        nb[it] = nmat
        qb[it] = jnp.where(pair, -nmat, 0.0)
        qkb[it] = (kq[c:] * decay).astype(BF16)

    for it, (ci, hd) in enumerate(items):
        rows = slice(ci * c, (ci + 1) * c)
        cs = slice(hd * LANES, (hd + 1) * LANES)
        q_ = qs[rows, cs]
        k_ = ks[rows, cs]
        gcc = gce[rows, cs]
        bet = ble[rows, cs]
        egc = jnp.exp(gcc)
        rbv[rows, cs] = vs[rows, cs] * bet
        rbk[rows, cs] = k_ * bet * egc
        qdb[rows, cs] = (q_ * egc).astype(BF16)
        kdt[it] = (k_ * jnp.exp(gle[rows, cs] - gcc)).T.astype(BF16)

    size = 2
    while size < c:
        join = (row // (2 * size) == col // (2 * size)) & (row // size != col // size)
        for it in range(len(items)):
            qm = qb[it]
            lo = jnp.where(join, nb[it], 0.0)
            a = lo + _dot_solve(qm, lo)
            qb[it] = qm - a - _dot_solve(a, qm)
        size *= 2

    for it, (ci, hd) in enumerate(items):
        rows = slice(ci * c, (ci + 1) * c)
        cs = slice(hd * LANES, (hd + 1) * LANES)
        rhs = jnp.concatenate([rbv[rows, cs], rbk[rows, cs]], axis=1)
        uw = rhs + _dot_solve(qb[it], rhs)
        ub[rows, cs] = uw[:, 0:DN_DV]
        wb[rows, cs] = uw[:, DN_DV:].astype(BF16)

    for it, (ci, hd) in enumerate(items):
        rows = slice(ci * c, (ci + 1) * c)
        cs = slice(hd * LANES, (hd + 1) * LANES)
        st = state[hd]
        ws = _dot(jnp.concatenate([wb[rows, cs], qdb[rows, cs]], axis=0), st.astype(BF16))
        vnb = (ub[rows, cs] - ws[0:c]).astype(BF16)
        kv = _dot(jnp.concatenate([qkb[it], kdt[it]], axis=0), vnb)
        o = ws[c:] + kv[0:c]
        state[hd] = st * jnp.exp(gle[rows, cs]) + kv[c:]
        zz = z_ref[rows, cs].astype(F32)
        y = _rms_rows(o, nrm) * (zz * _sigmoid(zz))
        o_ref[rows, cs] = y.astype(o_ref.dtype)


def _dn_constants(t):
    r = jnp.arange(t)
    same = (r[:, None] // DN_BLOCK) == (r[None, :] // DN_BLOCK)
    lower = r[:, None] >= r[None, :]
    cum = jnp.concatenate([(same & lower).astype(BF16), same.astype(BF16)], axis=0)
    lane = jnp.arange(LANES)[:, None]
    head = jnp.arange(DN_WIDTH)[None, :] // DN_DV
    eg = (lane == head).astype(BF16)
    eb = (lane == head + DN_HEADS).astype(BF16)
    return cum, eg, eb


def _delta_net(proj, ab, conv_w, a_log_pad, dt_bias_pad, dn_norm, t):
    s = proj.shape[0]
    cum, eg, eb = _dn_constants(t)
    cb = COL_DN_Q // DN_QK_WIDTH
    n_items = (t // DN_BLOCK) * DN_HEADS
    const = lambda shape: pl.BlockSpec(shape, lambda i: (0, 0))
    return pl.pallas_call(
        _dn_kernel,
        grid=(s // t,),
        in_specs=[
            pl.BlockSpec((t, DN_QK_WIDTH), lambda i: (i, cb)),
            pl.BlockSpec((t, DN_QK_WIDTH), lambda i: (i, cb + 1)),
            pl.BlockSpec((t, DN_WIDTH), lambda i: (i, cb + 2)),
            pl.BlockSpec((t, DN_WIDTH), lambda i: (i, cb + 3)),
            pl.BlockSpec((t, AB_COLS), lambda i: (i, 0)),
            const((DN_CONV, 2 * DN_QK_WIDTH + DN_WIDTH)),
            const((1, AB_COLS)),
            const((1, AB_COLS)),
            const((1, DN_DV)),
            const((2 * t, t)),
            const((LANES, DN_WIDTH)),
            const((LANES, DN_WIDTH)),
        ],
        out_specs=pl.BlockSpec((t, DN_WIDTH), lambda i: (i, 0)),
        out_shape=jax.ShapeDtypeStruct((s, DN_WIDTH), BF16),
        scratch_shapes=[
            pltpu.VMEM((t + SUBLANES, 2 * DN_QK_WIDTH + DN_WIDTH), F32),
            pltpu.VMEM((t, DN_QK_WIDTH), F32),
            pltpu.VMEM((t, DN_QK_WIDTH), F32),
            pltpu.VMEM((t, DN_WIDTH), F32),
            pltpu.VMEM((t, DN_WIDTH), F32),
            pltpu.VMEM((t, DN_WIDTH), F32),
            pltpu.VMEM((t, DN_WIDTH), F32),
            pltpu.VMEM((LANES, t), F32),
            pltpu.VMEM((n_items, DN_BLOCK, DN_BLOCK), F32),
            pltpu.VMEM((n_items, DN_BLOCK, DN_BLOCK), F32),
            pltpu.VMEM((n_items, DN_BLOCK, DN_BLOCK), BF16),
            pltpu.VMEM((t, DN_WIDTH), F32),
            pltpu.VMEM((t, DN_QK_WIDTH), F32),
            pltpu.VMEM((t, DN_WIDTH), F32),
            pltpu.VMEM((t, DN_QK_WIDTH), BF16),
            pltpu.VMEM((t, DN_QK_WIDTH), BF16),
            pltpu.VMEM((n_items, DN_DK, DN_BLOCK), BF16),
            pltpu.VMEM((DN_HEADS, DN_DK, DN_DV), F32),
        ],
        compiler_params=_cparams(("arbitrary",)),
        name="delta_net",
    )(proj, proj, proj, proj, ab, conv_w, a_log_pad, dt_bias_pad, dn_norm, cum, eg, eb)


def _swa_kernel(sinks_ref, q_ref, kv_ref, kvp_ref, bias_ref, o_ref, scb, pb, denb):
    i = pl.program_id(0)
    tq = q_ref.shape[0]
    w = WINDOW
    nwb = tq // w
    grp = SWA_HEADS // SWA_KV_HEADS
    ext = jnp.concatenate([kvp_ref[...], kv_ref[...]], axis=0)
    kj = lax.broadcasted_iota(jnp.int32, (w, 2 * w), 1)
    no_prev = jnp.where(jnp.logical_and(kj < w, i == 0), NEG_BIG, 0.0)
    for wb in range(nwb):
        win = ext[wb * w:wb * w + 2 * w]
        qwb = q_ref[wb * w:(wb + 1) * w, :]
        for hd in range(SWA_HEADS):
            kh = win[:, (hd // grp) * SWA_DH:(hd // grp + 1) * SWA_DH]
            sc = _dot_nt(qwb[:, hd * SWA_DH:(hd + 1) * SWA_DH], kh) * (SWA_DH ** -0.5) + bias_ref[hd]
            if wb == 0:
                sc = sc + no_prev
            scb[hd] = sc
        for hd in range(SWA_HEADS):
            sc = scb[hd]
            sink = sinks_ref[hd]
            mx = jnp.maximum(jnp.max(sc, axis=-1, keepdims=True), sink)
            p = jnp.exp(sc - mx)
            pb[hd] = p.astype(BF16)
            den = jnp.sum(p, axis=-1, keepdims=True) + jnp.exp(sink - mx)
            denb[hd] = jnp.broadcast_to(den, (w, SWA_DH))
        outs = []
        for hd in range(SWA_HEADS):
            hk = hd // grp
            vh = win[:, SWA_KV_WIDTH + hk * SWA_DH:SWA_KV_WIDTH + (hk + 1) * SWA_DH]
            outs.append(_dot(pb[hd], vh) / denb[hd])
        o_ref[wb * w:(wb + 1) * w, :] = jnp.concatenate(outs, axis=1).astype(o_ref.dtype)


def _swa_bias():
    qi = jnp.arange(WINDOW)[:, None]
    kj = jnp.arange(2 * WINDOW)[None, :]
    dist = qi + WINDOW - kj
    slopes = jnp.exp2(-8.0 * (jnp.arange(SWA_HEADS, dtype=F32) + 1.0) / SWA_HEADS)
    band = (dist >= 0) & (dist < WINDOW)
    return jnp.where(band[None], -slopes[:, None, None] * dist.astype(F32)[None], NEG_BIG)


def _swa(proj, sinks, bias, tq):
    s = proj.shape[0]
    nwb = tq // WINDOW
    qb = COL_SWA_Q // SWA_WIDTH
    kvb = COL_SWA_KV // (2 * SWA_KV_WIDTH)
    return pl.pallas_call(
        _swa_kernel,
        grid=(s // tq,),
        in_specs=[
            pl.BlockSpec(memory_space=pltpu.SMEM),
            pl.BlockSpec((tq, SWA_WIDTH), lambda i: (i, qb)),
            pl.BlockSpec((tq, 2 * SWA_KV_WIDTH), lambda i: (i, kvb)),
            pl.BlockSpec((WINDOW, 2 * SWA_KV_WIDTH), lambda i: (jnp.maximum(i * nwb - 1, 0), kvb)),
            pl.BlockSpec((SWA_HEADS, WINDOW, 2 * WINDOW), lambda i: (0, 0, 0)),
        ],
        out_specs=pl.BlockSpec((tq, SWA_WIDTH), lambda i: (i, 0)),
        out_shape=jax.ShapeDtypeStruct((s, SWA_WIDTH), BF16),
        scratch_shapes=[
            pltpu.VMEM((SWA_HEADS, WINDOW, 2 * WINDOW), F32),
            pltpu.VMEM((SWA_HEADS, WINDOW, 2 * WINDOW), BF16),
            pltpu.VMEM((SWA_HEADS, WINDOW, SWA_DH), F32),
        ],
        compiler_params=_cparams(("parallel",)),
        name="swa",
    )(sinks, proj, proj, proj, bias)


def _sb_knorm_kernel(k_ref, o_ref, run_ref, *, tk):
    @pl.when(pl.program_id(0) == 0)
    def _():
        run_ref[...] = jnp.zeros(run_ref.shape, F32)

    run = run_ref[...]
    for sub in range(k_ref.shape[0] // tk):
        k = k_ref[sub * tk:(sub + 1) * tk, :].astype(F32)
        cur = []
        for hd in range(SB_HEADS):
            kh = k[:, hd * SB_DH:(hd + 1) * SB_DH]
            norm = jnp.sqrt(jnp.sum(kh * kh, axis=-1, keepdims=True))
            cur.append(jnp.broadcast_to(jnp.max(norm, axis=0, keepdims=True), (1, SB_DH)))
        run = jnp.maximum(run, jnp.concatenate(cur, axis=1))
        o_ref[sub] = run
    run_ref[...] = run


def _sb_key_norms(proj, tk):
    s = proj.shape[0]
    kb = COL_SB_Q // SB_WIDTH + 1
    rows = min(s, SB_NORM_ROWS)
    return pl.pallas_call(
        functools.partial(_sb_knorm_kernel, tk=tk),
        grid=(s // rows,),
        in_specs=[pl.BlockSpec((rows, SB_WIDTH), lambda t: (t, kb))],
        out_specs=pl.BlockSpec((rows // tk, 1, SB_WIDTH), lambda t: (t, 0, 0)),
        out_shape=jax.ShapeDtypeStruct((s // tk, 1, SB_WIDTH), F32),
        scratch_shapes=[pltpu.VMEM((1, SB_WIDTH), F32)],
        compiler_params=_cparams(("arbitrary",)),
        name="sb_key_norms",
    )(proj)


def _sb_kernel(kmax_ref, q_ref, k_ref, v_ref, ntri_ref, o_ref, zbuf, spb, abuf, acc_ref, aft_ref, qn_ref):
    i = pl.program_id(0)
    tq = q_ref.shape[0]
    tk = tq
    blk = LANES
    nsub = tk // blk
    c2 = (SB_DH ** -0.5) * LOG2E
    ntri = ntri_ref[...]
    acc_ref[...] = jnp.zeros(acc_ref.shape, F32)
    aft_ref[...] = jnp.zeros(aft_ref.shape, F32)
    hcols = [slice(hd * SB_DH, (hd + 1) * SB_DH) for hd in range(SB_HEADS)]
    for hd, hc in enumerate(hcols):
        qf = q_ref[:, hc].astype(F32)
        qn = jnp.sqrt(jnp.sum(qf * qf, axis=-1, keepdims=True)) * c2
        qn_ref[hd] = jnp.broadcast_to(qn, (tq, LANES))

    def tile(k0, mask):
        keys = pl.ds(k0, tk)
        for hd, hc in enumerate(hcols):
            zbuf[hd] = _dot_nt(q_ref[:, hc], k_ref[keys, hc]) * c2
        for hd in range(SB_HEADS):
            z2 = zbuf[hd]
            sp2 = jnp.maximum(z2, 0.0) + jnp.log(1.0 + jnp.exp2(-jnp.abs(z2))) * LOG2E
            if mask is not None:
                sp2 = jnp.where(mask, sp2, 0.0)
            spb[hd] = sp2.astype(BF16)
        for hd in range(SB_HEADS):
            aft = aft_ref[hd]
            revs = [None] * nsub
            for sb in reversed(range(nsub)):
                cum = _dot(spb[hd, :, sb * blk:(sb + 1) * blk], ntri)
                revs[sb] = cum[:, 0:blk] + aft
                aft = aft + cum[:, blk:]
            a = jnp.exp2(zbuf[hd] + jnp.concatenate(revs, axis=1))
            if mask is not None:
                a = jnp.where(mask, a, 0.0)
            abuf[hd] = a.astype(BF16)
            aft_ref[hd] = aft
        for hd, hc in enumerate(hcols):
            acc_ref[hd] += _dot(abuf[hd], v_ref[keys, hc])

    def live(kt):
        kmax = kmax_ref[jnp.maximum(kt, 0)]
        bound = None
        for hd, hc in enumerate(hcols):
            b = qn_ref[hd] * kmax[:, hc] + aft_ref[hd]
            bound = b if bound is None else jnp.maximum(bound, b)
        return jnp.max(bound) > -SB_SKIP_LOG2

    row = lax.broadcasted_iota(jnp.int32, (tq, tk), 0)
    col = lax.broadcasted_iota(jnp.int32, (tq, tk), 1)
    tile(pl.multiple_of(i * tq, tq), row > col)

    def cond(carry):
        n, go = carry
        return jnp.logical_and(n < i, go)

    def body(carry):
        n, _ = carry
        kt = i - 1 - n
        tile(pl.multiple_of(kt * tk, tk), None)
        return n + 1, live(kt - 1)

    lax.while_loop(cond, body, (jnp.int32(0), jnp.asarray(True)))
    for hd, hc in enumerate(hcols):
        o_ref[:, hc] = acc_ref[hd].astype(o_ref.dtype)


def _stick_breaking(proj, tq):
    s = proj.shape[0]
    qb = COL_SB_Q // SB_WIDTH
    r = jnp.arange(LANES)
    ntri = -jnp.concatenate([(r[:, None] >= r[None, :]).astype(BF16), jnp.ones((LANES, LANES), BF16)], axis=1)
    kmax = _sb_key_norms(proj, tq)
    resident = lambda col: pl.BlockSpec((s, SB_WIDTH), lambda i: (0, col), pipeline_mode=pl.Buffered(1))
    return pl.pallas_call(
        _sb_kernel,
        grid=(s // tq,),
        in_specs=[
            pl.BlockSpec((s // tq, 1, SB_WIDTH), lambda i: (0, 0, 0)),
            pl.BlockSpec((tq, SB_WIDTH), lambda i: (i, qb)),
            resident(qb + 1),
            resident(qb + 2),
            pl.BlockSpec((LANES, 2 * LANES), lambda i: (0, 0)),
        ],
        out_specs=pl.BlockSpec((tq, SB_WIDTH), lambda i: (i, 0)),
        out_shape=jax.ShapeDtypeStruct((s, SB_WIDTH), BF16),
        scratch_shapes=[
            pltpu.VMEM((SB_HEADS, tq, tq), F32),
            pltpu.VMEM((SB_HEADS, tq, tq), BF16),
            pltpu.VMEM((SB_HEADS, tq, tq), BF16),
            pltpu.VMEM((SB_HEADS, tq, SB_DH), F32),
            pltpu.VMEM((SB_HEADS, tq, LANES), F32),
            pltpu.VMEM((SB_HEADS, tq, LANES), F32),
        ],
        compiler_params=_cparams(("parallel",)),
        name="stick_breaking",
    )(kmax, proj, proj, proj, ntri)


def _merge_kernel(h_ref, g0_ref, g1_ref, g2_ref, ydn_ref, yswa_ref, ysb_ref, wdn_ref, wswa_ref, wsb_ref, wo_ref,
                  o_ref):
    merged = _sigmoid(g0_ref[...].astype(F32)) * _dot(ydn_ref[...], wdn_ref[...])
    merged = merged + _sigmoid(g1_ref[...].astype(F32)) * _dot(yswa_ref[...], wswa_ref[...])
    merged = merged + _sigmoid(g2_ref[...].astype(F32)) * _dot(ysb_ref[...], wsb_ref[...])
    o_ref[...] = h_ref[...] + _dot(merged.astype(BF16), wo_ref[...])


def _merge(h, proj, y_dn, y_swa, y_sb, w_dn, w_swa, w_sb, w_o, l, tm):
    s = h.shape[0]
    rows = lambda width, col=0: pl.BlockSpec((tm, width), lambda i: (i, col))
    const = lambda a: pl.BlockSpec((None,) + a.shape[1:], lambda i: (l, 0, 0), pipeline_mode=pl.Buffered(1))
    return pl.pallas_call(
        _merge_kernel,
        grid=(s // tm,),
        in_specs=[
            rows(D_MODEL), rows(D_MODEL, 0), rows(D_MODEL, 1), rows(D_MODEL, 2),
            rows(DN_WIDTH), rows(SWA_WIDTH), rows(SB_WIDTH),
            const(w_dn), const(w_swa), const(w_sb), const(w_o),
        ],
        out_specs=rows(D_MODEL),
        out_shape=jax.ShapeDtypeStruct((s, D_MODEL), F32),
        compiler_params=_cparams(("parallel",)),
        name="merge",
    )(h, proj, proj, proj, y_dn, y_swa, y_sb, w_dn, w_swa, w_sb, w_o)


def _norm_matmul_kernel(x_ref, g_ref, w_ref, o_ref):
    xn = _rms_rows(x_ref[...], g_ref[...]).astype(BF16)
    o_ref[...] = _dot(xn, w_ref[...]).astype(o_ref.dtype)


def _mem_kv(mem, g, w, l):
    m = mem.shape[0]
    n = w.shape[2]
    return pl.pallas_call(
        _norm_matmul_kernel,
        grid=(1,),
        in_specs=[
            pl.BlockSpec((m, D_MODEL), lambda i: (0, 0)),
            pl.BlockSpec((1, D_MODEL), lambda i: (0, 0)),
            pl.BlockSpec((None, D_MODEL, n), lambda i: (l, 0, 0)),
        ],
        out_specs=pl.BlockSpec((m, n), lambda i: (0, 0)),
        out_shape=jax.ShapeDtypeStruct((m, n), BF16),
        compiler_params=_cparams(("arbitrary",)),
        name="mem_kv",
    )(mem, g, w)


def _xattn_kernel(h_ref, g_ref, kv_ref, wq_ref, wo_ref, o_ref):
    h = h_ref[...]
    hn = _rms_rows(h, g_ref[...]).astype(BF16)
    q = _dot(hn, wq_ref[...]).astype(BF16)
    outs = []
    for hd in range(X_HEADS):
        cs = slice(hd * X_DH, (hd + 1) * X_DH)
        kh = kv_ref[:, cs]
        vh = kv_ref[:, X_WIDTH + hd * X_DH:X_WIDTH + (hd + 1) * X_DH]
        sc = _dot_nt(q[:, cs], kh) * (X_DH ** -0.5)
        mx = jnp.max(sc, axis=-1, keepdims=True)
        p = jnp.exp(sc - mx)
        den = jnp.sum(p, axis=-1, keepdims=True)
        outs.append((_dot(p.astype(BF16), vh) / den).astype(BF16))
    o = jnp.concatenate(outs, axis=1)
    o_ref[...] = h + _dot(o, wo_ref[...])


def _xattn(h, g, kv, w_q, w_o, l, tm):
    s = h.shape[0]
    const = lambda a: pl.BlockSpec(a.shape, lambda i: (0, 0))
    layer = lambda a: pl.BlockSpec((None,) + a.shape[1:], lambda i: (l, 0, 0))
    return pl.pallas_call(
        _xattn_kernel,
        grid=(s // tm,),
        in_specs=[pl.BlockSpec((tm, D_MODEL), lambda i: (i, 0)), const(g), const(kv), layer(w_q), layer(w_o)],
        out_specs=pl.BlockSpec((tm, D_MODEL), lambda i: (i, 0)),
        out_shape=jax.ShapeDtypeStruct((s, D_MODEL), F32),
        compiler_params=_cparams(("parallel",)),
        name="xattn",
    )(h, g, kv, w_q, w_o)


def _ffn_kernel(h_ref, g_ref, wg_ref, wv_ref, cg_ref, cv_ref, wd_ref, gout_ref, o_ref, hn_ref, gbuf, vbuf, carry,
                *, norm_out):
    i = pl.program_id(0)
    j = pl.program_id(1)
    tm = h_ref.shape[0]
    halo = SUBLANES

    @pl.when(j == 0)
    def _():
        h = h_ref[...]
        hn_ref[...] = _rms_rows(h, g_ref[...]).astype(BF16)
        o_ref[...] = h

    @pl.when(i == 0)
    def _():
        carry[j] = jnp.zeros(carry.shape[1:], F32)

    hn = hn_ref[...]
    gbuf[0:halo, :] = carry[j, 0]
    vbuf[0:halo, :] = carry[j, 1]

    gbuf[halo:, :] = _dot(hn, wg_ref[...])
    vbuf[halo:, :] = _dot(hn, wv_ref[...])
    carry[j, 0] = gbuf[tm:tm + halo, :]
    carry[j, 1] = vbuf[tm:tm + halo, :]

    def conv(buf, cw_ref):
        base = halo - (FFN_CONV - 1)
        out = cw_ref[0:1, :] * buf[base:base + tm, :]
        for tap in range(1, FFN_CONV):
            out = out + cw_ref[tap:tap + 1, :] * buf[base + tap:base + tap + tm, :]
        return out

    gate = conv(gbuf, cg_ref)
    val = conv(vbuf, cv_ref)
    act = (gate * _sigmoid(gate) * val).astype(BF16)
    o_ref[...] += _dot(act, wd_ref[...])

    if norm_out:
        @pl.when(j == pl.num_programs(1) - 1)
        def _():
            o_ref[...] = _rms_rows(o_ref[...], gout_ref[...])


def _ffn(h, g, w_up, conv_w, w_down, g_out, l, tm, tn, norm_out):
    s = h.shape[0]
    nj = D_FF // tn
    return pl.pallas_call(
        functools.partial(_ffn_kernel, norm_out=norm_out),
        grid=(s // tm, nj),
        in_specs=[
            pl.BlockSpec((tm, D_MODEL), lambda i, j: (i, 0)),
            pl.BlockSpec((1, D_MODEL), lambda i, j: (0, 0)),
            pl.BlockSpec((None, D_MODEL, tn), lambda i, j: (l, 0, j)),
            pl.BlockSpec((None, D_MODEL, tn), lambda i, j: (l, 0, j + nj)),
            pl.BlockSpec((FFN_CONV, tn), lambda i, j: (0, j)),
            pl.BlockSpec((FFN_CONV, tn), lambda i, j: (0, j + nj)),
            pl.BlockSpec((None, tn, D_MODEL), lambda i, j: (l, j, 0)),
            pl.BlockSpec((1, D_MODEL), lambda i, j: (0, 0)),
        ],
        out_specs=pl.BlockSpec((tm, D_MODEL), lambda i, j: (i, 0)),
        out_shape=jax.ShapeDtypeStruct((s, D_MODEL), F32),
        scratch_shapes=[
            pltpu.VMEM((tm, D_MODEL), BF16),
            pltpu.VMEM((tm + SUBLANES, tn), F32),
            pltpu.VMEM((tm + SUBLANES, tn), F32),
            pltpu.VMEM((nj, 2, SUBLANES, tn), F32),
        ],
        compiler_params=_cparams(("arbitrary", "arbitrary")),
        name="conv_ffn",
    )(h, g, w_up, w_up, conv_w, conv_w, w_down, g_out)


def _tile(s, want):
    return min(s, want)


IN_PROJ_ROWS, IN_PROJ_COLS = 1024, 1792
DN_ROWS = 512
SWA_ROWS = 1024
SB_ROWS = 256
MERGE_ROWS = 512
XATTN_ROWS = 1024
FFN_ROWS, FFN_COLS = 512, 1024


def _w_in_pieces():
    o_z = 2 * DN_QK_WIDTH + DN_WIDTH
    o_a = o_z + DN_WIDTH
    o_swq = o_a + 2 * DN_HEADS
    o_swkv = o_swq + SWA_WIDTH
    o_sb = o_swkv + 2 * SWA_KV_WIDTH
    o_gate = o_sb + 3 * SB_WIDTH
    o_end = o_gate + N_BRANCH * D_MODEL
    big = [(o_gate, o_end), (0, o_a), (o_swq, o_swkv), (o_sb, o_gate), (o_swkv, o_sb)]
    return big, (o_a, o_swq), o_end


def _reorder_w_in(w_in):
    pieces, (a0, a1), _ = _w_in_pieces()
    wt = jnp.swapaxes(w_in, 1, 2)
    big = jnp.concatenate([wt[:, lo:hi].astype(BF16) for lo, hi in pieces], axis=1)
    ab = jnp.pad(wt[:, a0:a1], ((0, 0), (0, AB_COLS - (a1 - a0)), (0, 0))).astype(BF16)
    return big, ab


def kernel(x, mem, norm_mix, w_in, dn_conv, dn_a_log, dn_dt_bias, dn_norm, swa_sinks, w_br_dn, w_br_swa, w_br_sb, w_o, norm_xattn, norm_mem, w_xq, w_xkv, w_xo, norm_ffn, w_up, ffn_conv, w_down, norm_final):
    depth = w_in.shape[0]
    s = x.shape[1]
    assert x.shape[0] == 1 and s % 256 == 0
    h = x[0]
    mem2 = mem[0]

    w_big, w_ab = _reorder_w_in(w_in)
    w_br_dn_b = w_br_dn.astype(BF16)
    w_br_swa_b = w_br_swa.astype(BF16)
    w_br_sb_b = w_br_sb.astype(BF16)
    w_o_b = w_o.astype(BF16)
    w_xq_b = w_xq.astype(BF16)
    w_xkv_b = w_xkv.astype(BF16)
    w_xo_b = w_xo.astype(BF16)
    w_up_b = w_up.astype(BF16)
    w_down_b = w_down.astype(BF16)
    pad8 = lambda a: jnp.pad(a, ((0, 0), (0, AB_COLS - a.shape[-1])))
    a_log_pad = pad8(dn_a_log)
    dt_bias_pad = pad8(dn_dt_bias)
    swa_bias = _swa_bias()

    for l in range(depth):
        proj, ab = _in_proj(h, norm_mix[l][None], w_big, w_ab, l, tm=_tile(s, IN_PROJ_ROWS), tn=IN_PROJ_COLS)
        y_dn = _delta_net(proj, ab, dn_conv[l], a_log_pad[l][None], dt_bias_pad[l][None], dn_norm[l][None],
                          t=_tile(s, DN_ROWS))
        y_swa = _swa(proj, swa_sinks[l], swa_bias, tq=_tile(s, SWA_ROWS))
        y_sb = _stick_breaking(proj, tq=_tile(s, SB_ROWS))
        h = _merge(h, proj, y_dn, y_swa, y_sb, w_br_dn_b, w_br_swa_b, w_br_sb_b, w_o_b, l,
                   tm=_tile(s, MERGE_ROWS))
        kv = _mem_kv(mem2, norm_mem[l][None], w_xkv_b, l)
        h = _xattn(h, norm_xattn[l][None], kv, w_xq_b, w_xo_b, l, tm=_tile(s, XATTN_ROWS))
        h = _ffn(h, norm_ffn[l][None], w_up_b, ffn_conv[l], w_down_b, norm_final[None], l,
                 tm=_tile(s, FFN_ROWS), tn=FFN_COLS, norm_out=(l == depth - 1))
    return h[None]
```
